```python
import math
import jax
import jax.numpy as jnp
from jax import lax
import numpy as np

D_MODEL = 2048
BATCH = 2
SEQ = 16384
DEPTH = 1

CTX_LEN = 256
GRID_W = 64
EPS = 1e-6
F32 = jnp.float32

S5_WIDTH = D_MODEL // 2
S5_GROUP = 16
S5_GROUPS = S5_WIDTH // S5_GROUP
S5_STATE = 64
SCAN_CHUNK = 128

LRU_WIDTH = D_MODEL // 2
LRU_HEADS = 16
LRU_BLOCK = LRU_WIDTH // LRU_HEADS
LRU_C = 8.0
CONV_W = 4
CONV_PAD_LO = (CONV_W - 1) // 2
CONV_PAD_HI = CONV_W - 1 - CONV_PAD_LO

N_KEYS = 128
N_EXPERTS = N_KEYS * N_KEYS
PEER_HEADS = 8
PEER_TOPK = 16
PEER_DKEY = 256
PEER_HALF = PEER_DKEY // 2
TOKEN_BLOCK = 128

IN_WIDTH = S5_WIDTH + 2 * LRU_WIDTH + 2 * D_MODEL

kernel_name = 'hybrid_s5_rglru_peer_dit'


def rmsnorm(t, g):
    tf = t.astype(F32)
    y = tf * lax.rsqrt(jnp.mean(tf * tf, axis=-1, keepdims=True) + EPS)
    return (y * g.astype(F32)).astype(t.dtype)


def modulate(t, shift, scale):
    return t * (1 + scale) + shift


def flip_seq(t, rev):
    return jnp.flip(t, axis=1) if rev else t


def linear_combine(e1, e2):
    a1, b1 = e1
    a2, b2 = e2
    return a1 * a2, a2 * b1 + b2


def to_colmajor(t, rows):
    bsz, _, w = t.shape
    return t.reshape(bsz, rows, GRID_W, w).transpose(0, 2, 1, 3)


def from_colmajor(t):
    bsz, gw, rows, w = t.shape
    return t.transpose(0, 2, 1, 3).reshape(bsz, rows * gw, w)


def dwconv(t, w, b):
    out = lax.conv_general_dilated(t, w[:, None, :], window_strides=(1,),
                                   padding=[(CONV_PAD_LO, CONV_PAD_HI)],
                                   dimension_numbers=('NWC', 'WIO', 'NWC'),
                                   feature_group_count=t.shape[-1])
    return out + b


def s5_scan(u, h0, a_bar, b_bar, c_mat, with_output):
    bsz, length = u.shape[0], u.shape[1]
    n_chunks = length // SCAN_CHUNK
    u_chunks = u.reshape(bsz, n_chunks, SCAN_CHUNK, S5_GROUPS, S5_GROUP).swapaxes(0, 1)
    a_seq = jnp.broadcast_to(a_bar, (bsz, SCAN_CHUNK, S5_GROUPS, S5_STATE))

    def step(h, u_blk):
        bu = jnp.einsum('btgh,gph->btgp', u_blk.astype(jnp.complex64), b_bar)
        bu = bu.at[:, 0].add(a_bar * h)
        _, hs = lax.associative_scan(linear_combine, (a_seq, bu), axis=1)
        y = jnp.einsum('btgp,ghp->btgh', hs, c_mat).real if with_output else None
        return hs[:, -1], y

    h_last, y = lax.scan(step, h0, u_chunks)
    if with_output:
        y = y.swapaxes(0, 1).reshape(bsz, length, S5_GROUPS, S5_GROUP)
    return y, h_last


def s5_mixer(u_lat, u_ctx, need_ctx, a_re, a_im, log_dt, b_re, b_im, c_re, c_im, d_skip, w_glu):
    dtype = u_lat.dtype
    bsz = u_lat.shape[0]
    ul = u_lat.astype(F32).reshape(bsz, -1, S5_GROUPS, S5_GROUP)
    uc = u_ctx.astype(F32).reshape(bsz, -1, S5_GROUPS, S5_GROUP)
    d_g = d_skip.astype(F32).reshape(S5_GROUPS, S5_GROUP)
    y_lat = d_g * ul
    y_ctx = d_g * uc if need_ctx else None
    for d in range(2):
        rev = d == 1
        lam = lax.complex(a_re[d].astype(F32), a_im[d].astype(F32))
        dt = jnp.exp(log_dt[d].astype(F32))[:, None]
        a_bar = jnp.exp(lam * dt)
        b_bar = ((a_bar - 1.0) / lam)[..., None] * lax.complex(b_re[d].astype(F32), b_im[d].astype(F32))
        c_mat = lax.complex(c_re[d].astype(F32), c_im[d].astype(F32))
        h0 = jnp.zeros((bsz, S5_GROUPS, S5_STATE), jnp.complex64)
        yc, hc = s5_scan(flip_seq(uc, rev), h0, a_bar, b_bar, c_mat, need_ctx)
        yl, _ = s5_scan(flip_seq(ul, rev), hc, a_bar, b_bar, c_mat, True)
        y_lat = y_lat + flip_seq(yl, rev)
        if need_ctx:
            y_ctx = y_ctx + flip_seq(yc, rev)

    def glu(y):
        z = jax.nn.gelu(y.reshape(y.shape[0], y.shape[1], S5_WIDTH).astype(dtype))
        return z * jax.nn.sigmoid(z @ w_glu)

    return glu(y_lat), (glu(y_ctx) if need_ctx else None)


def rglru_scan(x, h0, lam, w_r, b_r, w_i, b_i):
    xb = x.reshape(x.shape[0], x.shape[1], LRU_HEADS, LRU_BLOCK)
    r = jax.nn.sigmoid(jnp.einsum('blhi,hij->blhj', xb, w_r.astype(F32)).reshape(x.shape) + b_r.astype(F32))
    i = jax.nn.sigmoid(jnp.einsum('blhi,hij->blhj', xb, w_i.astype(F32)).reshape(x.shape) + b_i.astype(F32))
    log_a = -LRU_C * r * jax.nn.softplus(-lam.astype(F32))
    a = jnp.exp(log_a)
    bx = jnp.sqrt(-jnp.expm1(2.0 * log_a)) * (i * x)
    bx = bx.at[:, 0].add(a[:, 0] * h0)
    _, h = lax.associative_scan(linear_combine, (a, bx), axis=1)
    return h, h[:, -1]


def lru_mixer(v_lat, v_ctx, rows, need_ctx, conv_w, conv_b, lam, w_r, b_r, w_i, b_i):
    dtype = v_lat.dtype
    bsz, length, width = v_lat.shape
    vl = to_colmajor(v_lat, rows).reshape(bsz * GRID_W, rows, width)
    xl = dwconv(vl, conv_w, conv_b).reshape(bsz, length, width).astype(F32)
    xc = dwconv(v_ctx, conv_w, conv_b).astype(F32)
    ys_lat = []
    ys_ctx = []
    for d in range(2):
        rev = d == 1
        h0 = jnp.zeros((bsz, width), F32)
        hc, hc_last = rglru_scan(flip_seq(xc, rev), h0, lam[d], w_r[d], b_r[d], w_i[d], b_i[d])
        hl, _ = rglru_scan(flip_seq(xl, rev), hc_last, lam[d], w_r[d], b_r[d], w_i[d], b_i[d])
        ys_lat.append(flip_seq(hl, rev))
        if need_ctx:
            ys_ctx.append(flip_seq(hc, rev))
    y_lat = from_colmajor((ys_lat[0] + ys_lat[1]).reshape(bsz, GRID_W, rows, width))
    y_ctx = (ys_ctx[0] + ys_ctx[1]).astype(dtype) if need_ctx else None
    return y_lat.astype(dtype), y_ctx


def mixer_block(n_lat, n_ctx, rows, need_ctx, w_in, s5_a_re, s5_a_im, s5_log_dt, s5_b_re, s5_b_im,
                s5_c_re, s5_c_im, s5_d, s5_w_glu, lru_conv_w, lru_conv_b, lru_lambda, lru_w_r, lru_b_r,
                lru_w_i, lru_b_i, w_proj_a, w_proj_b, w_out):
    o_v = S5_WIDTH
    o_g = o_v + LRU_WIDTH
    o_ga = o_g + LRU_WIDTH
    o_gb = o_ga + D_MODEL
    z_lat = n_lat @ w_in
    z_ctx = n_ctx @ (w_in if need_ctx else w_in[:, :o_g])
    ya_lat, ya_ctx = s5_mixer(z_lat[..., :o_v], z_ctx[..., :o_v], need_ctx, s5_a_re, s5_a_im, s5_log_dt,
                              s5_b_re, s5_b_im, s5_c_re, s5_c_im, s5_d, s5_w_glu)
    yb_lat, yb_ctx = lru_mixer(z_lat[..., o_v:o_g], z_ctx[..., o_v:o_g], rows, need_ctx, lru_conv_w,
                               lru_conv_b, lru_lambda, lru_w_r, lru_b_r, lru_w_i, lru_b_i)

    def merge(z, ya, yb):
        yb = yb * jax.nn.gelu(z[..., o_g:o_ga])
        m = jax.nn.sigmoid(z[..., o_ga:o_gb]) * (ya @ w_proj_a) + jax.nn.sigmoid(z[..., o_gb:]) * (yb @ w_proj_b)
        return m @ w_out

    out_lat = merge(z_lat, ya_lat, yb_lat)
    out_ctx = merge(z_ctx, ya_ctx, yb_ctx) if need_ctx else None
    return out_lat, out_ctx


def peer_ffn(t, w_query, sub_keys, expert_u, expert_v):
    shape = t.shape
    tokens = t.reshape(-1, TOKEN_BLOCK, D_MODEL)
    sk = sub_keys.astype(F32)
    kk = PEER_TOPK * PEER_TOPK

    def block(xb):
        q = (xb @ w_query).astype(F32).reshape(TOKEN_BLOCK, PEER_HEADS, 2, PEER_HALF)
        s = jnp.einsum('thsc,snc->thsn', q, sk)
        sv, si = lax.top_k(s, PEER_TOPK)
        cand = (sv[:, :, 0, :, None] + sv[:, :, 1, None, :]).reshape(TOKEN_BLOCK, PEER_HEADS, kk)
        cand_idx = (si[:, :, 0, :, None] * N_KEYS + si[:, :, 1, None, :]).reshape(TOKEN_BLOCK, PEER_HEADS, kk)
        top_s, top_pos = lax.top_k(cand, PEER_TOPK)
        e_idx = jnp.take_along_axis(cand_idx, top_pos, axis=-1)
        g = jax.nn.softmax(top_s, axis=-1)
        u = expert_u[e_idx]
        v = expert_v[e_idx]
        act = jax.nn.gelu(jnp.einsum('td,thkd->thk', xb, u).astype(F32))
        return jnp.einsum('thk,thkd->td', (g * act).astype(t.dtype), v)

    return lax.map(block, tokens).reshape(shape)


def setup_inputs(seed: int = 0) -> dict:
    key = jax.random.key(seed)
    k = jax.random.split(key, 40)

    def nrm(i, shape, scale):
        return jax.random.normal(k[i], shape, F32) * scale

    lam_im = jnp.broadcast_to(math.pi * jnp.arange(S5_STATE, dtype=F32), (DEPTH, 2, S5_GROUPS, S5_STATE))
    a_pow = jax.random.uniform(k[20], (DEPTH, 2, LRU_WIDTH), F32, minval=0.9, maxval=0.999)
    a_base = a_pow ** (1.0 / LRU_C)
    return {
        'x': nrm(0, (BATCH, SEQ, D_MODEL), 1.0),
        'c': nrm(1, (BATCH, D_MODEL), 1.0),
        'ctx': nrm(2, (BATCH, CTX_LEN, D_MODEL), 1.0),
        'c_ctx': nrm(3, (D_MODEL,), 1.0),
        'w_ada': nrm(4, (DEPTH, D_MODEL, 6 * D_MODEL), 0.5 * D_MODEL ** -0.5),
        'b_ada': nrm(5, (DEPTH, 6 * D_MODEL), 0.01),
        'g_norm1': 1.0 + nrm(6, (DEPTH, D_MODEL), 0.02),
        'w_in': nrm(7, (DEPTH, D_MODEL, IN_WIDTH), D_MODEL ** -0.5),
        's5_a_re': -0.5 + nrm(8, (DEPTH, 2, S5_GROUPS, S5_STATE), 0.01),
        's5_a_im': lam_im + nrm(9, (DEPTH, 2, S5_GROUPS, S5_STATE), 0.01),
        's5_log_dt': jax.random.uniform(k[10], (DEPTH, 2, S5_GROUPS), F32, minval=math.log(0.001), maxval=math.log(0.1)),
        's5_b_re': nrm(11, (DEPTH, 2, S5_GROUPS, S5_STATE, S5_GROUP), (2 * S5_GROUP) ** -0.5),
        's5_b_im': nrm(12, (DEPTH, 2, S5_GROUPS, S5_STATE, S5_GROUP), (2 * S5_GROUP) ** -0.5),
        's5_c_re': nrm(13, (DEPTH, 2, S5_GROUPS, S5_GROUP, S5_STATE), (2 * S5_STATE) ** -0.5),
        's5_c_im': nrm(14, (DEPTH, 2, S5_GROUPS, S5_GROUP, S5_STATE), (2 * S5_STATE) ** -0.5),
        's5_d': nrm(15, (DEPTH, S5_WIDTH), 1.0),
        's5_w_glu': nrm(16, (DEPTH, S5_WIDTH, S5_WIDTH), S5_WIDTH ** -0.5),
        'lru_conv_w': nrm(17, (DEPTH, CONV_W, LRU_WIDTH), CONV_W ** -0.5),
        'lru_conv_b': nrm(18, (DEPTH, LRU_WIDTH), 0.01),
        'lru_lambda': jnp.log(a_base) - jnp.log1p(-a_base),
        'lru_w_r': nrm(21, (DEPTH, 2, LRU_HEADS, LRU_BLOCK, LRU_BLOCK), LRU_BLOCK ** -0.5),
        'lru_b_r': nrm(22, (DEPTH, 2, LRU_WIDTH), 0.01),
        'lru_w_i': nrm(23, (DEPTH, 2, LRU_HEADS, LRU_BLOCK, LRU_BLOCK), LRU_BLOCK ** -0.5),
        'lru_b_i': nrm(24, (DEPTH, 2, LRU_WIDTH), 0.01),
        'w_proj_a': nrm(25, (DEPTH, S5_WIDTH, D_MODEL), S5_WIDTH ** -0.5),
        'w_proj_b': nrm(26, (DEPTH, LRU_WIDTH, D_MODEL), LRU_WIDTH ** -0.5),
        'w_out': nrm(27, (DEPTH, D_MODEL, D_MODEL), D_MODEL ** -0.5),
        'g_norm2': 1.0 + nrm(28, (DEPTH, D_MODEL), 0.02),
        'peer_w_query': nrm(29, (DEPTH, D_MODEL, PEER_HEADS * PEER_DKEY), D_MODEL ** -0.5),
        'peer_sub_keys': nrm(30, (DEPTH, 2, N_KEYS, PEER_HALF), PEER_HALF ** -0.5),
        'peer_u': nrm(31, (DEPTH, N_EXPERTS, D_MODEL), D_MODEL ** -0.5),
        'peer_v': nrm(32, (DEPTH, N_EXPERTS, D_MODEL), PEER_HEADS ** -0.5),
        'g_final': 1.0 + nrm(33, (D_MODEL,), 0.02),
    }


def reference(x, c, ctx, c_ctx, w_ada, b_ada, g_norm1, w_in, s5_a_re, s5_a_im, s5_log_dt, s5_b_re,
              s5_b_im, s5_c_re, s5_c_im, s5_d, s5_w_glu, lru_conv_w, lru_conv_b, lru_lambda, lru_w_r,
              lru_b_r, lru_w_i, lru_b_i, w_proj_a, w_proj_b, w_out, g_norm2, peer_w_query, peer_sub_keys,
              peer_u, peer_v, g_final):
    rows = x.shape[1] // GRID_W
    sc_lat = jax.nn.silu(c)
    sc_ctx = jax.nn.silu(c_ctx)
    h_lat = x
    h_ctx = ctx
    for layer in range(DEPTH):
        need_ctx = layer < DEPTH - 1
        mod_l = jnp.split((sc_lat @ w_ada[layer] + b_ada[layer])[:, None, :], 6, axis=-1)
        mod_c = jnp.split((sc_ctx @ w_ada[layer] + b_ada[layer])[None, None, :], 6, axis=-1)

        n_lat = modulate(rmsnorm(h_lat, g_norm1[layer]), mod_l[0], mod_l[1])
        n_ctx = modulate(rmsnorm(h_ctx, g_norm1[layer]), mod_c[0], mod_c[1])
        mix_lat, mix_ctx = mixer_block(
            n_lat, n_ctx, rows, need_ctx, w_in[layer], s5_a_re[layer], s5_a_im[layer], s5_log_dt[layer],
            s5_b_re[layer], s5_b_im[layer], s5_c_re[layer], s5_c_im[layer], s5_d[layer], s5_w_glu[layer],
            lru_conv_w[layer], lru_conv_b[layer], lru_lambda[layer], lru_w_r[layer], lru_b_r[layer],
            lru_w_i[layer], lru_b_i[layer], w_proj_a[layer], w_proj_b[layer], w_out[layer])
        h_lat = h_lat + mod_l[2] * mix_lat

        f_lat = modulate(rmsnorm(h_lat, g_norm2[layer]), mod_l[3], mod_l[4])
        h_lat = h_lat + mod_l[5] * peer_ffn(f_lat, peer_w_query[layer], peer_sub_keys[layer], peer_u[layer], peer_v[layer])

        if need_ctx:
            h_ctx = h_ctx + mod_c[2] * mix_ctx
            f_ctx = modulate(rmsnorm(h_ctx, g_norm2[layer]), mod_c[3], mod_c[4])
            h_ctx = h_ctx + mod_c[5] * peer_ffn(f_ctx, peer_w_query[layer], peer_sub_keys[layer], peer_u[layer], peer_v[layer])
    return rmsnorm(h_lat, g_final)
```

```python
import functools
import math

import jax
import jax.numpy as jnp
from jax import lax
from jax.experimental import pallas as pl
from jax.experimental.pallas import tpu as pltpu

F32 = jnp.float32
BF16 = jnp.bfloat16
EPS = 1e-6
GRID_W = 64
S5_GROUP = 16
S5_STATE = 64
S5_CHUNK = 16
LRU_HEADS = 16
LRU_C = 8.0
CONV_W = 4
CONV_PAD_LO = (CONV_W - 1) // 2
N_KEYS = 128
PEER_HEADS = 8
PEER_TOPK = 16
COLS_PER_TILE = 8
MXU_TILE = 256
VMEM_LIMIT = 56 * 1024 * 1024


def _params(sem, vmem=VMEM_LIMIT):
    return pltpu.CompilerParams(dimension_semantics=sem, vmem_limit_bytes=vmem)


def _rms(x, g):
    return x * lax.rsqrt(jnp.mean(x * x, axis=-1, keepdims=True) + EPS) * g


def _ada_kernel(c_ref, w_ref, b_ref, o_ref):
    c = c_ref[...]
    sc = c * jax.nn.sigmoid(c)
    o_ref[...] = jnp.dot(sc.astype(BF16), w_ref[...].astype(BF16),
                         preferred_element_type=F32) + b_ref[...]


def _ada(cin, w, b):
    d, n = w.shape
    tn = 1536
    return pl.pallas_call(
        _ada_kernel,
        grid=(n // tn,),
        in_specs=[pl.BlockSpec((8, d), lambda j: (0, 0)),
                  pl.BlockSpec((d, tn), lambda j: (0, j)),
                  pl.BlockSpec((1, tn), lambda j: (0, j))],
        out_specs=pl.BlockSpec((8, tn), lambda j: (0, j)),
        out_shape=jax.ShapeDtypeStruct((8, n), F32),
        compiler_params=_params(("arbitrary",)),
        name="ada",
    )(cin, w, b.reshape(1, n))


def _inproj_kernel(x_ref, g_ref, sh_ref, sc_ref, w_ref, *refs, n_u, n_v):
    if len(refs) == 4:
        zu_ref, zv_ref, zg_ref, n_scr = refs
    else:
        zu_ref, zv_ref, n_scr = refs
        zg_ref = None
    j = pl.program_id(2)

    @pl.when(j == 0)
    def _():
        y = _rms(x_ref[...], g_ref[...])
        n_scr[...] = (y * (1.0 + sc_ref[...]) + sh_ref[...]).astype(BF16)

    z = jnp.dot(n_scr[...], w_ref[...], preferred_element_type=F32)

    @pl.when(j < n_u)
    def _():
        zu_ref[...] = z.astype(zu_ref.dtype)

    @pl.when((j >= n_u) & (j < n_u + n_v))
    def _():
        zv_ref[...] = z

    if zg_ref is not None:
        @pl.when(j >= n_u + n_v)
        def _():
            zg_ref[...] = z.astype(zg_ref.dtype)


def _inproj(x, g, mod6, mod_row, w_bf16, width_u, width_v):
    bsz, length, d = x.shape
    n = w_bf16.shape[1]
    tm = min(512, length)
    tn = 1024
    n_u, n_v = width_u // tn, width_v // tn
    n_g = n // tn - n_u - n_v
    kern = functools.partial(_inproj_kernel, n_u=n_u, n_v=n_v)
    row = lambda k: (lambda b, i, j: (mod_row(b) * 6 + k, 0, 0))
    out_specs = [pl.BlockSpec((None, tm, tn), lambda b, i, j: (b, i, jnp.minimum(j, n_u - 1))),
                 pl.BlockSpec((None, tm, tn), lambda b, i, j: (b, i, jnp.clip(j - n_u, 0, n_v - 1)))]
    out_shape = [jax.ShapeDtypeStruct((bsz, length, width_u), BF16),
                 jax.ShapeDtypeStruct((bsz, length, width_v), F32)]
    if n_g:
        out_specs.append(pl.BlockSpec((None, tm, tn), lambda b, i, j: (b, i, jnp.maximum(j - n_u - n_v, 0))))
        out_shape.append(jax.ShapeDtypeStruct((bsz, length, n_g * tn), BF16))
    return pl.pallas_call(
        kern,
        grid=(bsz, length // tm, n // tn),
        in_specs=[pl.BlockSpec((None, tm, d), lambda b, i, j: (b, i, 0)),
                  pl.BlockSpec((1, d), lambda b, i, j: (0, 0)),
                  pl.BlockSpec((None, 1, d), row(0)),
                  pl.BlockSpec((None, 1, d), row(1)),
                  pl.BlockSpec((d, tn), lambda b, i, j: (0, j))],
        out_specs=out_specs,
        out_shape=out_shape,
        scratch_shapes=[pltpu.VMEM((tm, d), BF16)],
        compiler_params=_params(("arbitrary", "arbitrary", "arbitrary")),
        name="inproj",
    )(x, g.reshape(1, d), mod6, mod6, w_bf16)


def _s5_prep_kernel(are_ref, aim_ref, ldt_ref, bre_ref, bim_ref, cre_ref, cim_ref,
                    k_ref, cg_ref, bg_ref, a16_ref):
    nk = S5_CHUNK + 1
    dt = jnp.exp(ldt_ref[...])
    lre, lim = are_ref[...], aim_ref[...]
    xr, xi = lre * dt, lim * dt
    kf = jnp.right_shift(lax.broadcasted_iota(jnp.int32, (nk * S5_GROUP, 1), 0),
                         int(math.log2(S5_GROUP))).astype(F32)
    mag = jnp.exp(kf * xr)
    pr, pim = mag * jnp.cos(kf * xi), mag * jnp.sin(kf * xi)
    e1 = jnp.exp(xr)
    ar, ai = e1 * jnp.cos(xi), e1 * jnp.sin(xi)
    den = lre * lre + lim * lim
    qr = ((ar - 1.0) * lre + ai * lim) / den
    qi = (ai * lre - (ar - 1.0) * lim) / den
    bre, bim = bre_ref[...], bim_ref[...]
    bbr, bbi = qr * bre - qi * bim, qr * bim + qi * bre
    tile = lambda m: jnp.concatenate([m] * nk, axis=0)
    cr_t, ci_t, br_t, bi_t = tile(cre_ref[...]), tile(cim_ref[...]), tile(bbr), tile(bbi)
    cg = jnp.concatenate([cr_t * pr - ci_t * pim, -(cr_t * pim + ci_t * pr)], axis=1)
    bg = jnp.concatenate([br_t * pr - bi_t * pim, br_t * pim + bi_t * pr], axis=1)
    bcat = jnp.concatenate([bbr, bbi], axis=1)
    cg_ref[...] = cg
    bg_ref[...] = bg
    k_ref[...] = lax.dot_general(cg[:S5_CHUNK * S5_GROUP], bcat, (((1,), (1,)), ((), ())),
                                 precision=lax.Precision.HIGHEST, preferred_element_type=F32)
    n = S5_CHUNK * S5_GROUP
    a16_ref[...] = jnp.concatenate([pr[n:n + 1], pim[n:n + 1]], axis=1)


def _s5_operators(a_re, a_im, log_dt, b_re, b_im, c_re, c_im, d_skip):
    _, groups, p = a_re.shape
    h = S5_GROUP
    dg = 2 * groups
    nk = S5_CHUNK + 1
    vec = lambda t: t.reshape(dg, 1, p)
    tr = lambda t: jnp.swapaxes(t, -1, -2).reshape(dg, h, p)
    ldt = jnp.broadcast_to(log_dt[..., None], (2, groups, p))
    spec_v = pl.BlockSpec((None, 1, p), lambda i: (i, 0, 0))
    spec_m = pl.BlockSpec((None, h, p), lambda i: (i, 0, 0))
    kfl, cg, bg, a16 = pl.pallas_call(
        _s5_prep_kernel,
        grid=(dg,),
        in_specs=[spec_v, spec_v, spec_v, spec_m, spec_m, spec_m, spec_m],
        out_specs=[pl.BlockSpec((None, S5_CHUNK * h, h), lambda i: (i, 0, 0)),
                   pl.BlockSpec((None, nk * h, 2 * p), lambda i: (i, 0, 0)),
                   pl.BlockSpec((None, nk * h, 2 * p), lambda i: (i, 0, 0)),
                   pl.BlockSpec((None, 1, 2 * p), lambda i: (i, 0, 0))],
        out_shape=[jax.ShapeDtypeStruct((dg, S5_CHUNK * h, h), F32),
                   jax.ShapeDtypeStruct((dg, nk * h, 2 * p), F32),
                   jax.ShapeDtypeStruct((dg, nk * h, 2 * p), F32),
                   jax.ShapeDtypeStruct((dg, 1, 2 * p), F32)],
        compiler_params=_params(("arbitrary",)),
        name="s5_prep",
    )(vec(a_re), vec(a_im), vec(ldt), tr(b_re), tr(b_im),
      c_re.reshape(dg, h, p), c_im.reshape(dg, h, p))

    t = S5_CHUNK
    kfl = kfl.reshape(2, groups, t, h, h)
    cg = cg.reshape(2, groups, nk, h, 2 * p)
    bg = bg.reshape(2, groups, nk, h, 2 * p)
    ti = jnp.arange(t)
    lag = ti[:, None] - ti[None, :]
    kf = jnp.where((lag >= 0)[None, :, :, None, None], kfl[0][:, jnp.clip(lag, 0, t - 1)], 0.0)
    kb = jnp.where((lag <= 0)[None, :, :, None, None], kfl[1][:, jnp.clip(-lag, 0, t - 1)], 0.0)
    eye = (lag == 0)[None, :, :, None, None] * jnp.eye(h, dtype=F32)[None, None, None]
    m = kf + kb + eye * d_skip.reshape(groups, 1, 1, 1, h)
    mt = m.transpose(0, 2, 4, 1, 3).reshape(groups, t * h, t * h)
    bp_f = bg[0][:, t - 1 - ti].reshape(groups, t * h, 2 * p)
    bp_b = bg[1][:, ti].reshape(groups, t * h, 2 * p)
    bp = jnp.concatenate([bp_f, bp_b], axis=-1)
    cp_f = cg[0][:, ti + 1].reshape(groups, t * h, 2 * p)
    cp_b = cg[1][:, t - ti].reshape(groups, t * h, 2 * p)
    cp = jnp.concatenate([cp_f, cp_b], axis=-1).swapaxes(1, 2)
    return mt.astype(BF16), bp.astype(BF16), cp.astype(BF16), a16.reshape(2, groups, 1, 2 * p)


def _cplx_coef(a):
    p = a.shape[1] // 2
    lane = lax.broadcasted_iota(jnp.int32, a.shape, 1)
    sw = pltpu.roll(a, p, axis=1)
    return jnp.where(lane < p, a, sw), jnp.where(lane < p, -sw, a)


def _cplx_mul(a, x):
    c1, c2 = _cplx_coef(a)
    return c1 * x + c2 * pltpu.roll(x, x.shape[1] // 2, axis=1)


def _chunk_scan(x, a, reverse):
    n = x.shape[0]
    row = lax.broadcasted_iota(jnp.int32, (n, 1), 0)
    o = 1
    while o < n:
        if reverse:
            sh = jnp.where(row < n - o, pltpu.roll(x, n - o, axis=0), 0.0)
        else:
            sh = jnp.where(row >= o, pltpu.roll(x, o, axis=0), 0.0)
        x = x + _cplx_mul(a, sh)
        a = _cplx_mul(a, a)
        o *= 2
    return x


def _s5_kernel(u_ref, uc_ref, mt_ref, bp_ref, cp_ref, af_ref, ab_ref, y_ref):
    u = u_ref[...]
    nc = u.shape[0]
    ps = af_ref.shape[1]
    af, ab = af_ref[...], ab_ref[...]
    v = jnp.dot(u, bp_ref[...], preferred_element_type=F32)
    vc = jnp.dot(uc_ref[...], bp_ref[...], preferred_element_type=F32)
    ncc = vc.shape[0]
    hcf = _chunk_scan(vc[:, :ps], af, False)[ncc - 1:ncc]
    hcb = _chunk_scan(vc[:, ps:], ab, True)[0:1]
    row = lax.broadcasted_iota(jnp.int32, (nc, 1), 0)
    vf = v[:, :ps] + jnp.where(row == 0, _cplx_mul(af, hcf), 0.0)
    vb = v[:, ps:] + jnp.where(row == nc - 1, _cplx_mul(ab, hcb), 0.0)
    sf = _chunk_scan(vf, af, False)
    sb = _chunk_scan(vb, ab, True)
    sin_f = jnp.where(row == 0, hcf, pltpu.roll(sf, 1, axis=0))
    sin_b = jnp.where(row == nc - 1, hcb, pltpu.roll(sb, nc - 1, axis=0))
    s_in = jnp.concatenate([sin_f, sin_b], axis=1).astype(BF16)
    y = (jnp.dot(u, mt_ref[...], preferred_element_type=F32)
         + jnp.dot(s_in, cp_ref[...], preferred_element_type=F32))
    y_ref[...] = y.astype(y_ref.dtype)


def _s5(u, uc, mt, bp, cp, a16):
    bsz, groups, nc, w = u.shape
    ncc = uc.shape[2]
    ps = a16.shape[-1]
    wspec = pl.BlockSpec((None, w, w), lambda b, g: (g, 0, 0))
    return pl.pallas_call(
        _s5_kernel,
        grid=(bsz, groups),
        in_specs=[pl.BlockSpec((None, None, nc, w), lambda b, g: (b, g, 0, 0)),
                  pl.BlockSpec((None, None, ncc, w), lambda b, g: (b, g, 0, 0)),
                  wspec, wspec, wspec,
                  pl.BlockSpec((None, None, 1, ps), lambda b, g: (0, g, 0, 0)),
                  pl.BlockSpec((None, None, 1, ps), lambda b, g: (1, g, 0, 0))],
        out_specs=pl.BlockSpec((None, None, nc, w), lambda b, g: (b, g, 0, 0)),
        out_shape=jax.ShapeDtypeStruct((bsz, groups, nc, w), BF16),
        compiler_params=_params(("arbitrary", "arbitrary")),
        name="s5",
    )(u, uc, mt, bp, cp, a16, a16)


def _to_chunks(t):
    bsz, length, w = t.shape
    g = w // S5_GROUP
    t = t.reshape(bsz, length // S5_CHUNK, S5_CHUNK, g, S5_GROUP)
    return t.transpose(0, 3, 1, 2, 4).reshape(bsz, g, length // S5_CHUNK, S5_CHUNK * S5_GROUP)


def _from_chunks(t):
    bsz, g, nc, _ = t.shape
    t = t.reshape(bsz, g, nc, S5_CHUNK, S5_GROUP)
    return t.transpose(0, 2, 3, 1, 4).reshape(bsz, nc * S5_CHUNK, g * S5_GROUP)


LRU_ROW_BLOCK = 16


def _neg_expm1(y, exp_y):
    series = -y * (1.0 + y * (0.5 + y * (1.0 / 6.0 + y * (1.0 / 24.0))))
    return jnp.where(y > -0.01, series, 1.0 - exp_y)


def _lru_kernel(v_ref, cw_ref, cb_ref, wr_ref, br_ref, wi_ref, bi_ref, lam_ref, h0_ref, o_ref,
                vpad, a_scr, b_scr, carry_scr, *, rows, rev, chain):
    ct = v_ref.shape[-1]
    cpt = COLS_PER_TILE
    rb = LRU_ROW_BLOCK
    nblk = rows // rb
    nslab = ct // MXU_TILE

    if chain:
        @pl.when(pl.program_id(2) == 0)
        def _():
            carry_scr[...] = h0_ref[...]

    zero_row = jnp.zeros((cpt, ct), F32)
    for k in range(CONV_PAD_LO):
        vpad[k] = zero_row
    for k in range(CONV_W - 1 - CONV_PAD_LO):
        vpad[CONV_PAD_LO + rows + k] = zero_row

    def copy(i, _):
        r0 = pl.multiple_of(i * rb, rb)
        vpad[pl.ds(r0 + CONV_PAD_LO, rb)] = v_ref[pl.ds(r0, rb)]
        return 0

    lax.fori_loop(0, nblk, copy, 0)

    cw = cw_ref[...]
    cb = cb_ref[...]
    nl = -lam_ref[...]
    softplus = jnp.maximum(nl, 0.0) + jnp.log1p(jnp.exp(-jnp.abs(nl)))
    c8 = -LRU_C * softplus
    b_r, b_i = br_ref[...], bi_ref[...]

    def block(i, carry):
        acc_a, acc_h = carry
        bi_ = (nblk - 1 - i) if rev else i
        r0 = pl.multiple_of(bi_ * rb, rb)
        x = cb
        for k in range(CONV_W):
            x = x + cw[k:k + 1] * vpad[pl.ds(r0 + k, rb)].reshape(rb * cpt, ct)
        xb = x.astype(BF16)
        pre_r = jnp.concatenate(
            [jnp.dot(xb[:, s * MXU_TILE:(s + 1) * MXU_TILE], wr_ref[s], preferred_element_type=F32)
             for s in range(nslab)], axis=1)
        pre_i = jnp.concatenate(
            [jnp.dot(xb[:, s * MXU_TILE:(s + 1) * MXU_TILE], wi_ref[s], preferred_element_type=F32)
             for s in range(nslab)], axis=1)
        r = jax.nn.sigmoid(pre_r + b_r)
        ig = jax.nn.sigmoid(pre_i + b_i)
        log_a = c8 * r
        a = jnp.exp(log_a)
        bx = jnp.sqrt(_neg_expm1(2.0 * log_a, a * a)) * (ig * x)
        q0 = pl.multiple_of(r0 * cpt, rb * cpt)
        a_scr[pl.ds(q0, rb * cpt), :] = a
        b_scr[pl.ds(q0, rb * cpt), :] = bx
        order = range(rb - 1, -1, -1) if rev else range(rb)
        for j in order:
            aj = a[j * cpt:(j + 1) * cpt]
            acc_h = aj * acc_h + bx[j * cpt:(j + 1) * cpt]
            acc_a = acc_a * aj
        return acc_a, acc_h

    acc_a, acc_h = lax.fori_loop(0, nblk, block, (jnp.ones((cpt, ct), F32), jnp.zeros((cpt, ct), F32)))

    if not chain:
        o_ref[...] = acc_h
        return

    sub = lax.broadcasted_iota(jnp.int32, (cpt, ct), 0)
    carry = carry_scr[...]
    h_in = jnp.zeros((cpt, ct), F32)
    for s in (range(cpt - 1, -1, -1) if rev else range(cpt)):
        h_in = jnp.where(sub == s, carry, h_in)
        carry = acc_a[s:s + 1] * carry + acc_h[s:s + 1]
    carry_scr[...] = carry

    def row(i, h):
        r = (rows - 1 - i) if rev else i
        q = pl.multiple_of(r * cpt, cpt)
        h = a_scr[pl.ds(q, cpt), :] * h + b_scr[pl.ds(q, cpt), :]
        o_ref[r] = h
        return h

    lax.fori_loop(0, rows, row, h_in, unroll=8)


def _block_diag(w, per):
    hh, n, _ = w.shape
    eye = jnp.eye(per, dtype=w.dtype)
    return jnp.einsum('gpij,pq->gpiqj', w.reshape(hh // per, per, n, n), eye).reshape(hh // per, per * n, per * n)


def _lru(v5, conv_w, conv_b, lam, w_r, b_r, w_i, b_i, h0, *, rev, chain):
    bsz, rows, ncg, cpt, w = v5.shape
    ct = 512
    nct = w // ct
    per = MXU_TILE // (w // LRU_HEADS)
    wr = _block_diag(w_r, per).astype(BF16)
    wi = _block_diag(w_i, per).astype(BF16)
    nslab = ct // MXU_TILE
    cgi = (lambda c: ncg - 1 - c) if rev else (lambda c: c)
    vec = lambda n: pl.BlockSpec((n, ct), lambda b, k, c: (0, k))
    in_specs = [pl.BlockSpec((None, rows, None, cpt, ct), lambda b, k, c: (b, 0, cgi(c), 0, k)),
                vec(CONV_W), vec(1),
                pl.BlockSpec((nslab, MXU_TILE, MXU_TILE), lambda b, k, c: (k, 0, 0)), vec(1),
                pl.BlockSpec((nslab, MXU_TILE, MXU_TILE), lambda b, k, c: (k, 0, 0)), vec(1),
                vec(1),
                pl.BlockSpec((None, 1, ct), lambda b, k, c: (b, 0, k))]
    if chain:
        out_spec = pl.BlockSpec((None, rows, None, cpt, ct), lambda b, k, c: (b, 0, cgi(c), 0, k))
        out_shape = jax.ShapeDtypeStruct(v5.shape, F32)
    else:
        out_spec = pl.BlockSpec((None, None, cpt, ct), lambda b, k, c: (b, c, 0, k))
        out_shape = jax.ShapeDtypeStruct((bsz, ncg, cpt, w), F32)
    kern = functools.partial(_lru_kernel, rows=rows, rev=rev, chain=chain)
    return pl.pallas_call(
        kern,
        grid=(bsz, nct, ncg),
        in_specs=in_specs,
        out_specs=out_spec,
        out_shape=out_shape,
        scratch_shapes=[pltpu.VMEM((rows + CONV_W - 1, cpt, ct), F32),
                        pltpu.VMEM((rows * cpt, ct), F32),
                        pltpu.VMEM((rows * cpt, ct), F32),
                        pltpu.VMEM((1, ct), F32)],
        compiler_params=_params(("arbitrary", "arbitrary", "arbitrary")),
        name="lru_rev" if rev else "lru_fwd",
    )(v5, conv_w, conv_b.reshape(1, w), wr, b_r.reshape(1, w), wi, b_i.reshape(1, w),
      lam.reshape(1, w), h0)


def _merge_kernel(x_ref, ys_ref, ylf_ref, ylb_ref, zg_ref, gate_ref, sh_ref, sc_ref, g2_ref,
                  wglu_ref, wa_ref, wb_ref, wo_ref, h1_ref, ft_ref, *, w_lru, d):
    za = jax.nn.gelu(ys_ref[...].astype(F32))
    ya = (za * jax.nn.sigmoid(jnp.dot(za.astype(BF16), wglu_ref[...],
                                      preferred_element_type=F32))).astype(BF16)
    zg = zg_ref[...]
    yb = ((ylf_ref[...] + ylb_ref[...]) * jax.nn.gelu(zg[:, :w_lru].astype(F32))).astype(BF16)
    ga = jax.nn.sigmoid(zg[:, w_lru:w_lru + d].astype(F32))
    gb = jax.nn.sigmoid(zg[:, w_lru + d:].astype(F32))
    m = (ga * jnp.dot(ya, wa_ref[...], preferred_element_type=F32)
         + gb * jnp.dot(yb, wb_ref[...], preferred_element_type=F32))
    o = jnp.dot(m.astype(BF16), wo_ref[...], preferred_element_type=F32)
    h1 = x_ref[...] + gate_ref[...] * o
    h1_ref[...] = h1
    f = _rms(h1, g2_ref[...]) * (1.0 + sc_ref[...]) + sh_ref[...]
    ft_ref[...] = f.T.astype(BF16)


def _merge(x, ys5, ylf, ylb, zg, mod6, g2, w_glu, w_a, w_b, w_o):
    bsz, length, d = x.shape
    w_s5 = ys5.shape[-1]
    w_lru = ylf.shape[-1]
    tm = 256
    nt = length // tm
    row = lambda k: (lambda b, i: (b * 6 + k, 0, 0))
    tok = lambda w: pl.BlockSpec((None, tm, w), lambda b, i: (b, i, 0))
    const = lambda shape: pl.BlockSpec(shape, lambda b, i: (0, 0), pipeline_mode=pl.Buffered(1))
    kern = functools.partial(_merge_kernel, w_lru=w_lru, d=d)
    return pl.pallas_call(
        kern,
        grid=(bsz, nt),
        in_specs=[tok(d), tok(w_s5), tok(w_lru), tok(w_lru), tok(zg.shape[-1]),
                  pl.BlockSpec((None, 1, d), row(2)), pl.BlockSpec((None, 1, d), row(3)),
                  pl.BlockSpec((None, 1, d), row(4)),
                  pl.BlockSpec((1, d), lambda b, i: (0, 0)),
                  const(w_glu.shape), const(w_a.shape), const(w_b.shape), const(w_o.shape)],
        out_specs=[tok(d), pl.BlockSpec((d, tm), lambda b, i: (0, b * nt + i))],
        out_shape=[jax.ShapeDtypeStruct((bsz, length, d), F32),
                   jax.ShapeDtypeStruct((d, bsz * length), BF16)],
        compiler_params=_params(("arbitrary", "arbitrary")),
        name="merge",
    )(x, ys5, ylf, ylb, zg, mod6, mod6, mod6, g2.reshape(1, d), w_glu, w_a, w_b, w_o)


def _top_rows(x, k):
    out = []
    for _ in range(k):
        m = jnp.max(x, axis=0, keepdims=True)
        out.append(m)
        x = jnp.where(x == m, -jnp.inf, x)
    return out


def _route_kernel(ft_ref, wq_ref, sk_ref, s2_ref, e2_ref, t1_ref, e1_ref):
    nk = sk_ref.shape[1]
    half = sk_ref.shape[2]
    qt = jnp.dot(wq_ref[...], ft_ref[...], preferred_element_type=F32).astype(BF16)
    for h in range(PEER_HEADS):
        s = [jnp.dot(sk_ref[side], qt[(2 * h + side) * half:(2 * h + side + 1) * half],
                     preferred_element_type=F32) for side in range(2)]
        top = [_top_rows(s[side], PEER_TOPK) for side in range(2)]
        sv2 = jnp.concatenate(top[1], axis=0)
        cand = jnp.concatenate([top[0][j] + sv2 for j in range(PEER_TOPK)], axis=0)
        best = _top_rows(cand, PEER_TOPK + 1)
        z = jnp.ones_like(best[0])
        for k in range(1, PEER_TOPK):
            z = z + jnp.exp(best[k] - best[0])
        tau = 0.5 * (best[PEER_TOPK - 1] + best[PEER_TOPK])
        s2_ref[h] = s[1]
        e2_ref[h] = jnp.exp(s[1] - top[1][0])
        t1_ref[h] = tau - s[0]
        e1_ref[h] = jnp.exp(s[0] - top[0][0]) / z


def _route(ft, wq_t, sk):
    d, nt = ft.shape
    nk = sk.shape[1]
    tm = 256
    out = jax.ShapeDtypeStruct((PEER_HEADS, nk, nt), F32)
    ospec = pl.BlockSpec((PEER_HEADS, nk, tm), lambda i: (0, 0, i))
    return pl.pallas_call(
        _route_kernel,
        grid=(nt // tm,),
        in_specs=[pl.BlockSpec((d, tm), lambda i: (0, i)),
                  pl.BlockSpec(wq_t.shape, lambda i: (0, 0)),
                  pl.BlockSpec(sk.shape, lambda i: (0, 0, 0))],
        out_specs=[ospec] * 4,
        out_shape=[out] * 4,
        compiler_params=_params(("arbitrary",)),
        name="route",
    )(ft, wq_t, sk)


def _peer_kernel(ft_ref, u_ref, v_ref, s2_ref, e2_ref, t1_ref, e1_ref, h1_ref, gate_ref, gf_ref,
                 o_ref, acc_ref, *, nk):
    j = pl.program_id(1)

    @pl.when(j == 0)
    def _():
        acc_ref[...] = jnp.zeros_like(acc_ref)

    st = jnp.dot(u_ref[...], ft_ref[...], preferred_element_type=F32)
    act = jax.nn.gelu(st)
    n1 = u_ref.shape[0] // nk
    ws = []
    for a in range(n1):
        i1 = j * n1 + a
        w = jnp.zeros((nk, st.shape[1]), F32)
        for h in range(PEER_HEADS):
            thr = t1_ref[h, pl.ds(i1, 1), :]
            cf = e1_ref[h, pl.ds(i1, 1), :]
            w = w + jnp.where(s2_ref[h] >= thr, e2_ref[h] * cf, 0.0)
        ws.append(w)
    w = jnp.concatenate(ws, axis=0) if n1 > 1 else ws[0]
    p = (w * act).astype(BF16)
    acc_ref[...] += lax.dot_general(p, v_ref[...], (((0,), (0,)), ((), ())), preferred_element_type=F32)

    @pl.when(j == pl.num_programs(1) - 1)
    def _():
        h2 = h1_ref[...] + gate_ref[...] * acc_ref[...]
        o_ref[...] = _rms(h2, gf_ref[...])


def _peer(ft, u, v, s2, e2, t1, e1, h1, mod6, g_final, tiles_per_batch_fn):
    d, nt = ft.shape
    ne = u.shape[0]
    nk = s2.shape[1]
    tm = 512
    et = 256
    rspec = pl.BlockSpec((PEER_HEADS, nk, tm), lambda i, j: (0, 0, i))
    kern = functools.partial(_peer_kernel, nk=nk)
    return pl.pallas_call(
        kern,
        grid=(nt // tm, ne // et),
        in_specs=[pl.BlockSpec((d, tm), lambda i, j: (0, i)),
                  pl.BlockSpec((et, d), lambda i, j: (j, 0)),
                  pl.BlockSpec((et, d), lambda i, j: (j, 0)),
                  rspec, rspec, rspec, rspec,
                  pl.BlockSpec((tm, d), lambda i, j: (i, 0)),
                  pl.BlockSpec((None, 1, d), lambda i, j: (tiles_per_batch_fn(i, tm) * 6 + 5, 0, 0)),
                  pl.BlockSpec((1, d), lambda i, j: (0, 0))],
        out_specs=pl.BlockSpec((tm, d), lambda i, j: (i, 0)),
        out_shape=jax.ShapeDtypeStruct((nt, d), F32),
        scratch_shapes=[pltpu.VMEM((tm, d), F32)],
        compiler_params=_params(("arbitrary", "arbitrary")),
        name="peer",
    )(ft, u, v, s2, e2, t1, e1, h1, mod6, g_final.reshape(1, d))


def kernel(x, c, ctx, c_ctx, w_ada, b_ada, g_norm1, w_in, s5_a_re, s5_a_im, s5_log_dt, s5_b_re, s5_b_im, s5_c_re, s5_c_im, s5_d, s5_w_glu, lru_conv_w, lru_conv_b, lru_lambda, lru_w_r, lru_b_r, lru_w_i, lru_b_i, w_proj_a, w_proj_b, w_out, g_norm2, peer_w_query, peer_sub_keys, peer_u, peer_v, g_final):
    bsz, length, d = x.shape
    assert w_ada.shape[0] == 1, "single-layer kernel"
    rows = length // GRID_W
    w_s5 = s5_d.shape[-1]
    w_lru = lru_lambda.shape[-1]
    assert bsz < 8 and length % (GRID_W * LRU_ROW_BLOCK) == 0 and ctx.shape[1] % LRU_ROW_BLOCK == 0

    cin = jnp.zeros((8, d), F32).at[:bsz].set(c).at[bsz].set(c_ctx)
    mod6 = _ada(cin, w_ada[0], b_ada[0]).reshape(8 * 6, 1, d)

    w_in_b = w_in[0].astype(BF16)
    zu, zv, zg = _inproj(x, g_norm1[0], mod6, lambda b: b, w_in_b, w_s5, w_lru)
    zu_c, zv_c = _inproj(ctx, g_norm1[0], mod6, lambda b: bsz, w_in_b[:, :w_s5 + w_lru], w_s5, w_lru)

    mt, bp, cp, a16 = _s5_operators(s5_a_re[0], s5_a_im[0], s5_log_dt[0], s5_b_re[0], s5_b_im[0],
                                    s5_c_re[0], s5_c_im[0], s5_d[0])
    ys5 = _from_chunks(_s5(_to_chunks(zu), _to_chunks(zu_c), mt, bp, cp, a16))

    lctx = ctx.shape[1]
    vc5 = jnp.zeros((1, lctx, 1, COLS_PER_TILE, w_lru), F32).at[0, :, 0, :bsz].set(zv_c.transpose(1, 0, 2))
    v5 = zv.reshape(bsz, rows, GRID_W // COLS_PER_TILE, COLS_PER_TILE, w_lru)
    zero_h = jnp.zeros((1, 1, w_lru), F32)
    yl = []
    for dr in range(2):
        args = (lru_conv_w[0], lru_conv_b[0], lru_lambda[0, dr], lru_w_r[0, dr], lru_b_r[0, dr],
                lru_w_i[0, dr], lru_b_i[0, dr])
        hc = _lru(vc5, *args, zero_h, rev=dr == 1, chain=False)
        h0 = hc[0, 0, :bsz].reshape(bsz, 1, w_lru)
        yl.append(_lru(v5, *args, h0, rev=dr == 1, chain=True).reshape(bsz, length, w_lru))

    h1, ft = _merge(x, ys5, yl[0], yl[1], zg, mod6, g_norm2[0], s5_w_glu[0].astype(BF16),
                    w_proj_a[0].astype(BF16), w_proj_b[0].astype(BF16), w_out[0].astype(BF16))

    s2, e2, t1, e1 = _route(ft, peer_w_query[0].T.astype(BF16), peer_sub_keys[0].astype(BF16))
    out = _peer(ft, peer_u[0].astype(BF16), peer_v[0].astype(BF16), s2, e2, t1, e1,
                h1.reshape(bsz * length, d), mod6, g_final,
                lambda i, tm: i // (length // tm))
    return out.reshape(bsz, length, d)
```

```python
import functools
import math

import jax
import jax.numpy as jnp
from jax import lax
from jax.experimental import pallas as pl
from jax.experimental.pallas import tpu as pltpu

F32 = jnp.float32
BF16 = jnp.bfloat16
EPS = 1e-6
GRID_W = 64
S5_GROUP = 16
S5_STATE = 64
S5_CHUNK = 16
LRU_HEADS = 16
LRU_C = 8.0
CONV_W = 4
CONV_PAD_LO = (CONV_W - 1) // 2
N_KEYS = 128
PEER_HEADS = 8
PEER_TOPK = 16
COLS_PER_TILE = 8
MXU_TILE = 256
VMEM_LIMIT = 56 * 1024 * 1024


def _params(sem, vmem=VMEM_LIMIT):
    return pltpu.CompilerParams(dimension_semantics=sem, vmem_limit_bytes=vmem)


def _rms(x, g):
    return x * lax.rsqrt(jnp.mean(x * x, axis=-1, keepdims=True) + EPS) * g


def _ada_kernel(c_ref, w_ref, b_ref, o_ref):
    c = c_ref[...]
    sc = c * jax.nn.sigmoid(c)
    o_ref[...] = jnp.dot(sc.astype(BF16), w_ref[...].astype(BF16),
                         preferred_element_type=F32) + b_ref[...]


def _ada(cin, w, b):
    d, n = w.shape
    tn = 1536
    return pl.pallas_call(
        _ada_kernel,
        grid=(n // tn,),
        in_specs=[pl.BlockSpec((8, d), lambda j: (0, 0)),
                  pl.BlockSpec((d, tn), lambda j: (0, j)),
                  pl.BlockSpec((1, tn), lambda j: (0, j))],
        out_specs=pl.BlockSpec((8, tn), lambda j: (0, j)),
        out_shape=jax.ShapeDtypeStruct((8, n), F32),
        compiler_params=_params(("arbitrary",)),
        name="ada",
    )(cin, w, b.reshape(1, n))


def _inproj_kernel(x_ref, g_ref, sh_ref, sc_ref, w_ref, *refs, n_u, n_v):
    if len(refs) == 4:
        zu_ref, zv_ref, zg_ref, n_scr = refs
    else:
        zu_ref, zv_ref, n_scr = refs
        zg_ref = None
    j = pl.program_id(2)

    @pl.when(j == 0)
    def _():
        y = _rms(x_ref[...], g_ref[...])
        n_scr[...] = (y * (1.0 + sc_ref[...]) + sh_ref[...]).astype(BF16)

    z = jnp.dot(n_scr[...], w_ref[...], preferred_element_type=F32)

    @pl.when(j < n_u)
    def _():
        zu_ref[...] = z.astype(zu_ref.dtype)

    @pl.when((j >= n_u) & (j < n_u + n_v))
    def _():
        zv_ref[...] = z

    if zg_ref is not None:
        @pl.when(j >= n_u + n_v)
        def _():
            zg_ref[...] = z.astype(zg_ref.dtype)


def _inproj(x, g, mod6, mod_row, w_bf16, width_u, width_v):
    bsz, length, d = x.shape
    n = w_bf16.shape[1]
    tm = min(512, length)
    tn = 1024
    n_u, n_v = width_u // tn, width_v // tn
    n_g = n // tn - n_u - n_v
    kern = functools.partial(_inproj_kernel, n_u=n_u, n_v=n_v)
    row = lambda k: (lambda b, i, j: (mod_row(b) * 6 + k, 0, 0))
    out_specs = [pl.BlockSpec((None, tm, tn), lambda b, i, j: (b, i, jnp.minimum(j, n_u - 1))),
                 pl.BlockSpec((None, tm, tn), lambda b, i, j: (b, i, jnp.clip(j - n_u, 0, n_v - 1)))]
    out_shape = [jax.ShapeDtypeStruct((bsz, length, width_u), F32),
                 jax.ShapeDtypeStruct((bsz, length, width_v), F32)]
    if n_g:
        out_specs.append(pl.BlockSpec((None, tm, tn), lambda b, i, j: (b, i, jnp.maximum(j - n_u - n_v, 0))))
        out_shape.append(jax.ShapeDtypeStruct((bsz, length, n_g * tn), BF16))
    return pl.pallas_call(
        kern,
        grid=(bsz, length // tm, n // tn),
        in_specs=[pl.BlockSpec((None, tm, d), lambda b, i, j: (b, i, 0)),
                  pl.BlockSpec((1, d), lambda b, i, j: (0, 0)),
                  pl.BlockSpec((None, 1, d), row(0)),
                  pl.BlockSpec((None, 1, d), row(1)),
                  pl.BlockSpec((d, tn), lambda b, i, j: (0, j))],
        out_specs=out_specs,
        out_shape=out_shape,
        scratch_shapes=[pltpu.VMEM((tm, d), BF16)],
        compiler_params=_params(("arbitrary", "arbitrary", "arbitrary")),
        name="inproj",
    )(x, g.reshape(1, d), mod6, mod6, w_bf16)


def _s5_prep_kernel(are_ref, aim_ref, ldt_ref, bre_ref, bim_ref, cre_ref, cim_ref,
                    k_ref, cg_ref, bg_ref, a16_ref):
    nk = S5_CHUNK + 1
    dt = jnp.exp(ldt_ref[...])
    lre, lim = are_ref[...], aim_ref[...]
    xr, xi = lre * dt, lim * dt
    kf = jnp.right_shift(lax.broadcasted_iota(jnp.int32, (nk * S5_GROUP, 1), 0),
                         int(math.log2(S5_GROUP))).astype(F32)
    mag = jnp.exp(kf * xr)
    pr, pim = mag * jnp.cos(kf * xi), mag * jnp.sin(kf * xi)
    e1 = jnp.exp(xr)
    ar, ai = e1 * jnp.cos(xi), e1 * jnp.sin(xi)
    den = lre * lre + lim * lim
    qr = ((ar - 1.0) * lre + ai * lim) / den
    qi = (ai * lre - (ar - 1.0) * lim) / den
    bre, bim = bre_ref[...], bim_ref[...]
    bbr, bbi = qr * bre - qi * bim, qr * bim + qi * bre
    tile = lambda m: jnp.concatenate([m] * nk, axis=0)
    cr_t, ci_t, br_t, bi_t = tile(cre_ref[...]), tile(cim_ref[...]), tile(bbr), tile(bbi)
    cg = jnp.concatenate([cr_t * pr - ci_t * pim, -(cr_t * pim + ci_t * pr)], axis=1)
    bg = jnp.concatenate([br_t * pr - bi_t * pim, br_t * pim + bi_t * pr], axis=1)
    bcat = jnp.concatenate([bbr, bbi], axis=1)
    cg_ref[...] = cg
    bg_ref[...] = bg
    k_ref[...] = lax.dot_general(cg[:S5_CHUNK * S5_GROUP], bcat, (((1,), (1,)), ((), ())),
                                 precision=lax.Precision.HIGHEST, preferred_element_type=F32)
    n = S5_CHUNK * S5_GROUP
    a16_ref[...] = jnp.concatenate([pr[n:n + 1], pim[n:n + 1]], axis=1)


def _s5_operators(a_re, a_im, log_dt, b_re, b_im, c_re, c_im, d_skip):
    _, groups, p = a_re.shape
    h = S5_GROUP
    dg = 2 * groups
    nk = S5_CHUNK + 1
    vec = lambda t: t.reshape(dg, 1, p)
    tr = lambda t: jnp.swapaxes(t, -1, -2).reshape(dg, h, p)
    ldt = jnp.broadcast_to(log_dt[..., None], (2, groups, p))
    spec_v = pl.BlockSpec((None, 1, p), lambda i: (i, 0, 0))
    spec_m = pl.BlockSpec((None, h, p), lambda i: (i, 0, 0))
    kfl, cg, bg, a16 = pl.pallas_call(
        _s5_prep_kernel,
        grid=(dg,),
        in_specs=[spec_v, spec_v, spec_v, spec_m, spec_m, spec_m, spec_m],
        out_specs=[pl.BlockSpec((None, S5_CHUNK * h, h), lambda i: (i, 0, 0)),
                   pl.BlockSpec((None, nk * h, 2 * p), lambda i: (i, 0, 0)),
                   pl.BlockSpec((None, nk * h, 2 * p), lambda i: (i, 0, 0)),
                   pl.BlockSpec((None, 1, 2 * p), lambda i: (i, 0, 0))],
        out_shape=[jax.ShapeDtypeStruct((dg, S5_CHUNK * h, h), F32),
                   jax.ShapeDtypeStruct((dg, nk * h, 2 * p), F32),
                   jax.ShapeDtypeStruct((dg, nk * h, 2 * p), F32),
                   jax.ShapeDtypeStruct((dg, 1, 2 * p), F32)],
        compiler_params=_params(("arbitrary",)),
        name="s5_prep",
    )(vec(a_re), vec(a_im), vec(ldt), tr(b_re), tr(b_im),
      c_re.reshape(dg, h, p), c_im.reshape(dg, h, p))

    t = S5_CHUNK
    kfl = kfl.reshape(2, groups, t, h, h)
    cg = cg.reshape(2, groups, nk, h, 2 * p)
    bg = bg.reshape(2, groups, nk, h, 2 * p)
    ti = jnp.arange(t)
    lag = ti[:, None] - ti[None, :]
    kf = jnp.where((lag >= 0)[None, :, :, None, None], kfl[0][:, jnp.clip(lag, 0, t - 1)], 0.0)
    kb = jnp.where((lag <= 0)[None, :, :, None, None], kfl[1][:, jnp.clip(-lag, 0, t - 1)], 0.0)
    eye = (lag == 0)[None, :, :, None, None] * jnp.eye(h, dtype=F32)[None, None, None]
    m = kf + kb + eye * d_skip.reshape(groups, 1, 1, 1, h)
    mt = m.transpose(0, 2, 4, 1, 3).reshape(groups, t * h, t * h)
    bp_f = bg[0][:, t - 1 - ti].reshape(groups, t * h, 2 * p)
    bp_b = bg[1][:, ti].reshape(groups, t * h, 2 * p)
    bp = jnp.concatenate([bp_f, bp_b], axis=-1)
    cp_f = cg[0][:, ti + 1].reshape(groups, t * h, 2 * p)
    cp_b = cg[1][:, t - ti].reshape(groups, t * h, 2 * p)
    cp = jnp.concatenate([cp_f, cp_b], axis=-1).swapaxes(1, 2)
    return mt.astype(BF16), bp.astype(BF16), cp.astype(BF16), a16.reshape(2, groups, 1, 2 * p)


def _cplx_coef(a):
    p = a.shape[1] // 2
    lane = lax.broadcasted_iota(jnp.int32, a.shape, 1)
    sw = pltpu.roll(a, p, axis=1)
    return jnp.where(lane < p, a, sw), jnp.where(lane < p, -sw, a)


def _cplx_mul(a, x):
    c1, c2 = _cplx_coef(a)
    return c1 * x + c2 * pltpu.roll(x, x.shape[1] // 2, axis=1)


def _chunk_scan(x, a, reverse):
    n = x.shape[0]
    row = lax.broadcasted_iota(jnp.int32, (n, 1), 0)
    o = 1
    while o < n:
        if reverse:
            sh = jnp.where(row < n - o, pltpu.roll(x, n - o, axis=0), 0.0)
        else:
            sh = jnp.where(row >= o, pltpu.roll(x, o, axis=0), 0.0)
        x = x + _cplx_mul(a, sh)
        a = _cplx_mul(a, a)
        o *= 2
    return x


S5_GPB = 128 // S5_GROUP
S5_TPT = 128 // S5_GROUP


def _chunk_perm():
    n = S5_TPT * S5_GPB * S5_GROUP
    i = jnp.arange(n)
    t, g, h = i // (S5_GPB * S5_GROUP), (i // S5_GROUP) % S5_GPB, i % S5_GROUP
    dst = g * (S5_TPT * S5_GROUP) + t * S5_GROUP + h
    return jnp.zeros((n, n), BF16).at[i, dst].set(1.0)


def _load_chunks(src_ref, perm, dst_scr):
    nc = src_ref.shape[0] // S5_CHUNK
    halves = []
    for r in range(S5_CHUNK // S5_TPT):
        xcat = jnp.concatenate(
            [src_ref[pl.ds(r * S5_TPT + t, nc, stride=S5_CHUNK), :].astype(BF16) for t in range(S5_TPT)], axis=1)
        halves.append(jnp.dot(xcat, perm, preferred_element_type=F32).astype(BF16))
    for g in range(S5_GPB):
        dst_scr[g] = jnp.concatenate([hv[:, g * 128:(g + 1) * 128] for hv in halves], axis=1)


def _s5_kernel(u_ref, uc_ref, perm_ref, permt_ref, mt_ref, bp_ref, cp_ref, af_ref, ab_ref, y_ref,
               u_scr, uc_scr, y_scr):
    nc = u_ref.shape[0] // S5_CHUNK
    _load_chunks(u_ref, perm_ref[...], u_scr)
    _load_chunks(uc_ref, perm_ref[...], uc_scr)

    def group(g, _):
        y_scr[g] = _s5_group(u_scr[g], uc_scr[g], mt_ref[g], bp_ref[g], cp_ref[g], af_ref[g], ab_ref[g])
        return 0

    lax.fori_loop(0, S5_GPB, group, 0)

    for r in range(S5_CHUNK // S5_TPT):
        ycat = jnp.concatenate([y_scr[g][:, r * 128:(r + 1) * 128] for g in range(S5_GPB)], axis=1)
        back = jnp.dot(ycat, permt_ref[...], preferred_element_type=F32)
        for t in range(S5_TPT):
            y_ref[pl.ds(r * S5_TPT + t, nc, stride=S5_CHUNK), :] = back[:, t * 128:(t + 1) * 128]


def _s5_group(u, uc, mt, bp, cp, af, ab):
    nc = u.shape[0]
    ps = af.shape[1]
    v = jnp.dot(u, bp, preferred_element_type=F32)
    vc = jnp.dot(uc, bp, preferred_element_type=F32)
    ncc = vc.shape[0]
    hcf = _chunk_scan(vc[:, :ps], af, False)[ncc - 1:ncc]
    hcb = _chunk_scan(vc[:, ps:], ab, True)[0:1]
    row = lax.broadcasted_iota(jnp.int32, (nc, 1), 0)
    vf = v[:, :ps] + jnp.where(row == 0, _cplx_mul(af, hcf), 0.0)
    vb = v[:, ps:] + jnp.where(row == nc - 1, _cplx_mul(ab, hcb), 0.0)
    sf = _chunk_scan(vf, af, False)
    sb = _chunk_scan(vb, ab, True)
    sin_f = jnp.where(row == 0, hcf, pltpu.roll(sf, 1, axis=0))
    sin_b = jnp.where(row == nc - 1, hcb, pltpu.roll(sb, nc - 1, axis=0))
    s_in = jnp.concatenate([sin_f, sin_b], axis=1).astype(BF16)
    y = (jnp.dot(u, mt, preferred_element_type=F32)
         + jnp.dot(s_in, cp, preferred_element_type=F32))
    return y.astype(BF16)


def _s5(zu, zu_c, mt, bp, cp, a16):
    bsz, length, w = zu.shape
    lctx = zu_c.shape[1]
    nc, ncc = length // S5_CHUNK, lctx // S5_CHUNK
    cw = S5_CHUNK * S5_GROUP
    ps = a16.shape[-1]
    perm = _chunk_perm()
    once = lambda shape, imap: pl.BlockSpec(shape, imap, pipeline_mode=pl.Buffered(1))
    wspec = pl.BlockSpec((S5_GPB, cw, cw), lambda b, q: (q, 0, 0))
    return pl.pallas_call(
        _s5_kernel,
        grid=(bsz, w // 128),
        in_specs=[once((None, length, 128), lambda b, q: (b, 0, q)),
                  pl.BlockSpec((None, lctx, 128), lambda b, q: (b, 0, q)),
                  once(perm.shape, lambda b, q: (0, 0)),
                  once(perm.shape, lambda b, q: (0, 0)),
                  wspec, wspec, wspec,
                  pl.BlockSpec((None, S5_GPB, 1, ps), lambda b, q: (0, q, 0, 0)),
                  pl.BlockSpec((None, S5_GPB, 1, ps), lambda b, q: (1, q, 0, 0))],
        out_specs=once((None, length, 128), lambda b, q: (b, 0, q)),
        out_shape=jax.ShapeDtypeStruct((bsz, length, w), F32),
        scratch_shapes=[pltpu.VMEM((S5_GPB, nc, cw), BF16),
                        pltpu.VMEM((S5_GPB, ncc, cw), BF16),
                        pltpu.VMEM((S5_GPB, nc, cw), BF16)],
        compiler_params=_params(("arbitrary", "arbitrary")),
        name="s5",
    )(zu, zu_c, perm, perm.T, mt, bp, cp, a16, a16)


LRU_ROW_BLOCK = 16


def _neg_expm1(y, exp_y):
    series = -y * (1.0 + y * (0.5 + y * (1.0 / 6.0 + y * (1.0 / 24.0))))
    return jnp.where(y > -0.01, series, 1.0 - exp_y)


def _lru_kernel(v_ref, cw_ref, cb_ref, wr_ref, br_ref, wi_ref, bi_ref, lam_ref, h0_ref, o_ref,
                vpad, a_scr, b_scr, carry_scr, *, rows, rev, chain):
    ct = v_ref.shape[-1]
    cpt = COLS_PER_TILE
    rb = LRU_ROW_BLOCK
    nblk = rows // rb
    nslab = ct // MXU_TILE

    if chain:
        @pl.when(pl.program_id(2) == 0)
        def _():
            carry_scr[...] = h0_ref[...]

    zero_row = jnp.zeros((cpt, ct), F32)
    for k in range(CONV_PAD_LO):
        vpad[k] = zero_row
    for k in range(CONV_W - 1 - CONV_PAD_LO):
        vpad[CONV_PAD_LO + rows + k] = zero_row

    def copy(i, _):
        r0 = pl.multiple_of(i * rb, rb)
        vpad[pl.ds(r0 + CONV_PAD_LO, rb)] = v_ref[pl.ds(r0, rb)]
        return 0

    lax.fori_loop(0, nblk, copy, 0)

    cw = cw_ref[...]
    cb = cb_ref[...]
    nl = -lam_ref[...]
    softplus = jnp.maximum(nl, 0.0) + jnp.log1p(jnp.exp(-jnp.abs(nl)))
    c8 = -LRU_C * softplus
    b_r, b_i = br_ref[...], bi_ref[...]

    def block(i, carry):
        acc_a, acc_h = carry
        bi_ = (nblk - 1 - i) if rev else i
        r0 = pl.multiple_of(bi_ * rb, rb)
        x = cb
        for k in range(CONV_W):
            x = x + cw[k:k + 1] * vpad[pl.ds(r0 + k, rb)].reshape(rb * cpt, ct)
        xb = x.astype(BF16)
        pre_r = jnp.concatenate(
            [jnp.dot(xb[:, s * MXU_TILE:(s + 1) * MXU_TILE], wr_ref[s], preferred_element_type=F32)
             for s in range(nslab)], axis=1)
        pre_i = jnp.concatenate(
            [jnp.dot(xb[:, s * MXU_TILE:(s + 1) * MXU_TILE], wi_ref[s], preferred_element_type=F32)
             for s in range(nslab)], axis=1)
        r = jax.nn.sigmoid(pre_r + b_r)
        ig = jax.nn.sigmoid(pre_i + b_i)
        log_a = c8 * r
        a = jnp.exp(log_a)
        bx = jnp.sqrt(_neg_expm1(2.0 * log_a, a * a)) * (ig * x)
        q0 = pl.multiple_of(r0 * cpt, rb * cpt)
        a_scr[pl.ds(q0, rb * cpt), :] = a
        b_scr[pl.ds(q0, rb * cpt), :] = bx
        order = range(rb - 1, -1, -1) if rev else range(rb)
        for j in order:
            aj = a[j * cpt:(j + 1) * cpt]
            acc_h = aj * acc_h + bx[j * cpt:(j + 1) * cpt]
            acc_a = acc_a * aj
        return acc_a, acc_h

    acc_a, acc_h = lax.fori_loop(0, nblk, block, (jnp.ones((cpt, ct), F32), jnp.zeros((cpt, ct), F32)))

    if not chain:
        o_ref[...] = acc_h
        return

    sub = lax.broadcasted_iota(jnp.int32, (cpt, ct), 0)
    carry = carry_scr[...]
    h_in = jnp.zeros((cpt, ct), F32)
    for s in (range(cpt - 1, -1, -1) if rev else range(cpt)):
        h_in = jnp.where(sub == s, carry, h_in)
        carry = acc_a[s:s + 1] * carry + acc_h[s:s + 1]
    carry_scr[...] = carry

    def row(i, h):
        r = (rows - 1 - i) if rev else i
        q = pl.multiple_of(r * cpt, cpt)
        h = a_scr[pl.ds(q, cpt), :] * h + b_scr[pl.ds(q, cpt), :]
        o_ref[r] = h
        return h

    lax.fori_loop(0, rows, row, h_in, unroll=8)


def _block_diag(w, per):
    hh, n, _ = w.shape
    eye = jnp.eye(per, dtype=w.dtype)
    return jnp.einsum('gpij,pq->gpiqj', w.reshape(hh // per, per, n, n), eye).reshape(hh // per, per * n, per * n)


def _lru(v5, conv_w, conv_b, lam, w_r, b_r, w_i, b_i, h0, *, rev, chain):
    bsz, rows, ncg, cpt, w = v5.shape
    ct = 512
    nct = w // ct
    per = MXU_TILE // (w // LRU_HEADS)
    wr = _block_diag(w_r, per).astype(BF16)
    wi = _block_diag(w_i, per).astype(BF16)
    nslab = ct // MXU_TILE
    cgi = (lambda c: ncg - 1 - c) if rev else (lambda c: c)
    vec = lambda n: pl.BlockSpec((n, ct), lambda b, k, c: (0, k))
    in_specs = [pl.BlockSpec((None, rows, None, cpt, ct), lambda b, k, c: (b, 0, cgi(c), 0, k)),
                vec(CONV_W), vec(1),
                pl.BlockSpec((nslab, MXU_TILE, MXU_TILE), lambda b, k, c: (k, 0, 0)), vec(1),
                pl.BlockSpec((nslab, MXU_TILE, MXU_TILE), lambda b, k, c: (k, 0, 0)), vec(1),
                vec(1),
                pl.BlockSpec((None, 1, ct), lambda b, k, c: (b, 0, k))]
    if chain:
        out_spec = pl.BlockSpec((None, rows, None, cpt, ct), lambda b, k, c: (b, 0, cgi(c), 0, k))
        out_shape = jax.ShapeDtypeStruct(v5.shape, F32)
    else:
        out_spec = pl.BlockSpec((None, None, cpt, ct), lambda b, k, c: (b, c, 0, k))
        out_shape = jax.ShapeDtypeStruct((bsz, ncg, cpt, w), F32)
    kern = functools.partial(_lru_kernel, rows=rows, rev=rev, chain=chain)
    return pl.pallas_call(
        kern,
        grid=(bsz, nct, ncg),
        in_specs=in_specs,
        out_specs=out_spec,
        out_shape=out_shape,
        scratch_shapes=[pltpu.VMEM((rows + CONV_W - 1, cpt, ct), F32),
                        pltpu.VMEM((rows * cpt, ct), F32),
                        pltpu.VMEM((rows * cpt, ct), F32),
                        pltpu.VMEM((1, ct), F32)],
        compiler_params=_params(("arbitrary", "arbitrary", "arbitrary")),
        name="lru_rev" if rev else "lru_fwd",
    )(v5, conv_w, conv_b.reshape(1, w), wr, b_r.reshape(1, w), wi, b_i.reshape(1, w),
      lam.reshape(1, w), h0)


def _merge_kernel(x_ref, ys_ref, ylf_ref, ylb_ref, zg_ref, gate_ref, sh_ref, sc_ref, g2_ref,
                  wglu_ref, wa_ref, wb_ref, wo_ref, h1_ref, ft_ref, *, w_lru, d):
    za = jax.nn.gelu(ys_ref[...])
    ya = (za * jax.nn.sigmoid(jnp.dot(za.astype(BF16), wglu_ref[...],
                                      preferred_element_type=F32))).astype(BF16)
    zg = zg_ref[...]
    yb = ((ylf_ref[...] + ylb_ref[...]) * jax.nn.gelu(zg[:, :w_lru].astype(F32))).astype(BF16)
    ga = jax.nn.sigmoid(zg[:, w_lru:w_lru + d].astype(F32))
    gb = jax.nn.sigmoid(zg[:, w_lru + d:].astype(F32))
    m = (ga * jnp.dot(ya, wa_ref[...], preferred_element_type=F32)
         + gb * jnp.dot(yb, wb_ref[...], preferred_element_type=F32))
    o = jnp.dot(m.astype(BF16), wo_ref[...], preferred_element_type=F32)
    h1 = x_ref[...] + gate_ref[...] * o
    h1_ref[...] = h1
    f = _rms(h1, g2_ref[...]) * (1.0 + sc_ref[...]) + sh_ref[...]
    ft_ref[...] = f.T.astype(BF16)


def _merge(x, ys5, ylf, ylb, zg, mod6, g2, w_glu, w_a, w_b, w_o):
    bsz, length, d = x.shape
    w_s5 = ys5.shape[-1]
    w_lru = ylf.shape[-1]
    tm = 256
    nt = length // tm
    row = lambda k: (lambda b, i: (b * 6 + k, 0, 0))
    tok = lambda w: pl.BlockSpec((None, tm, w), lambda b, i: (b, i, 0))
    const = lambda shape: pl.BlockSpec(shape, lambda b, i: (0, 0), pipeline_mode=pl.Buffered(1))
    kern = functools.partial(_merge_kernel, w_lru=w_lru, d=d)
    return pl.pallas_call(
        kern,
        grid=(bsz, nt),
        in_specs=[tok(d), tok(w_s5), tok(w_lru), tok(w_lru), tok(zg.shape[-1]),
                  pl.BlockSpec((None, 1, d), row(2)), pl.BlockSpec((None, 1, d), row(3)),
                  pl.BlockSpec((None, 1, d), row(4)),
                  pl.BlockSpec((1, d), lambda b, i: (0, 0)),
                  const(w_glu.shape), const(w_a.shape), const(w_b.shape), const(w_o.shape)],
        out_specs=[tok(d), pl.BlockSpec((d, tm), lambda b, i: (0, b * nt + i))],
        out_shape=[jax.ShapeDtypeStruct((bsz, length, d), F32),
                   jax.ShapeDtypeStruct((d, bsz * length), BF16)],
        compiler_params=_params(("arbitrary", "arbitrary")),
        name="merge",
    )(x, ys5, ylf, ylb, zg, mod6, mod6, mod6, g2.reshape(1, d), w_glu, w_a, w_b, w_o)


def _top_rows(x, k, n_rank=0):
    out = []
    rank = jnp.full(x.shape, float(n_rank), F32) if n_rank else None
    for i in range(k):
        m = jnp.max(x, axis=0, keepdims=True)
        out.append(m)
        hit = x == m
        if i < n_rank:
            rank = jnp.where(hit, float(i), rank)
        x = jnp.where(hit, -jnp.inf, x)
    return out, rank


_CAND_PAIRS = [(j, k) for j in range(PEER_TOPK + 1) for k in range(PEER_TOPK + 1)
               if (j + 1) * (k + 1) <= PEER_TOPK + 1]
_CAND_ROWS = -(-len(_CAND_PAIRS) // 8) * 8


def _route_kernel(ft_ref, wq_ref, sk_ref, r2_ref, e2_ref, n1_ref, e1_ref, cand_scr):
    half = sk_ref.shape[2]
    tm = ft_ref.shape[1]
    qt = jnp.dot(wq_ref[...], ft_ref[...], preferred_element_type=F32).astype(BF16)
    for r in range(len(_CAND_PAIRS), _CAND_ROWS):
        cand_scr[r:r + 1, :] = jnp.full((1, tm), -jnp.inf, F32)
    for h in range(PEER_HEADS):
        s = [jnp.dot(sk_ref[side], qt[(2 * h + side) * half:(2 * h + side + 1) * half],
                     preferred_element_type=F32) for side in range(2)]
        top0, _ = _top_rows(s[0], PEER_TOPK + 1)
        top1, rank2 = _top_rows(s[1], PEER_TOPK + 1, PEER_TOPK)
        for r, (j, k) in enumerate(_CAND_PAIRS):
            cand_scr[r:r + 1, :] = top0[j] + top1[k]
        best, _ = _top_rows(cand_scr[...], PEER_TOPK + 1)
        z = jnp.ones_like(best[0])
        for k in range(1, PEER_TOPK):
            z = z + jnp.exp(best[k] - best[0])
        t1 = 0.5 * (best[PEER_TOPK - 1] + best[PEER_TOPK]) - s[0]
        n1 = jnp.zeros_like(t1)
        for k in range(PEER_TOPK):
            n1 = n1 + jnp.where(top1[k] >= t1, 1.0, 0.0)
        r2_ref[h] = rank2.astype(BF16)
        e2_ref[h] = jnp.exp(s[1] - top1[0]).astype(BF16)
        n1_ref[h] = n1
        e1_ref[h] = jnp.exp(s[0] - top0[0]) / z


def _route(ft, wq_t, sk):
    d, nt = ft.shape
    nk = sk.shape[1]
    tm = 256
    ospec = pl.BlockSpec((PEER_HEADS, nk, tm), lambda i: (0, 0, i))
    return pl.pallas_call(
        _route_kernel,
        grid=(nt // tm,),
        in_specs=[pl.BlockSpec((d, tm), lambda i: (0, i)),
                  pl.BlockSpec(wq_t.shape, lambda i: (0, 0), pipeline_mode=pl.Buffered(1)),
                  pl.BlockSpec(sk.shape, lambda i: (0, 0, 0))],
        out_specs=[ospec] * 4,
        out_shape=[jax.ShapeDtypeStruct((PEER_HEADS, nk, nt), dt) for dt in (BF16, BF16, F32, F32)],
        scratch_shapes=[pltpu.VMEM((_CAND_ROWS, tm), F32)],
        compiler_params=_params(("arbitrary",)),
        name="route",
    )(ft, wq_t, sk)


def _peer_kernel(ft_ref, u_ref, v_ref, r2_ref, e2_ref, n1_ref, e1_ref, h1_ref, gate_ref, gf_ref,
                 o_ref, acc_ref, *, nk):
    j = pl.program_id(1)

    @pl.when(j == 0)
    def _():
        acc_ref[...] = jnp.zeros_like(acc_ref)

    tm = ft_ref.shape[1]
    n_i1 = u_ref.shape[0] // nk
    per = MXU_TILE // nk
    acc = acc_ref[...]
    n_chain = n_i1 // per
    rows_of = lambda c: slice(c * MXU_TILE, (c + 1) * MXU_TILE)
    pre = lambda c: jnp.dot(u_ref[rows_of(c), :], ft_ref[...], preferred_element_type=F32)
    st_next = pre(0)
    for c in range(n_chain):
        rows = rows_of(c)
        st = st_next
        if c + 1 < n_chain:
            st_next = pre(c + 1)
        ps = []
        for a in range(per):
            i1 = j * n_i1 + c * per + a
            act = jax.nn.gelu(st[a * nk:(a + 1) * nk]).astype(BF16)
            w = jnp.zeros((nk, tm), BF16)
            for h in range(PEER_HEADS):
                cnt = n1_ref[h, pl.ds(i1, 1), :].astype(BF16)
                cf = e1_ref[h, pl.ds(i1, 1), :].astype(BF16)
                w = w + jnp.where(r2_ref[h] < cnt, e2_ref[h] * cf, jnp.zeros((), BF16))
            ps.append(w * act)
        p = jnp.concatenate(ps, axis=0)
        acc = acc + lax.dot_general(p, v_ref[rows, :], (((0,), (0,)), ((), ())), preferred_element_type=F32)
    acc_ref[...] = acc

    @pl.when(j == pl.num_programs(1) - 1)
    def _():
        h2 = h1_ref[...] + gate_ref[...] * acc_ref[...]
        o_ref[...] = _rms(h2, gf_ref[...])


def _peer(ft, u, v, r2, e2, n1, e1, h1, mod6, g_final, tiles_per_batch_fn):
    d, nt = ft.shape
    ne = u.shape[0]
    nk = r2.shape[1]
    tm = 512
    et = 1024
    rspec = pl.BlockSpec((PEER_HEADS, nk, tm), lambda i, j: (0, 0, i), pipeline_mode=pl.Buffered(1))
    kern = functools.partial(_peer_kernel, nk=nk)
    return pl.pallas_call(
        kern,
        grid=(nt // tm, ne // et),
        in_specs=[pl.BlockSpec((d, tm), lambda i, j: (0, i)),
                  pl.BlockSpec((et, d), lambda i, j: (j, 0)),
                  pl.BlockSpec((et, d), lambda i, j: (j, 0)),
                  rspec, rspec, rspec, rspec,
                  pl.BlockSpec((tm, d), lambda i, j: (i, 0), pipeline_mode=pl.Buffered(1)),
                  pl.BlockSpec((None, 1, d), lambda i, j: (tiles_per_batch_fn(i, tm) * 6 + 5, 0, 0)),
                  pl.BlockSpec((1, d), lambda i, j: (0, 0))],
        out_specs=pl.BlockSpec((tm, d), lambda i, j: (i, 0), pipeline_mode=pl.Buffered(1)),
        out_shape=jax.ShapeDtypeStruct((nt, d), F32),
        scratch_shapes=[pltpu.VMEM((tm, d), F32)],
        compiler_params=_params(("arbitrary", "arbitrary")),
        name="peer",
    )(ft, u, v, r2, e2, n1, e1, h1, mod6, g_final.reshape(1, d))


def kernel(x, c, ctx, c_ctx, w_ada, b_ada, g_norm1, w_in, s5_a_re, s5_a_im, s5_log_dt, s5_b_re, s5_b_im, s5_c_re, s5_c_im, s5_d, s5_w_glu, lru_conv_w, lru_conv_b, lru_lambda, lru_w_r, lru_b_r, lru_w_i, lru_b_i, w_proj_a, w_proj_b, w_out, g_norm2, peer_w_query, peer_sub_keys, peer_u, peer_v, g_final):
    bsz, length, d = x.shape
    assert w_ada.shape[0] == 1, "single-layer kernel"
    rows = length // GRID_W
    w_s5 = s5_d.shape[-1]
    w_lru = lru_lambda.shape[-1]
    assert bsz < 8 and length % (GRID_W * LRU_ROW_BLOCK) == 0 and ctx.shape[1] % LRU_ROW_BLOCK == 0

    cin = jnp.zeros((8, d), F32).at[:bsz].set(c).at[bsz].set(c_ctx)
    mod6 = _ada(cin, w_ada[0], b_ada[0]).reshape(8 * 6, 1, d)

    w_in_b = w_in[0].astype(BF16)
    zu, zv, zg = _inproj(x, g_norm1[0], mod6, lambda b: b, w_in_b, w_s5, w_lru)
    zu_c, zv_c = _inproj(ctx, g_norm1[0], mod6, lambda b: bsz, w_in_b[:, :w_s5 + w_lru], w_s5, w_lru)

    mt, bp, cp, a16 = _s5_operators(s5_a_re[0], s5_a_im[0], s5_log_dt[0], s5_b_re[0], s5_b_im[0],
                                    s5_c_re[0], s5_c_im[0], s5_d[0])
    ys5 = _s5(zu, zu_c, mt, bp, cp, a16)

    lctx = ctx.shape[1]
    vc5 = jnp.zeros((1, lctx, 1, COLS_PER_TILE, w_lru), F32).at[0, :, 0, :bsz].set(zv_c.transpose(1, 0, 2))
    v5 = zv.reshape(bsz, rows, GRID_W // COLS_PER_TILE, COLS_PER_TILE, w_lru)
    zero_h = jnp.zeros((1, 1, w_lru), F32)
    yl = []
    for dr in range(2):
        args = (lru_conv_w[0], lru_conv_b[0], lru_lambda[0, dr], lru_w_r[0, dr], lru_b_r[0, dr],
                lru_w_i[0, dr], lru_b_i[0, dr])
        hc = _lru(vc5, *args, zero_h, rev=dr == 1, chain=False)
        h0 = hc[0, 0, :bsz].reshape(bsz, 1, w_lru)
        yl.append(_lru(v5, *args, h0, rev=dr == 1, chain=True).reshape(bsz, length, w_lru))

    h1, ft = _merge(x, ys5, yl[0], yl[1], zg, mod6, g_norm2[0], s5_w_glu[0].astype(BF16),
                    w_proj_a[0].astype(BF16), w_proj_b[0].astype(BF16), w_out[0].astype(BF16))

    r2, e2, n1, e1 = _route(ft, peer_w_query[0].T.astype(BF16), peer_sub_keys[0].astype(BF16))
    out = _peer(ft, peer_u[0].astype(BF16), peer_v[0].astype(BF16), r2, e2, n1, e1,
                h1.reshape(bsz * length, d), mod6, g_final,
                lambda i, tm: i // (length // tm))
    return out.reshape(bsz, length, d)
```

```python
import functools

import jax
import jax.numpy as jnp
from jax import lax
from jax.experimental import pallas as pl
from jax.experimental.pallas import tpu as pltpu

F32 = jnp.float32
BF16 = jnp.bfloat16
EPS = 1e-6
GRID_W = 64
S5_GROUP = 16
S5_STATE = 64
S5_CHUNK = 16
LRU_HEADS = 16
LRU_C = 8.0
CONV_W = 4
CONV_PAD_LO = (CONV_W - 1) // 2
N_KEYS = 128
PEER_HEADS = 8
PEER_TOPK = 16
COLS_PER_TILE = 8
MXU_TILE = 256
VMEM_LIMIT = 56 * 1024 * 1024


def _params(sem, vmem=VMEM_LIMIT):
    return pltpu.CompilerParams(dimension_semantics=sem, vmem_limit_bytes=vmem)


def _rms(x, g):
    return x * lax.rsqrt(jnp.mean(x * x, axis=-1, keepdims=True) + EPS) * g


def _ada_kernel(c_ref, w_ref, b_ref, o_ref):
    c = c_ref[...]
    sc = c * jax.nn.sigmoid(c)
    o_ref[...] = jnp.dot(sc.astype(BF16), w_ref[...].astype(BF16),
                         preferred_element_type=F32) + b_ref[...]


def _ada(cin, w, b):
    d, n = w.shape
    tn = 1536
    return pl.pallas_call(
        _ada_kernel,
        grid=(n // tn,),
        in_specs=[pl.BlockSpec((8, d), lambda j: (0, 0)),
                  pl.BlockSpec((d, tn), lambda j: (0, j)),
                  pl.BlockSpec((1, tn), lambda j: (0, j))],
        out_specs=pl.BlockSpec((8, tn), lambda j: (0, j)),
        out_shape=jax.ShapeDtypeStruct((8, n), F32),
        compiler_params=_params(("arbitrary",)),
        name="ada",
    )(cin, w, b.reshape(1, n))


def _inproj_kernel(x_ref, g_ref, sh_ref, sc_ref, w_ref, *refs, n_u, n_v):
    if len(refs) == 4:
        zu_ref, zv_ref, zg_ref, n_scr = refs
    else:
        zu_ref, zv_ref, n_scr = refs
        zg_ref = None
    j = pl.program_id(2)

    @pl.when(j == 0)
    def _():
        y = _rms(x_ref[...], g_ref[...])
        n_scr[...] = (y * (1.0 + sc_ref[...]) + sh_ref[...]).astype(BF16)

    z = jnp.dot(n_scr[...], w_ref[...], preferred_element_type=F32)

    @pl.when(j < n_u)
    def _():
        zu_ref[...] = z.astype(zu_ref.dtype)

    @pl.when((j >= n_u) & (j < n_u + n_v))
    def _():
        zv_ref[...] = z

    if zg_ref is not None:
        @pl.when(j >= n_u + n_v)
        def _():
            zg_ref[...] = z.astype(zg_ref.dtype)


def _inproj(x, g, mod6, mod_row, w_bf16, width_u, width_v):
    bsz, length, d = x.shape
    n = w_bf16.shape[1]
    tm = min(512, length)
    tn = 1024
    n_u, n_v = width_u // tn, width_v // tn
    n_g = n // tn - n_u - n_v
    kern = functools.partial(_inproj_kernel, n_u=n_u, n_v=n_v)
    row = lambda k: (lambda b, i, j: (mod_row(b) * 6 + k, 0, 0))
    out_specs = [pl.BlockSpec((None, tm, tn), lambda b, i, j: (b, i, jnp.minimum(j, n_u - 1))),
                 pl.BlockSpec((None, tm, tn), lambda b, i, j: (b, i, jnp.clip(j - n_u, 0, n_v - 1)))]
    out_shape = [jax.ShapeDtypeStruct((bsz, length, width_u), F32),
                 jax.ShapeDtypeStruct((bsz, length, width_v), F32)]
    if n_g:
        out_specs.append(pl.BlockSpec((None, tm, tn), lambda b, i, j: (b, i, jnp.maximum(j - n_u - n_v, 0))))
        out_shape.append(jax.ShapeDtypeStruct((bsz, length, n_g * tn), BF16))
    return pl.pallas_call(
        kern,
        grid=(bsz, length // tm, n // tn),
        in_specs=[pl.BlockSpec((None, tm, d), lambda b, i, j: (b, i, 0)),
                  pl.BlockSpec((1, d), lambda b, i, j: (0, 0)),
                  pl.BlockSpec((None, 1, d), row(0)),
                  pl.BlockSpec((None, 1, d), row(1)),
                  pl.BlockSpec((d, tn), lambda b, i, j: (0, j))],
        out_specs=out_specs,
        out_shape=out_shape,
        scratch_shapes=[pltpu.VMEM((tm, d), BF16)],
        compiler_params=_params(("arbitrary", "arbitrary", "arbitrary")),
        name="inproj",
    )(x, g.reshape(1, d), mod6, mod6, w_bf16)


def _s5_prep_kernel(are_ref, aim_ref, ldt_ref, bre_ref, bim_ref, cre_ref, cim_ref,
                    k_ref, cg_ref, bg_ref, a16_ref):
    nk = S5_CHUNK + 1
    dt = jnp.exp(ldt_ref[...])
    lre, lim = are_ref[...], aim_ref[...]
    xr, xi = lre * dt, lim * dt
    e1 = jnp.exp(xr)
    ar, ai = e1 * jnp.cos(xi), e1 * jnp.sin(xi)
    pows = [(jnp.ones_like(ar), jnp.zeros_like(ai))]
    for _ in range(nk - 1):
        kr, ki = pows[-1]
        pows.append((kr * ar - ki * ai, kr * ai + ki * ar))
    rep = lambda rows: jnp.concatenate([jnp.broadcast_to(r, (S5_GROUP, r.shape[1])) for r in rows], axis=0)
    pr, pim = rep([q[0] for q in pows]), rep([q[1] for q in pows])
    den = lre * lre + lim * lim
    qr = ((ar - 1.0) * lre + ai * lim) / den
    qi = (ai * lre - (ar - 1.0) * lim) / den
    bre, bim = bre_ref[...], bim_ref[...]
    bbr, bbi = qr * bre - qi * bim, qr * bim + qi * bre
    tile = lambda m: jnp.concatenate([m] * nk, axis=0)
    cr_t, ci_t, br_t, bi_t = tile(cre_ref[...]), tile(cim_ref[...]), tile(bbr), tile(bbi)
    cg = jnp.concatenate([cr_t * pr - ci_t * pim, -(cr_t * pim + ci_t * pr)], axis=1)
    bg = jnp.concatenate([br_t * pr - bi_t * pim, br_t * pim + bi_t * pr], axis=1)
    bcat = jnp.concatenate([bbr, bbi], axis=1)
    cg_ref[...] = cg
    bg_ref[...] = bg
    k_ref[...] = lax.dot_general(cg[:S5_CHUNK * S5_GROUP], bcat, (((1,), (1,)), ((), ())),
                                 precision=lax.Precision.HIGHEST, preferred_element_type=F32)
    n = S5_CHUNK * S5_GROUP
    a16_ref[...] = jnp.concatenate([pr[n:n + 1], pim[n:n + 1]], axis=1)


def _s5_operators(a_re, a_im, log_dt, b_re, b_im, c_re, c_im, d_skip):
    _, groups, p = a_re.shape
    h = S5_GROUP
    dg = 2 * groups
    nk = S5_CHUNK + 1
    vec = lambda t: t.reshape(dg, 1, p)
    tr = lambda t: jnp.swapaxes(t, -1, -2).reshape(dg, h, p)
    ldt = jnp.broadcast_to(log_dt[..., None], (2, groups, p))
    spec_v = pl.BlockSpec((None, 1, p), lambda i: (i, 0, 0))
    spec_m = pl.BlockSpec((None, h, p), lambda i: (i, 0, 0))
    kfl, cg, bg, a16 = pl.pallas_call(
        _s5_prep_kernel,
        grid=(dg,),
        in_specs=[spec_v, spec_v, spec_v, spec_m, spec_m, spec_m, spec_m],
        out_specs=[pl.BlockSpec((None, S5_CHUNK * h, h), lambda i: (i, 0, 0)),
                   pl.BlockSpec((None, nk * h, 2 * p), lambda i: (i, 0, 0)),
                   pl.BlockSpec((None, nk * h, 2 * p), lambda i: (i, 0, 0)),
                   pl.BlockSpec((None, 1, 2 * p), lambda i: (i, 0, 0))],
        out_shape=[jax.ShapeDtypeStruct((dg, S5_CHUNK * h, h), F32),
                   jax.ShapeDtypeStruct((dg, nk * h, 2 * p), F32),
                   jax.ShapeDtypeStruct((dg, nk * h, 2 * p), F32),
                   jax.ShapeDtypeStruct((dg, 1, 2 * p), F32)],
        compiler_params=_params(("arbitrary",)),
        name="s5_prep",
    )(vec(a_re), vec(a_im), vec(ldt), tr(b_re), tr(b_im),
      c_re.reshape(dg, h, p), c_im.reshape(dg, h, p))

    t = S5_CHUNK
    kfl = kfl.reshape(2, groups, t, h, h)
    cg = cg.reshape(2, groups, nk, h, 2 * p)
    bg = bg.reshape(2, groups, nk, h, 2 * p)
    ti = jnp.arange(t)
    lag = ti[:, None] - ti[None, :]
    kf = jnp.where((lag >= 0)[None, :, :, None, None], kfl[0][:, jnp.clip(lag, 0, t - 1)], 0.0)
    kb = jnp.where((lag <= 0)[None, :, :, None, None], kfl[1][:, jnp.clip(-lag, 0, t - 1)], 0.0)
    eye = (lag == 0)[None, :, :, None, None] * jnp.eye(h, dtype=F32)[None, None, None]
    m = kf + kb + eye * d_skip.reshape(groups, 1, 1, 1, h)
    mt = m.transpose(0, 2, 4, 1, 3).reshape(groups, t * h, t * h)
    bp_f = bg[0][:, t - 1 - ti].reshape(groups, t * h, 2 * p)
    bp_b = bg[1][:, ti].reshape(groups, t * h, 2 * p)
    bp = jnp.concatenate([bp_f, bp_b], axis=-1)
    cp_f = cg[0][:, ti + 1].reshape(groups, t * h, 2 * p)
    cp_b = cg[1][:, t - ti].reshape(groups, t * h, 2 * p)
    cp = jnp.concatenate([cp_f, cp_b], axis=-1).swapaxes(1, 2)
    return mt.astype(BF16), bp.astype(BF16), cp.astype(BF16), a16.reshape(2, groups, 1, 2 * p)


def _cplx_coef(a):
    p = a.shape[1] // 2
    lane = lax.broadcasted_iota(jnp.int32, a.shape, 1)
    sw = pltpu.roll(a, p, axis=1)
    return jnp.where(lane < p, a, sw), jnp.where(lane < p, -sw, a)


def _cplx_mul(a, x):
    c1, c2 = _cplx_coef(a)
    return c1 * x + c2 * pltpu.roll(x, x.shape[1] // 2, axis=1)


def _chunk_scan(x, a, reverse):
    n = x.shape[0]
    row = lax.broadcasted_iota(jnp.int32, (n, 1), 0)
    o = 1
    while o < n:
        if reverse:
            sh = jnp.where(row < n - o, pltpu.roll(x, n - o, axis=0), 0.0)
        else:
            sh = jnp.where(row >= o, pltpu.roll(x, o, axis=0), 0.0)
        x = x + _cplx_mul(a, sh)
        a = _cplx_mul(a, a)
        o *= 2
    return x


S5_GPB = 128 // S5_GROUP
S5_TPT = 128 // S5_GROUP


def _chunk_perm():
    n = S5_TPT * S5_GPB * S5_GROUP
    i = jnp.arange(n)
    t, g, h = i // (S5_GPB * S5_GROUP), (i // S5_GROUP) % S5_GPB, i % S5_GROUP
    dst = g * (S5_TPT * S5_GROUP) + t * S5_GROUP + h
    return jnp.zeros((n, n), BF16).at[i, dst].set(1.0)


def _load_chunks(src_ref, perm, dst_scr):
    nc = src_ref.shape[0] // S5_CHUNK
    halves = []
    for r in range(S5_CHUNK // S5_TPT):
        xcat = jnp.concatenate(
            [src_ref[pl.ds(r * S5_TPT + t, nc, stride=S5_CHUNK), :].astype(BF16) for t in range(S5_TPT)], axis=1)
        halves.append(jnp.dot(xcat, perm, preferred_element_type=F32).astype(BF16))
    for g in range(S5_GPB):
        dst_scr[g] = jnp.concatenate([hv[:, g * 128:(g + 1) * 128] for hv in halves], axis=1)


def _s5_kernel(u_ref, uc_ref, perm_ref, permt_ref, mt_ref, bp_ref, cp_ref, af_ref, ab_ref, y_ref,
               u_scr, uc_scr, y_scr):
    nc = u_ref.shape[0] // S5_CHUNK
    _load_chunks(u_ref, perm_ref[...], u_scr)
    _load_chunks(uc_ref, perm_ref[...], uc_scr)

    def group(g, _):
        y_scr[g] = _s5_group(u_scr[g], uc_scr[g], mt_ref[g], bp_ref[g], cp_ref[g], af_ref[g], ab_ref[g])
        return 0

    lax.fori_loop(0, S5_GPB, group, 0)

    for r in range(S5_CHUNK // S5_TPT):
        ycat = jnp.concatenate([y_scr[g][:, r * 128:(r + 1) * 128] for g in range(S5_GPB)], axis=1)
        back = jnp.dot(ycat, permt_ref[...], preferred_element_type=F32)
        for t in range(S5_TPT):
            y_ref[pl.ds(r * S5_TPT + t, nc, stride=S5_CHUNK), :] = back[:, t * 128:(t + 1) * 128]


def _s5_group(u, uc, mt, bp, cp, af, ab):
    nc = u.shape[0]
    ps = af.shape[1]
    v = jnp.dot(u, bp, preferred_element_type=F32)
    vc = jnp.dot(uc, bp, preferred_element_type=F32)
    ncc = vc.shape[0]
    hcf = _chunk_scan(vc[:, :ps], af, False)[ncc - 1:ncc]
    hcb = _chunk_scan(vc[:, ps:], ab, True)[0:1]
    row = lax.broadcasted_iota(jnp.int32, (nc, 1), 0)
    vf = v[:, :ps] + jnp.where(row == 0, _cplx_mul(af, hcf), 0.0)
    vb = v[:, ps:] + jnp.where(row == nc - 1, _cplx_mul(ab, hcb), 0.0)
    sf = _chunk_scan(vf, af, False)
    sb = _chunk_scan(vb, ab, True)
    sin_f = jnp.where(row == 0, hcf, pltpu.roll(sf, 1, axis=0))
    sin_b = jnp.where(row == nc - 1, hcb, pltpu.roll(sb, nc - 1, axis=0))
    s_in = jnp.concatenate([sin_f, sin_b], axis=1).astype(BF16)
    y = (jnp.dot(u, mt, preferred_element_type=F32)
         + jnp.dot(s_in, cp, preferred_element_type=F32))
    return y.astype(BF16)


def _s5(zu, zu_c, mt, bp, cp, a16):
    bsz, length, w = zu.shape
    lctx = zu_c.shape[1]
    nc, ncc = length // S5_CHUNK, lctx // S5_CHUNK
    cw = S5_CHUNK * S5_GROUP
    ps = a16.shape[-1]
    perm = _chunk_perm()
    once = lambda shape, imap: pl.BlockSpec(shape, imap, pipeline_mode=pl.Buffered(1))
    wspec = pl.BlockSpec((S5_GPB, cw, cw), lambda b, q: (q, 0, 0))
    return pl.pallas_call(
        _s5_kernel,
        grid=(bsz, w // 128),
        in_specs=[once((None, length, 128), lambda b, q: (b, 0, q)),
                  pl.BlockSpec((None, lctx, 128), lambda b, q: (b, 0, q)),
                  once(perm.shape, lambda b, q: (0, 0)),
                  once(perm.shape, lambda b, q: (0, 0)),
                  wspec, wspec, wspec,
                  pl.BlockSpec((None, S5_GPB, 1, ps), lambda b, q: (0, q, 0, 0)),
                  pl.BlockSpec((None, S5_GPB, 1, ps), lambda b, q: (1, q, 0, 0))],
        out_specs=once((None, length, 128), lambda b, q: (b, 0, q)),
        out_shape=jax.ShapeDtypeStruct((bsz, length, w), F32),
        scratch_shapes=[pltpu.VMEM((S5_GPB, nc, cw), BF16),
                        pltpu.VMEM((S5_GPB, ncc, cw), BF16),
                        pltpu.VMEM((S5_GPB, nc, cw), BF16)],
        compiler_params=_params(("arbitrary", "arbitrary")),
        name="s5",
    )(zu, zu_c, perm, perm.T, mt, bp, cp, a16, a16)


LRU_ROW_BLOCK = 16


def _neg_expm1(y, exp_y):
    series = -y * (1.0 + y * (0.5 + y * (1.0 / 6.0 + y * (1.0 / 24.0))))
    return jnp.where(y > -0.01, series, 1.0 - exp_y)


def _lru_kernel(v_ref, cw_ref, cb_ref, wr_ref, br_ref, wi_ref, bi_ref, lam_ref, h0_ref, o_ref,
                vpad, a_scr, b_scr, carry_scr, *, rows, rev, chain):
    ct = v_ref.shape[-1]
    cpt = COLS_PER_TILE
    rb = LRU_ROW_BLOCK
    nblk = rows // rb
    nslab = ct // MXU_TILE

    if chain:
        @pl.when(pl.program_id(2) == 0)
        def _():
            carry_scr[...] = h0_ref[...]

    zero_row = jnp.zeros((cpt, ct), F32)
    for k in range(CONV_PAD_LO):
        vpad[k] = zero_row
    for k in range(CONV_W - 1 - CONV_PAD_LO):
        vpad[CONV_PAD_LO + rows + k] = zero_row

    def copy(i, _):
        r0 = pl.multiple_of(i * rb, rb)
        vpad[pl.ds(r0 + CONV_PAD_LO, rb)] = v_ref[pl.ds(r0, rb)]
        return 0

    lax.fori_loop(0, nblk, copy, 0)

    cw = cw_ref[...]
    cb = cb_ref[...]
    nl = -lam_ref[...]
    softplus = jnp.maximum(nl, 0.0) + jnp.log1p(jnp.exp(-jnp.abs(nl)))
    c8 = -LRU_C * softplus
    b_r, b_i = br_ref[...], bi_ref[...]

    def block(i, carry):
        acc_a, acc_h = carry
        bi_ = (nblk - 1 - i) if rev else i
        r0 = pl.multiple_of(bi_ * rb, rb)
        x = cb
        for k in range(CONV_W):
            x = x + cw[k:k + 1] * vpad[pl.ds(r0 + k, rb)].reshape(rb * cpt, ct)
        xb = x.astype(BF16)
        pre_r = jnp.concatenate(
            [jnp.dot(xb[:, s * MXU_TILE:(s + 1) * MXU_TILE], wr_ref[s], preferred_element_type=F32)
             for s in range(nslab)], axis=1)
        pre_i = jnp.concatenate(
            [jnp.dot(xb[:, s * MXU_TILE:(s + 1) * MXU_TILE], wi_ref[s], preferred_element_type=F32)
             for s in range(nslab)], axis=1)
        r = jax.nn.sigmoid(pre_r + b_r)
        ig = jax.nn.sigmoid(pre_i + b_i)
        log_a = c8 * r
        a = jnp.exp(log_a)
        bx = jnp.sqrt(_neg_expm1(2.0 * log_a, a * a)) * (ig * x)
        q0 = pl.multiple_of(r0 * cpt, rb * cpt)
        a_scr[pl.ds(q0, rb * cpt), :] = a
        b_scr[pl.ds(q0, rb * cpt), :] = bx
        order = range(rb - 1, -1, -1) if rev else range(rb)
        for j in order:
            aj = a[j * cpt:(j + 1) * cpt]
            acc_h = aj * acc_h + bx[j * cpt:(j + 1) * cpt]
            acc_a = acc_a * aj
        return acc_a, acc_h

    acc_a, acc_h = lax.fori_loop(0, nblk, block, (jnp.ones((cpt, ct), F32), jnp.zeros((cpt, ct), F32)))

    if not chain:
        o_ref[...] = acc_h
        return

    sub = lax.broadcasted_iota(jnp.int32, (cpt, ct), 0)
    carry = carry_scr[...]
    h_in = jnp.zeros((cpt, ct), F32)
    for s in (range(cpt - 1, -1, -1) if rev else range(cpt)):
        h_in = jnp.where(sub == s, carry, h_in)
        carry = acc_a[s:s + 1] * carry + acc_h[s:s + 1]
    carry_scr[...] = carry

    def row(i, h):
        r = (rows - 1 - i) if rev else i
        q = pl.multiple_of(r * cpt, cpt)
        h = a_scr[pl.ds(q, cpt), :] * h + b_scr[pl.ds(q, cpt), :]
        o_ref[r] = h
        return h

    lax.fori_loop(0, rows, row, h_in, unroll=8)


def _block_diag(w, per):
    hh, n, _ = w.shape
    eye = jnp.eye(per, dtype=w.dtype)
    return jnp.einsum('gpij,pq->gpiqj', w.reshape(hh // per, per, n, n), eye).reshape(hh // per, per * n, per * n)


def _lru(v5, conv_w, conv_b, lam, w_r, b_r, w_i, b_i, h0, *, rev, chain):
    bsz, rows, ncg, cpt, w = v5.shape
    ct = 512
    nct = w // ct
    per = MXU_TILE // (w // LRU_HEADS)
    wr = _block_diag(w_r, per).astype(BF16)
    wi = _block_diag(w_i, per).astype(BF16)
    nslab = ct // MXU_TILE
    cgi = (lambda c: ncg - 1 - c) if rev else (lambda c: c)
    vec = lambda n: pl.BlockSpec((n, ct), lambda b, k, c: (0, k))
    in_specs = [pl.BlockSpec((None, rows, None, cpt, ct), lambda b, k, c: (b, 0, cgi(c), 0, k)),
                vec(CONV_W), vec(1),
                pl.BlockSpec((nslab, MXU_TILE, MXU_TILE), lambda b, k, c: (k, 0, 0)), vec(1),
                pl.BlockSpec((nslab, MXU_TILE, MXU_TILE), lambda b, k, c: (k, 0, 0)), vec(1),
                vec(1),
                pl.BlockSpec((None, 1, ct), lambda b, k, c: (b, 0, k))]
    if chain:
        out_spec = pl.BlockSpec((None, rows, None, cpt, ct), lambda b, k, c: (b, 0, cgi(c), 0, k))
        out_shape = jax.ShapeDtypeStruct(v5.shape, F32)
    else:
        out_spec = pl.BlockSpec((None, None, cpt, ct), lambda b, k, c: (b, c, 0, k))
        out_shape = jax.ShapeDtypeStruct((bsz, ncg, cpt, w), F32)
    kern = functools.partial(_lru_kernel, rows=rows, rev=rev, chain=chain)
    return pl.pallas_call(
        kern,
        grid=(bsz, nct, ncg),
        in_specs=in_specs,
        out_specs=out_spec,
        out_shape=out_shape,
        scratch_shapes=[pltpu.VMEM((rows + CONV_W - 1, cpt, ct), F32),
                        pltpu.VMEM((rows * cpt, ct), F32),
                        pltpu.VMEM((rows * cpt, ct), F32),
                        pltpu.VMEM((1, ct), F32)],
        compiler_params=_params(("arbitrary", "arbitrary", "arbitrary")),
        name="lru_rev" if rev else "lru_fwd",
    )(v5, conv_w, conv_b.reshape(1, w), wr, b_r.reshape(1, w), wi, b_i.reshape(1, w),
      lam.reshape(1, w), h0)


def _merge_kernel(x_ref, ys_ref, ylf_ref, ylb_ref, zg_ref, gate_ref, sh_ref, sc_ref, g2_ref,
                  wglu_ref, wa_ref, wb_ref, wo_ref, h1_ref, ft_ref, *, w_lru, d):
    za = jax.nn.gelu(ys_ref[...])
    ya = (za * jax.nn.sigmoid(jnp.dot(za.astype(BF16), wglu_ref[...],
                                      preferred_element_type=F32))).astype(BF16)
    zg = zg_ref[...]
    yb = ((ylf_ref[...] + ylb_ref[...]) * jax.nn.gelu(zg[:, :w_lru].astype(F32))).astype(BF16)
    ga = jax.nn.sigmoid(zg[:, w_lru:w_lru + d].astype(F32))
    gb = jax.nn.sigmoid(zg[:, w_lru + d:].astype(F32))
    m = (ga * jnp.dot(ya, wa_ref[...], preferred_element_type=F32)
         + gb * jnp.dot(yb, wb_ref[...], preferred_element_type=F32))
    o = jnp.dot(m.astype(BF16), wo_ref[...], preferred_element_type=F32)
    h1 = x_ref[...] + gate_ref[...] * o
    h1_ref[...] = h1
    f = _rms(h1, g2_ref[...]) * (1.0 + sc_ref[...]) + sh_ref[...]
    ft_ref[...] = f.T.astype(BF16)


def _merge(x, ys5, ylf, ylb, zg, mod6, g2, w_glu, w_a, w_b, w_o):
    bsz, length, d = x.shape
    w_s5 = ys5.shape[-1]
    w_lru = ylf.shape[-1]
    tm = 256
    nt = length // tm
    row = lambda k: (lambda b, i: (b * 6 + k, 0, 0))
    tok = lambda w: pl.BlockSpec((None, tm, w), lambda b, i: (b, i, 0))
    const = lambda shape: pl.BlockSpec(shape, lambda b, i: (0, 0), pipeline_mode=pl.Buffered(1))
    kern = functools.partial(_merge_kernel, w_lru=w_lru, d=d)
    return pl.pallas_call(
        kern,
        grid=(bsz, nt),
        in_specs=[tok(d), tok(w_s5), tok(w_lru), tok(w_lru), tok(zg.shape[-1]),
                  pl.BlockSpec((None, 1, d), row(2)), pl.BlockSpec((None, 1, d), row(3)),
                  pl.BlockSpec((None, 1, d), row(4)),
                  pl.BlockSpec((1, d), lambda b, i: (0, 0)),
                  const(w_glu.shape), const(w_a.shape), const(w_b.shape), const(w_o.shape)],
        out_specs=[tok(d), pl.BlockSpec((d, tm), lambda b, i: (0, b * nt + i))],
        out_shape=[jax.ShapeDtypeStruct((bsz, length, d), F32),
                   jax.ShapeDtypeStruct((d, bsz * length), BF16)],
        compiler_params=_params(("arbitrary", "arbitrary")),
        name="merge",
    )(x, ys5, ylf, ylb, zg, mod6, mod6, mod6, g2.reshape(1, d), w_glu, w_a, w_b, w_o)


def _top_rows(x, k):
    out = []
    for _ in range(k):
        m = jnp.max(x, axis=0, keepdims=True)
        out.append(m)
        x = jnp.where(x == m, -jnp.inf, x)
    return out


def _sort_pairs(n):
    out = []
    p = 1
    while p < n:
        k = p
        while k >= 1:
            for j in range(k % p, n - k, 2 * k):
                for i in range(min(k, n - j - k)):
                    if (i + j) // (2 * p) == (i + j + k) // (2 * p):
                        out.append((i + j, i + j + k))
            k //= 2
        p *= 2
    return out


def _top_rows_sorted(x, k):
    n_slab = x.shape[0] // 8
    slabs = [x[8 * r:8 * r + 8] for r in range(n_slab)]
    for a, b in _sort_pairs(n_slab):
        slabs[a], slabs[b] = jnp.maximum(slabs[a], slabs[b]), jnp.minimum(slabs[a], slabs[b])
    slabs.append(jnp.full_like(slabs[0], -jnp.inf))
    out = []
    for it in range(k):
        m = jnp.max(slabs[0], axis=0, keepdims=True)
        out.append(m)
        hit = slabs[0] == m
        for r in range(min(k - 1 - it, n_slab)):
            slabs[r] = jnp.where(hit, slabs[r + 1], slabs[r])
    return out


def _count_above(rows, x, strict):
    assert PEER_TOPK == 16
    above = (lambda a: a > x) if strict else (lambda a: a >= x)
    sel = jnp.where
    b3 = above(rows[7])
    b2 = above(sel(b3, rows[11], rows[3]))
    b1 = above(sel(b3, sel(b2, rows[13], rows[9]), sel(b2, rows[5], rows[1])))
    b0 = above(sel(b3, sel(b2, sel(b1, rows[14], rows[12]), sel(b1, rows[10], rows[8])),
                   sel(b2, sel(b1, rows[6], rows[4]), sel(b1, rows[2], rows[0]))))
    return (sel(b3, 8.0, 0.0) + sel(b2, 4.0, 0.0) + sel(b1, 2.0, 0.0) + sel(b0, 1.0, 0.0)
            + sel(above(rows[15]), 1.0, 0.0))


_CAND_PAIRS = [(j, k) for j in range(PEER_TOPK + 1) for k in range(PEER_TOPK + 1)
               if (j + 1) * (k + 1) <= PEER_TOPK + 1]
_CAND_ROWS = -(-len(_CAND_PAIRS) // 8) * 8


def _route_kernel(ft_ref, wq_ref, sk_ref, r2_ref, e2_ref, n1_ref, e1_ref, cand_scr):
    half = sk_ref.shape[2]
    tm = ft_ref.shape[1]
    qt = jnp.dot(wq_ref[...], ft_ref[...], preferred_element_type=F32).astype(BF16)
    for r in range(len(_CAND_PAIRS), _CAND_ROWS):
        cand_scr[r:r + 1, :] = jnp.full((1, 128), -jnp.inf, F32)
    for h in range(PEER_HEADS):
        s_full = [jnp.dot(sk_ref[side], qt[(2 * h + side) * half:(2 * h + side + 1) * half],
                          preferred_element_type=F32) for side in range(2)]
        for l in range(tm // 128):
            lanes = slice(l * 128, (l + 1) * 128)
            s = [sf[:, lanes] for sf in s_full]
            top0 = _top_rows_sorted(s[0], PEER_TOPK + 1)
            top1 = _top_rows_sorted(s[1], PEER_TOPK + 1)
            for r, (j, k) in enumerate(_CAND_PAIRS):
                cand_scr[r:r + 1, :] = top0[j] + top1[k]
            best = _top_rows(cand_scr[...], PEER_TOPK + 1)
            z = jnp.ones_like(best[0])
            for k in range(1, PEER_TOPK):
                z = z + jnp.exp(best[k] - best[0])
            t1 = 0.5 * (best[PEER_TOPK - 1] + best[PEER_TOPK]) - s[0]
            n1 = _count_above(top1, t1, strict=False)
            rank2 = _count_above(top1, s[1], strict=True)
            r2_ref[h, :, lanes] = rank2.astype(BF16)
            e2_ref[h, :, lanes] = jnp.exp(s[1] - top1[0]).astype(BF16)
            n1_ref[h, :, lanes] = n1
            e1_ref[h, :, lanes] = jnp.exp(s[0] - top0[0]) / z


def _route(ft, wq_t, sk):
    d, nt = ft.shape
    nk = sk.shape[1]
    tm = 256
    ospec = pl.BlockSpec((PEER_HEADS, nk, tm), lambda i: (0, 0, i))
    return pl.pallas_call(
        _route_kernel,
        grid=(nt // tm,),
        in_specs=[pl.BlockSpec((d, tm), lambda i: (0, i)),
                  pl.BlockSpec(wq_t.shape, lambda i: (0, 0), pipeline_mode=pl.Buffered(1)),
                  pl.BlockSpec(sk.shape, lambda i: (0, 0, 0))],
        out_specs=[ospec] * 4,
        out_shape=[jax.ShapeDtypeStruct((PEER_HEADS, nk, nt), dt) for dt in (BF16, BF16, F32, F32)],
        scratch_shapes=[pltpu.VMEM((_CAND_ROWS, 128), F32)],
        compiler_params=_params(("arbitrary",)),
        name="route",
    )(ft, wq_t, sk)


def _peer_kernel(ft_ref, u_ref, v_ref, r2_ref, e2_ref, n1_ref, e1_ref, h1_ref, gate_ref, gf_ref,
                 o_ref, acc_ref, *, nk):
    j = pl.program_id(1)

    @pl.when(j == 0)
    def _():
        acc_ref[...] = jnp.zeros_like(acc_ref)

    tm = ft_ref.shape[1]
    n_i1 = u_ref.shape[0] // nk
    per = MXU_TILE // nk
    ps = []
    for c in range(n_i1 // per):
        st = jnp.dot(u_ref[c * MXU_TILE:(c + 1) * MXU_TILE, :], ft_ref[...],
                     preferred_element_type=F32)
        for a in range(per):
            i1 = j * n_i1 + c * per + a
            act = jax.nn.gelu(st[a * nk:(a + 1) * nk]).astype(BF16)
            w = jnp.zeros((nk, tm), BF16)
            for h in range(PEER_HEADS):
                cnt = n1_ref[h, pl.ds(i1, 1), :].astype(BF16)
                cf = e1_ref[h, pl.ds(i1, 1), :].astype(BF16)
                w = w + jnp.where(r2_ref[h] < cnt, e2_ref[h] * cf, jnp.zeros((), BF16))
            ps.append(w * act)
    p = jnp.concatenate(ps, axis=0)
    acc_ref[...] += lax.dot_general(p, v_ref[...], (((0,), (0,)), ((), ())), preferred_element_type=F32)

    @pl.when(j == pl.num_programs(1) - 1)
    def _():
        h2 = h1_ref[...] + gate_ref[...] * acc_ref[...]
        o_ref[...] = _rms(h2, gf_ref[...])


def _peer(ft, u, v, r2, e2, n1, e1, h1, mod6, g_final, tiles_per_batch_fn):
    d, nt = ft.shape
    ne = u.shape[0]
    nk = r2.shape[1]
    tm = 512
    et = 1024
    rspec = pl.BlockSpec((PEER_HEADS, nk, tm), lambda i, j: (0, 0, i), pipeline_mode=pl.Buffered(1))
    kern = functools.partial(_peer_kernel, nk=nk)
    return pl.pallas_call(
        kern,
        grid=(nt // tm, ne // et),
        in_specs=[pl.BlockSpec((d, tm), lambda i, j: (0, i)),
                  pl.BlockSpec((et, d), lambda i, j: (j, 0)),
                  pl.BlockSpec((et, d), lambda i, j: (j, 0)),
                  rspec, rspec, rspec, rspec,
                  pl.BlockSpec((tm, d), lambda i, j: (i, 0), pipeline_mode=pl.Buffered(1)),
                  pl.BlockSpec((None, 1, d), lambda i, j: (tiles_per_batch_fn(i, tm) * 6 + 5, 0, 0)),
                  pl.BlockSpec((1, d), lambda i, j: (0, 0))],
        out_specs=pl.BlockSpec((tm, d), lambda i, j: (i, 0), pipeline_mode=pl.Buffered(1)),
        out_shape=jax.ShapeDtypeStruct((nt, d), F32),
        scratch_shapes=[pltpu.VMEM((tm, d), F32)],
        compiler_params=_params(("arbitrary", "arbitrary")),
        name="peer",
    )(ft, u, v, r2, e2, n1, e1, h1, mod6, g_final.reshape(1, d))


def kernel(x, c, ctx, c_ctx, w_ada, b_ada, g_norm1, w_in, s5_a_re, s5_a_im, s5_log_dt, s5_b_re, s5_b_im, s5_c_re, s5_c_im, s5_d, s5_w_glu, lru_conv_w, lru_conv_b, lru_lambda, lru_w_r, lru_b_r, lru_w_i, lru_b_i, w_proj_a, w_proj_b, w_out, g_norm2, peer_w_query, peer_sub_keys, peer_u, peer_v, g_final):
    bsz, length, d = x.shape
    assert w_ada.shape[0] == 1, "single-layer kernel"
    rows = length // GRID_W
    w_s5 = s5_d.shape[-1]
    w_lru = lru_lambda.shape[-1]
    assert bsz < 8 and length % (GRID_W * LRU_ROW_BLOCK) == 0 and ctx.shape[1] % LRU_ROW_BLOCK == 0

    cin = jnp.zeros((8, d), F32).at[:bsz].set(c).at[bsz].set(c_ctx)
    mod6 = _ada(cin, w_ada[0], b_ada[0]).reshape(8 * 6, 1, d)

    w_in_b = w_in[0].astype(BF16)
    zu, zv, zg = _inproj(x, g_norm1[0], mod6, lambda b: b, w_in_b, w_s5, w_lru)
    zu_c, zv_c = _inproj(ctx, g_norm1[0], mod6, lambda b: bsz, w_in_b[:, :w_s5 + w_lru], w_s5, w_lru)

    mt, bp, cp, a16 = _s5_operators(s5_a_re[0], s5_a_im[0], s5_log_dt[0], s5_b_re[0], s5_b_im[0],
                                    s5_c_re[0], s5_c_im[0], s5_d[0])
    ys5 = _s5(zu, zu_c, mt, bp, cp, a16)

    lctx = ctx.shape[1]
    vc5 = jnp.zeros((1, lctx, 1, COLS_PER_TILE, w_lru), F32).at[0, :, 0, :bsz].set(zv_c.transpose(1, 0, 2))
    v5 = zv.reshape(bsz, rows, GRID_W // COLS_PER_TILE, COLS_PER_TILE, w_lru)
    zero_h = jnp.zeros((1, 1, w_lru), F32)
    yl = []
    for dr in range(2):
        args = (lru_conv_w[0], lru_conv_b[0], lru_lambda[0, dr], lru_w_r[0, dr], lru_b_r[0, dr],
                lru_w_i[0, dr], lru_b_i[0, dr])
        hc = _lru(vc5, *args, zero_h, rev=dr == 1, chain=False)
        h0 = hc[0, 0, :bsz].reshape(bsz, 1, w_lru)
        yl.append(_lru(v5, *args, h0, rev=dr == 1, chain=True).reshape(bsz, length, w_lru))

    h1, ft = _merge(x, ys5, yl[0], yl[1], zg, mod6, g_norm2[0], s5_w_glu[0].astype(BF16),
                    w_proj_a[0].astype(BF16), w_proj_b[0].astype(BF16), w_out[0].astype(BF16))

    r2, e2, n1, e1 = _route(ft, peer_w_query[0].T.astype(BF16), peer_sub_keys[0].astype(BF16))
    out = _peer(ft, peer_u[0].astype(BF16), peer_v[0].astype(BF16), r2, e2, n1, e1,
                h1.reshape(bsz * length, d), mod6, g_final,
                lambda i, tm: i // (length // tm))
    return out.reshape(bsz, length, d)
```

```python
import functools

import jax
import jax.numpy as jnp
from jax import lax
from jax.experimental import pallas as pl
from jax.experimental.pallas import tpu as pltpu

F32 = jnp.float32
BF16 = jnp.bfloat16
EPS = 1e-6
GRID_W = 64
S5_GROUP = 16
S5_STATE = 64
S5_CHUNK = 16
LRU_HEADS = 16
LRU_C = 8.0
CONV_W = 4
CONV_PAD_LO = (CONV_W - 1) // 2
N_KEYS = 128
PEER_HEADS = 8
PEER_TOPK = 16
COLS_PER_TILE = 8
MXU_TILE = 256
ROW_SLAB = 16
VMEM_LIMIT = 56 * 1024 * 1024


def _params(sem, vmem=VMEM_LIMIT):
    return pltpu.CompilerParams(dimension_semantics=sem, vmem_limit_bytes=vmem)


def _rms(x, g):
    return x * lax.rsqrt(jnp.mean(x * x, axis=-1, keepdims=True) + EPS) * g


def _ada_kernel(c_ref, w_ref, b_ref, o_ref):
    c = c_ref[...]
    sc = c * jax.nn.sigmoid(c)
    o_ref[...] = jnp.dot(sc.astype(BF16), w_ref[...].astype(BF16),
                         preferred_element_type=F32) + b_ref[...]


def _ada(cin, w, b):
    d, n = w.shape
    tn = 1536
    return pl.pallas_call(
        _ada_kernel,
        grid=(n // tn,),
        in_specs=[pl.BlockSpec((8, d), lambda j: (0, 0)),
                  pl.BlockSpec((d, tn), lambda j: (0, j)),
                  pl.BlockSpec((1, tn), lambda j: (0, j))],
        out_specs=pl.BlockSpec((8, tn), lambda j: (0, j)),
        out_shape=jax.ShapeDtypeStruct((8, n), F32),
        compiler_params=_params(("arbitrary",)),
        name="ada",
    )(cin, w, b.reshape(1, n))


def _inproj_kernel(x_ref, g_ref, sh_ref, sc_ref, w_ref, *refs, n_u, n_v):
    if len(refs) == 4:
        zu_ref, zv_ref, zg_ref, n_scr = refs
    else:
        zu_ref, zv_ref, n_scr = refs
        zg_ref = None
    j = pl.program_id(2)

    @pl.when(j == 0)
    def _():
        g, sh, scale1 = g_ref[...], sh_ref[...], 1.0 + sc_ref[...]

        def rows(i, _):
            r = pl.multiple_of(i * ROW_SLAB, ROW_SLAB)
            n_scr[pl.ds(r, ROW_SLAB), :] = (_rms(x_ref[pl.ds(r, ROW_SLAB), :], g) * scale1 + sh).astype(BF16)
            return 0

        lax.fori_loop(0, x_ref.shape[0] // ROW_SLAB, rows, 0)

    def project(out_ref):
        for q in range(w_ref.shape[1] // (2 * MXU_TILE)):
            cols = slice(q * 2 * MXU_TILE, (q + 1) * 2 * MXU_TILE)
            out_ref[:, cols] = jnp.dot(n_scr[...], w_ref[:, cols],
                                       preferred_element_type=F32).astype(out_ref.dtype)

    pl.when(j < n_u)(lambda: project(zu_ref))
    pl.when((j >= n_u) & (j < n_u + n_v))(lambda: project(zv_ref))
    if zg_ref is not None:
        pl.when(j >= n_u + n_v)(lambda: project(zg_ref))


def _inproj(x, g, mod6, mod_row, w_bf16, width_u, width_v):
    bsz, length, d = x.shape
    n = w_bf16.shape[1]
    tm = min(512, length)
    tn = 1024
    n_u, n_v = width_u // tn, width_v // tn
    n_g = n // tn - n_u - n_v
    kern = functools.partial(_inproj_kernel, n_u=n_u, n_v=n_v)
    row = lambda k: (lambda b, i, j: (mod_row(b) * 6 + k, 0, 0))
    out_specs = [pl.BlockSpec((None, tm, tn), lambda b, i, j: (b, i, jnp.minimum(j, n_u - 1))),
                 pl.BlockSpec((None, tm, tn), lambda b, i, j: (b, i, jnp.clip(j - n_u, 0, n_v - 1)))]
    out_shape = [jax.ShapeDtypeStruct((bsz, length, width_u), F32),
                 jax.ShapeDtypeStruct((bsz, length, width_v), F32)]
    if n_g:
        out_specs.append(pl.BlockSpec((None, tm, tn), lambda b, i, j: (b, i, jnp.maximum(j - n_u - n_v, 0))))
        out_shape.append(jax.ShapeDtypeStruct((bsz, length, n_g * tn), BF16))
    return pl.pallas_call(
        kern,
        grid=(bsz, length // tm, n // tn),
        in_specs=[pl.BlockSpec((None, tm, d), lambda b, i, j: (b, i, 0)),
                  pl.BlockSpec((1, d), lambda b, i, j: (0, 0)),
                  pl.BlockSpec((None, 1, d), row(0)),
                  pl.BlockSpec((None, 1, d), row(1)),
                  pl.BlockSpec((d, tn), lambda b, i, j: (0, j))],
        out_specs=out_specs,
        out_shape=out_shape,
        scratch_shapes=[pltpu.VMEM((tm, d), BF16)],
        compiler_params=_params(("arbitrary", "arbitrary", "arbitrary")),
        name="inproj",
    )(x, g.reshape(1, d), mod6, mod6, w_bf16)


def _s5_prep_kernel(are_ref, aim_ref, ldt_ref, bre_ref, bim_ref, cre_ref, cim_ref,
                    k_ref, cg_ref, bg_ref, a16_ref):
    nk = S5_CHUNK + 1
    dt = jnp.exp(ldt_ref[...])
    lre, lim = are_ref[...], aim_ref[...]
    xr, xi = lre * dt, lim * dt
    e1 = jnp.exp(xr)
    ar, ai = e1 * jnp.cos(xi), e1 * jnp.sin(xi)
    pows = [(jnp.ones_like(ar), jnp.zeros_like(ai))]
    for _ in range(nk - 1):
        kr, ki = pows[-1]
        pows.append((kr * ar - ki * ai, kr * ai + ki * ar))
    rep = lambda rows: jnp.concatenate([jnp.broadcast_to(r, (S5_GROUP, r.shape[1])) for r in rows], axis=0)
    pr, pim = rep([q[0] for q in pows]), rep([q[1] for q in pows])
    den = lre * lre + lim * lim
    qr = ((ar - 1.0) * lre + ai * lim) / den
    qi = (ai * lre - (ar - 1.0) * lim) / den
    bre, bim = bre_ref[...], bim_ref[...]
    bbr, bbi = qr * bre - qi * bim, qr * bim + qi * bre
    tile = lambda m: jnp.concatenate([m] * nk, axis=0)
    cr_t, ci_t, br_t, bi_t = tile(cre_ref[...]), tile(cim_ref[...]), tile(bbr), tile(bbi)
    cg = jnp.concatenate([cr_t * pr - ci_t * pim, -(cr_t * pim + ci_t * pr)], axis=1)
    bg = jnp.concatenate([br_t * pr - bi_t * pim, br_t * pim + bi_t * pr], axis=1)
    bcat = jnp.concatenate([bbr, bbi], axis=1)
    cg_ref[...] = cg
    bg_ref[...] = bg
    k_ref[...] = lax.dot_general(cg[:S5_CHUNK * S5_GROUP], bcat, (((1,), (1,)), ((), ())),
                                 precision=lax.Precision.HIGHEST, preferred_element_type=F32)
    n = S5_CHUNK * S5_GROUP
    a16_ref[...] = jnp.concatenate([pr[n:n + 1], pim[n:n + 1]], axis=1)


def _s5_operators(a_re, a_im, log_dt, b_re, b_im, c_re, c_im, d_skip):
    _, groups, p = a_re.shape
    h = S5_GROUP
    dg = 2 * groups
    nk = S5_CHUNK + 1
    vec = lambda t: t.reshape(dg, 1, p)
    tr = lambda t: jnp.swapaxes(t, -1, -2).reshape(dg, h, p)
    ldt = jnp.broadcast_to(log_dt[..., None], (2, groups, p))
    spec_v = pl.BlockSpec((None, 1, p), lambda i: (i, 0, 0))
    spec_m = pl.BlockSpec((None, h, p), lambda i: (i, 0, 0))
    kfl, cg, bg, a16 = pl.pallas_call(
        _s5_prep_kernel,
        grid=(dg,),
        in_specs=[spec_v, spec_v, spec_v, spec_m, spec_m, spec_m, spec_m],
        out_specs=[pl.BlockSpec((None, S5_CHUNK * h, h), lambda i: (i, 0, 0)),
                   pl.BlockSpec((None, nk * h, 2 * p), lambda i: (i, 0, 0)),
                   pl.BlockSpec((None, nk * h, 2 * p), lambda i: (i, 0, 0)),
                   pl.BlockSpec((None, 1, 2 * p), lambda i: (i, 0, 0))],
        out_shape=[jax.ShapeDtypeStruct((dg, S5_CHUNK * h, h), F32),
                   jax.ShapeDtypeStruct((dg, nk * h, 2 * p), F32),
                   jax.ShapeDtypeStruct((dg, nk * h, 2 * p), F32),
                   jax.ShapeDtypeStruct((dg, 1, 2 * p), F32)],
        compiler_params=_params(("arbitrary",)),
        name="s5_prep",
    )(vec(a_re), vec(a_im), vec(ldt), tr(b_re), tr(b_im),
      c_re.reshape(dg, h, p), c_im.reshape(dg, h, p))

    t = S5_CHUNK
    kfl = kfl.reshape(2, groups, t, h, h)
    cg = cg.reshape(2, groups, nk, h, 2 * p)
    bg = bg.reshape(2, groups, nk, h, 2 * p)
    ti = jnp.arange(t)
    lag = ti[:, None] - ti[None, :]
    kf = jnp.where((lag >= 0)[None, :, :, None, None], kfl[0][:, jnp.clip(lag, 0, t - 1)], 0.0)
    kb = jnp.where((lag <= 0)[None, :, :, None, None], kfl[1][:, jnp.clip(-lag, 0, t - 1)], 0.0)
    eye = (lag == 0)[None, :, :, None, None] * jnp.eye(h, dtype=F32)[None, None, None]
    m = kf + kb + eye * d_skip.reshape(groups, 1, 1, 1, h)
    mt = m.transpose(0, 2, 4, 1, 3).reshape(groups, t * h, t * h)
    bp_f = bg[0][:, t - 1 - ti].reshape(groups, t * h, 2 * p)
    bp_b = bg[1][:, ti].reshape(groups, t * h, 2 * p)
    bp = jnp.concatenate([bp_f, bp_b], axis=-1)
    cp_f = cg[0][:, ti + 1].reshape(groups, t * h, 2 * p)
    cp_b = cg[1][:, t - ti].reshape(groups, t * h, 2 * p)
    cp = jnp.concatenate([cp_f, cp_b], axis=-1).swapaxes(1, 2)
    return mt.astype(BF16), bp.astype(BF16), cp.astype(BF16), a16.reshape(2, groups, 1, 2 * p)


def _cplx_coef(a):
    p = a.shape[1] // 2
    lane = lax.broadcasted_iota(jnp.int32, a.shape, 1)
    sw = pltpu.roll(a, p, axis=1)
    return jnp.where(lane < p, a, sw), jnp.where(lane < p, -sw, a)


def _cplx_mul(a, x):
    c1, c2 = _cplx_coef(a)
    return c1 * x + c2 * pltpu.roll(x, x.shape[1] // 2, axis=1)


def _chunk_scan(x, a, reverse):
    n = x.shape[0]
    row = lax.broadcasted_iota(jnp.int32, (n, 1), 0)
    o = 1
    while o < n:
        if reverse:
            sh = jnp.where(row < n - o, pltpu.roll(x, n - o, axis=0), 0.0)
        else:
            sh = jnp.where(row >= o, pltpu.roll(x, o, axis=0), 0.0)
        x = x + _cplx_mul(a, sh)
        a = _cplx_mul(a, a)
        o *= 2
    return x


S5_GPB = 128 // S5_GROUP
S5_TPT = 128 // S5_GROUP


def _chunk_perm():
    n = S5_TPT * S5_GPB * S5_GROUP
    i = jnp.arange(n)
    t, g, h = i // (S5_GPB * S5_GROUP), (i // S5_GROUP) % S5_GPB, i % S5_GROUP
    dst = g * (S5_TPT * S5_GROUP) + t * S5_GROUP + h
    return jnp.zeros((n, n), BF16).at[i, dst].set(1.0)


def _load_chunks(src_ref, perm, dst_scr):
    nc = src_ref.shape[0] // S5_CHUNK
    halves = []
    for r in range(S5_CHUNK // S5_TPT):
        xcat = jnp.concatenate(
            [src_ref[pl.ds(r * S5_TPT + t, nc, stride=S5_CHUNK), :].astype(BF16) for t in range(S5_TPT)], axis=1)
        halves.append(jnp.dot(xcat, perm, preferred_element_type=F32).astype(BF16))
    for g in range(S5_GPB):
        dst_scr[g] = jnp.concatenate([hv[:, g * 128:(g + 1) * 128] for hv in halves], axis=1)


def _s5_kernel(u_ref, uc_ref, perm_ref, permt_ref, mt_ref, bp_ref, cp_ref, af_ref, ab_ref, y_ref,
               u_scr, uc_scr, y_scr):
    nc = u_ref.shape[0] // S5_CHUNK
    _load_chunks(u_ref, perm_ref[...], u_scr)
    _load_chunks(uc_ref, perm_ref[...], uc_scr)

    def group(g, _):
        y_scr[g] = _s5_group(u_scr[g], uc_scr[g], mt_ref[g], bp_ref[g], cp_ref[g], af_ref[g], ab_ref[g])
        return 0

    lax.fori_loop(0, S5_GPB, group, 0)

    for r in range(S5_CHUNK // S5_TPT):
        ycat = jnp.concatenate([y_scr[g][:, r * 128:(r + 1) * 128] for g in range(S5_GPB)], axis=1)
        back = jnp.dot(ycat, permt_ref[...], preferred_element_type=F32)
        for t in range(S5_TPT):
            y_ref[pl.ds(r * S5_TPT + t, nc, stride=S5_CHUNK), :] = back[:, t * 128:(t + 1) * 128]


def _s5_group(u, uc, mt, bp, cp, af, ab):
    nc = u.shape[0]
    ps = af.shape[1]
    v = jnp.dot(u, bp, preferred_element_type=F32)
    vc = jnp.dot(uc, bp, preferred_element_type=F32)
    ncc = vc.shape[0]
    hcf = _chunk_scan(vc[:, :ps], af, False)[ncc - 1:ncc]
    hcb = _chunk_scan(vc[:, ps:], ab, True)[0:1]
    row = lax.broadcasted_iota(jnp.int32, (nc, 1), 0)
    vf = v[:, :ps] + jnp.where(row == 0, _cplx_mul(af, hcf), 0.0)
    vb = v[:, ps:] + jnp.where(row == nc - 1, _cplx_mul(ab, hcb), 0.0)
    sf = _chunk_scan(vf, af, False)
    sb = _chunk_scan(vb, ab, True)
    sin_f = jnp.where(row == 0, hcf, pltpu.roll(sf, 1, axis=0))
    sin_b = jnp.where(row == nc - 1, hcb, pltpu.roll(sb, nc - 1, axis=0))
    s_in = jnp.concatenate([sin_f, sin_b], axis=1).astype(BF16)
    y = (jnp.dot(u, mt, preferred_element_type=F32)
         + jnp.dot(s_in, cp, preferred_element_type=F32))
    return y.astype(BF16)


def _s5(zu, zu_c, mt, bp, cp, a16):
    bsz, length, w = zu.shape
    lctx = zu_c.shape[1]
    nc, ncc = length // S5_CHUNK, lctx // S5_CHUNK
    cw = S5_CHUNK * S5_GROUP
    ps = a16.shape[-1]
    perm = _chunk_perm()
    once = lambda shape, imap: pl.BlockSpec(shape, imap, pipeline_mode=pl.Buffered(1))
    wspec = pl.BlockSpec((S5_GPB, cw, cw), lambda b, q: (q, 0, 0))
    return pl.pallas_call(
        _s5_kernel,
        grid=(bsz, w // 128),
        in_specs=[once((None, length, 128), lambda b, q: (b, 0, q)),
                  pl.BlockSpec((None, lctx, 128), lambda b, q: (b, 0, q)),
                  once(perm.shape, lambda b, q: (0, 0)),
                  once(perm.shape, lambda b, q: (0, 0)),
                  wspec, wspec, wspec,
                  pl.BlockSpec((None, S5_GPB, 1, ps), lambda b, q: (0, q, 0, 0)),
                  pl.BlockSpec((None, S5_GPB, 1, ps), lambda b, q: (1, q, 0, 0))],
        out_specs=once((None, length, 128), lambda b, q: (b, 0, q)),
        out_shape=jax.ShapeDtypeStruct((bsz, length, w), F32),
        scratch_shapes=[pltpu.VMEM((S5_GPB, nc, cw), BF16),
                        pltpu.VMEM((S5_GPB, ncc, cw), BF16),
                        pltpu.VMEM((S5_GPB, nc, cw), BF16)],
        compiler_params=_params(("arbitrary", "arbitrary")),
        name="s5",
    )(zu, zu_c, perm, perm.T, mt, bp, cp, a16, a16)


LRU_ROW_BLOCK = 16


def _neg_expm1(y, exp_y):
    series = -y * (1.0 + y * (0.5 + y * (1.0 / 6.0 + y * (1.0 / 24.0))))
    return jnp.where(y > -0.01, series, 1.0 - exp_y)


def _lru_kernel(v_ref, cw_ref, cb_ref, wr_ref, br_ref, wi_ref, bi_ref, lam_ref, h0_ref, o_ref,
                vpad, a_scr, b_scr, carry_scr, *, rows, rev, chain):
    ct = v_ref.shape[-1]
    cpt = COLS_PER_TILE
    rb = LRU_ROW_BLOCK
    nblk = rows // rb
    nslab = ct // MXU_TILE

    if chain:
        @pl.when(pl.program_id(2) == 0)
        def _():
            carry_scr[...] = h0_ref[...]

    zero_row = jnp.zeros((cpt, ct), F32)
    for k in range(CONV_PAD_LO):
        vpad[k] = zero_row
    for k in range(CONV_W - 1 - CONV_PAD_LO):
        vpad[CONV_PAD_LO + rows + k] = zero_row

    def copy(i, _):
        r0 = pl.multiple_of(i * rb, rb)
        vpad[pl.ds(r0 + CONV_PAD_LO, rb)] = v_ref[pl.ds(r0, rb)]
        return 0

    lax.fori_loop(0, nblk, copy, 0)

    cw = cw_ref[...]
    cb = cb_ref[...]
    nl = -lam_ref[...]
    softplus = jnp.maximum(nl, 0.0) + jnp.log1p(jnp.exp(-jnp.abs(nl)))
    c8 = -LRU_C * softplus
    b_r, b_i = br_ref[...], bi_ref[...]

    def block(i, carry):
        acc_a, acc_h = carry
        bi_ = (nblk - 1 - i) if rev else i
        r0 = pl.multiple_of(bi_ * rb, rb)
        x = cb
        for k in range(CONV_W):
            x = x + cw[k:k + 1] * vpad[pl.ds(r0 + k, rb)].reshape(rb * cpt, ct)
        xb = x.astype(BF16)
        pre_r = jnp.concatenate(
            [jnp.dot(xb[:, s * MXU_TILE:(s + 1) * MXU_TILE], wr_ref[s], preferred_element_type=F32)
             for s in range(nslab)], axis=1)
        pre_i = jnp.concatenate(
            [jnp.dot(xb[:, s * MXU_TILE:(s + 1) * MXU_TILE], wi_ref[s], preferred_element_type=F32)
             for s in range(nslab)], axis=1)
        r = jax.nn.sigmoid(pre_r + b_r)
        ig = jax.nn.sigmoid(pre_i + b_i)
        log_a = c8 * r
        a = jnp.exp(log_a)
        bx = jnp.sqrt(_neg_expm1(2.0 * log_a, a * a)) * (ig * x)
        q0 = pl.multiple_of(r0 * cpt, rb * cpt)
        a_scr[pl.ds(q0, rb * cpt), :] = a
        b_scr[pl.ds(q0, rb * cpt), :] = bx
        order = range(rb - 1, -1, -1) if rev else range(rb)
        for j in order:
            aj = a[j * cpt:(j + 1) * cpt]
            acc_h = aj * acc_h + bx[j * cpt:(j + 1) * cpt]
            acc_a = acc_a * aj
        return acc_a, acc_h

    acc_a, acc_h = lax.fori_loop(0, nblk, block, (jnp.ones((cpt, ct), F32), jnp.zeros((cpt, ct), F32)))

    if not chain:
        o_ref[...] = acc_h
        return

    sub = lax.broadcasted_iota(jnp.int32, (cpt, ct), 0)
    carry = carry_scr[...]
    h_in = jnp.zeros((cpt, ct), F32)
    for s in (range(cpt - 1, -1, -1) if rev else range(cpt)):
        h_in = jnp.where(sub == s, carry, h_in)
        carry = acc_a[s:s + 1] * carry + acc_h[s:s + 1]
    carry_scr[...] = carry

    def row(i, h):
        r = (rows - 1 - i) if rev else i
        q = pl.multiple_of(r * cpt, cpt)
        h = a_scr[pl.ds(q, cpt), :] * h + b_scr[pl.ds(q, cpt), :]
        o_ref[r] = h
        return h

    lax.fori_loop(0, rows, row, h_in, unroll=8)


def _block_diag(w, per):
    hh, n, _ = w.shape
    eye = jnp.eye(per, dtype=w.dtype)
    return jnp.einsum('gpij,pq->gpiqj', w.reshape(hh // per, per, n, n), eye).reshape(hh // per, per * n, per * n)


def _lru(v5, conv_w, conv_b, lam, w_r, b_r, w_i, b_i, h0, *, rev, chain):
    bsz, rows, ncg, cpt, w = v5.shape
    ct = 512
    nct = w // ct
    per = MXU_TILE // (w // LRU_HEADS)
    wr = _block_diag(w_r, per).astype(BF16)
    wi = _block_diag(w_i, per).astype(BF16)
    nslab = ct // MXU_TILE
    cgi = (lambda c: ncg - 1 - c) if rev else (lambda c: c)
    vec = lambda n: pl.BlockSpec((n, ct), lambda b, k, c: (0, k))
    in_specs = [pl.BlockSpec((None, rows, None, cpt, ct), lambda b, k, c: (b, 0, cgi(c), 0, k)),
                vec(CONV_W), vec(1),
                pl.BlockSpec((nslab, MXU_TILE, MXU_TILE), lambda b, k, c: (k, 0, 0)), vec(1),
                pl.BlockSpec((nslab, MXU_TILE, MXU_TILE), lambda b, k, c: (k, 0, 0)), vec(1),
                vec(1),
                pl.BlockSpec((None, 1, ct), lambda b, k, c: (b, 0, k))]
    if chain:
        out_spec = pl.BlockSpec((None, rows, None, cpt, ct), lambda b, k, c: (b, 0, cgi(c), 0, k))
        out_shape = jax.ShapeDtypeStruct(v5.shape, F32)
    else:
        out_spec = pl.BlockSpec((None, None, cpt, ct), lambda b, k, c: (b, c, 0, k))
        out_shape = jax.ShapeDtypeStruct((bsz, ncg, cpt, w), F32)
    kern = functools.partial(_lru_kernel, rows=rows, rev=rev, chain=chain)
    return pl.pallas_call(
        kern,
        grid=(bsz, nct, ncg),
        in_specs=in_specs,
        out_specs=out_spec,
        out_shape=out_shape,
        scratch_shapes=[pltpu.VMEM((rows + CONV_W - 1, cpt, ct), F32),
                        pltpu.VMEM((rows * cpt, ct), F32),
                        pltpu.VMEM((rows * cpt, ct), F32),
                        pltpu.VMEM((1, ct), F32)],
        compiler_params=_params(("arbitrary", "arbitrary", "arbitrary")),
        name="lru_rev" if rev else "lru_fwd",
    )(v5, conv_w, conv_b.reshape(1, w), wr, b_r.reshape(1, w), wi, b_i.reshape(1, w),
      lam.reshape(1, w), h0)


def _merge_kernel(x_ref, ys_ref, ylf_ref, ylb_ref, zg_ref, gate_ref, sh_ref, sc_ref, g2_ref,
                  wglu_ref, wa_ref, wb_ref, wo_ref, h1_ref, ft_ref, *, w_lru, d):
    za = jax.nn.gelu(ys_ref[...])
    ya = (za * jax.nn.sigmoid(jnp.dot(za.astype(BF16), wglu_ref[...],
                                      preferred_element_type=F32))).astype(BF16)
    zg = zg_ref[...]
    yb = ((ylf_ref[...] + ylb_ref[...]) * jax.nn.gelu(zg[:, :w_lru].astype(F32))).astype(BF16)
    ga = jax.nn.sigmoid(zg[:, w_lru:w_lru + d].astype(F32))
    gb = jax.nn.sigmoid(zg[:, w_lru + d:].astype(F32))
    m = (ga * jnp.dot(ya, wa_ref[...], preferred_element_type=F32)
         + gb * jnp.dot(yb, wb_ref[...], preferred_element_type=F32))
    o = jnp.dot(m.astype(BF16), wo_ref[...], preferred_element_type=F32)
    h1 = x_ref[...] + gate_ref[...] * o
    h1_ref[...] = h1
    f = _rms(h1, g2_ref[...]) * (1.0 + sc_ref[...]) + sh_ref[...]
    ft_ref[...] = f.T.astype(BF16)


def _merge(x, ys5, ylf, ylb, zg, mod6, g2, w_glu, w_a, w_b, w_o):
    bsz, length, d = x.shape
    w_s5 = ys5.shape[-1]
    w_lru = ylf.shape[-1]
    tm = 256
    nt = length // tm
    row = lambda k: (lambda b, i: (b * 6 + k, 0, 0))
    tok = lambda w: pl.BlockSpec((None, tm, w), lambda b, i: (b, i, 0))
    const = lambda shape: pl.BlockSpec(shape, lambda b, i: (0, 0), pipeline_mode=pl.Buffered(1))
    kern = functools.partial(_merge_kernel, w_lru=w_lru, d=d)
    return pl.pallas_call(
        kern,
        grid=(bsz, nt),
        in_specs=[tok(d), tok(w_s5), tok(w_lru), tok(w_lru), tok(zg.shape[-1]),
                  pl.BlockSpec((None, 1, d), row(2)), pl.BlockSpec((None, 1, d), row(3)),
                  pl.BlockSpec((None, 1, d), row(4)),
                  pl.BlockSpec((1, d), lambda b, i: (0, 0)),
                  const(w_glu.shape), const(w_a.shape), const(w_b.shape), const(w_o.shape)],
        out_specs=[tok(d), pl.BlockSpec((d, tm), lambda b, i: (0, b * nt + i))],
        out_shape=[jax.ShapeDtypeStruct((bsz, length, d), F32),
                   jax.ShapeDtypeStruct((d, bsz * length), BF16)],
        compiler_params=_params(("arbitrary", "arbitrary")),
        name="merge",
    )(x, ys5, ylf, ylb, zg, mod6, mod6, mod6, g2.reshape(1, d), w_glu, w_a, w_b, w_o)


def _top_rows(x, k):
    out = []
    for _ in range(k):
        m = jnp.max(x, axis=0, keepdims=True)
        out.append(m)
        x = jnp.where(x == m, -jnp.inf, x)
    return out


def _sort_pairs(n):
    out = []
    p = 1
    while p < n:
        k = p
        while k >= 1:
            for j in range(k % p, n - k, 2 * k):
                for i in range(min(k, n - j - k)):
                    if (i + j) // (2 * p) == (i + j + k) // (2 * p):
                        out.append((i + j, i + j + k))
            k //= 2
        p *= 2
    return out


def _top_rows_sorted(x, k):
    n_slab = x.shape[0] // 8
    slabs = [x[8 * r:8 * r + 8] for r in range(n_slab)]
    for a, b in _sort_pairs(n_slab):
        slabs[a], slabs[b] = jnp.maximum(slabs[a], slabs[b]), jnp.minimum(slabs[a], slabs[b])
    slabs.append(jnp.full_like(slabs[0], -jnp.inf))
    out = []
    for it in range(k):
        m = jnp.max(slabs[0], axis=0, keepdims=True)
        out.append(m)
        hit = slabs[0] == m
        for r in range(min(k - 1 - it, n_slab)):
            slabs[r] = jnp.where(hit, slabs[r + 1], slabs[r])
    return out


def _count_above(rows, x, strict):
    assert PEER_TOPK == 16
    above = (lambda a: a > x) if strict else (lambda a: a >= x)
    sel = jnp.where
    b3 = above(rows[7])
    b2 = above(sel(b3, rows[11], rows[3]))
    b1 = above(sel(b3, sel(b2, rows[13], rows[9]), sel(b2, rows[5], rows[1])))
    b0 = above(sel(b3, sel(b2, sel(b1, rows[14], rows[12]), sel(b1, rows[10], rows[8])),
                   sel(b2, sel(b1, rows[6], rows[4]), sel(b1, rows[2], rows[0]))))
    return (sel(b3, 8.0, 0.0) + sel(b2, 4.0, 0.0) + sel(b1, 2.0, 0.0) + sel(b0, 1.0, 0.0)
            + sel(above(rows[15]), 1.0, 0.0))


_CAND_PAIRS = [(j, k) for j in range(PEER_TOPK + 1) for k in range(PEER_TOPK + 1)
               if (j + 1) * (k + 1) <= PEER_TOPK + 1]
_CAND_ROWS = -(-len(_CAND_PAIRS) // 8) * 8


def _route_kernel(ft_ref, wq_ref, sk_ref, r2_ref, e2_ref, n1_ref, e1_ref, cand_scr):
    half = sk_ref.shape[2]
    tm = ft_ref.shape[1]
    qt = jnp.dot(wq_ref[...], ft_ref[...], preferred_element_type=F32).astype(BF16)
    for r in range(len(_CAND_PAIRS), _CAND_ROWS):
        cand_scr[r:r + 1, :] = jnp.full((1, 128), -jnp.inf, F32)
    for h in range(PEER_HEADS):
        s_full = [jnp.dot(sk_ref[side], qt[(2 * h + side) * half:(2 * h + side + 1) * half],
                          preferred_element_type=F32) for side in range(2)]
        for l in range(tm // 128):
            lanes = slice(l * 128, (l + 1) * 128)
            s = [sf[:, lanes] for sf in s_full]
            top0 = _top_rows_sorted(s[0], PEER_TOPK + 1)
            top1 = _top_rows_sorted(s[1], PEER_TOPK + 1)
            for r, (j, k) in enumerate(_CAND_PAIRS):
                cand_scr[r:r + 1, :] = top0[j] + top1[k]
            best = _top_rows(cand_scr[...], PEER_TOPK + 1)
            z = jnp.ones_like(best[0])
            for k in range(1, PEER_TOPK):
                z = z + jnp.exp(best[k] - best[0])
            t1 = 0.5 * (best[PEER_TOPK - 1] + best[PEER_TOPK]) - s[0]
            n1 = _count_above(top1, t1, strict=False)
            rank2 = _count_above(top1, s[1], strict=True)
            r2_ref[h, :, lanes] = rank2.astype(BF16)
            e2_ref[h, :, lanes] = jnp.exp(s[1] - top1[0]).astype(BF16)
            n1_ref[h, :, lanes] = n1
            e1_ref[h, :, lanes] = jnp.exp(s[0] - top0[0]) / z


def _route(ft, wq_t, sk):
    d, nt = ft.shape
    nk = sk.shape[1]
    tm = 256
    ospec = pl.BlockSpec((PEER_HEADS, nk, tm), lambda i: (0, 0, i))
    return pl.pallas_call(
        _route_kernel,
        grid=(nt // tm,),
        in_specs=[pl.BlockSpec((d, tm), lambda i: (0, i)),
                  pl.BlockSpec(wq_t.shape, lambda i: (0, 0), pipeline_mode=pl.Buffered(1)),
                  pl.BlockSpec(sk.shape, lambda i: (0, 0, 0))],
        out_specs=[ospec] * 4,
        out_shape=[jax.ShapeDtypeStruct((PEER_HEADS, nk, nt), dt) for dt in (BF16, BF16, F32, F32)],
        scratch_shapes=[pltpu.VMEM((_CAND_ROWS, 128), F32)],
        compiler_params=_params(("arbitrary",)),
        name="route",
    )(ft, wq_t, sk)


def _peer_kernel(ft_ref, u_ref, v_ref, r2_ref, e2_ref, n1_ref, e1_ref, h1_ref, gate_ref, gf_ref,
                 o_ref, acc_ref, *, nk):
    j = pl.program_id(1)

    @pl.when(j == 0)
    def _():
        acc_ref[...] = jnp.zeros_like(acc_ref)

    tm = ft_ref.shape[1]
    n_i1 = u_ref.shape[0] // nk
    per = MXU_TILE // nk
    ps = []
    for c in range(n_i1 // per):
        st = jnp.dot(u_ref[c * MXU_TILE:(c + 1) * MXU_TILE, :], ft_ref[...],
                     preferred_element_type=F32)
        for a in range(per):
            i1 = j * n_i1 + c * per + a
            act = jax.nn.gelu(st[a * nk:(a + 1) * nk].astype(BF16))
            w = jnp.zeros((nk, tm), BF16)
            for h in range(PEER_HEADS):
                cnt = n1_ref[h, pl.ds(i1, 1), :].astype(BF16)
                cf = e1_ref[h, pl.ds(i1, 1), :].astype(BF16)
                w = w + jnp.where(r2_ref[h] < cnt, e2_ref[h] * cf, jnp.zeros((), BF16))
            ps.append(w * act)
    p = jnp.concatenate(ps, axis=0)
    acc_ref[...] += lax.dot_general(p, v_ref[...], (((0,), (0,)), ((), ())), preferred_element_type=F32)

    @pl.when(j == pl.num_programs(1) - 1)
    def _():
        gate, gf = gate_ref[...], gf_ref[...]

        def rows(i, _):
            r = pl.multiple_of(i * ROW_SLAB, ROW_SLAB)
            h2 = h1_ref[pl.ds(r, ROW_SLAB), :] + gate * acc_ref[pl.ds(r, ROW_SLAB), :]
            o_ref[pl.ds(r, ROW_SLAB), :] = _rms(h2, gf)
            return 0

        lax.fori_loop(0, tm // ROW_SLAB, rows, 0)


def _peer(ft, u, v, r2, e2, n1, e1, h1, mod6, g_final, tiles_per_batch_fn):
    d, nt = ft.shape
    ne = u.shape[0]
    nk = r2.shape[1]
    tm = 512
    et = 1024
    rspec = pl.BlockSpec((PEER_HEADS, nk, tm), lambda i, j: (0, 0, i), pipeline_mode=pl.Buffered(1))
    kern = functools.partial(_peer_kernel, nk=nk)
    return pl.pallas_call(
        kern,
        grid=(nt // tm, ne // et),
        in_specs=[pl.BlockSpec((d, tm), lambda i, j: (0, i)),
                  pl.BlockSpec((et, d), lambda i, j: (j, 0)),
                  pl.BlockSpec((et, d), lambda i, j: (j, 0)),
                  rspec, rspec, rspec, rspec,
                  pl.BlockSpec((tm, d), lambda i, j: (i, 0), pipeline_mode=pl.Buffered(1)),
                  pl.BlockSpec((None, 1, d), lambda i, j: (tiles_per_batch_fn(i, tm) * 6 + 5, 0, 0)),
                  pl.BlockSpec((1, d), lambda i, j: (0, 0))],
        out_specs=pl.BlockSpec((tm, d), lambda i, j: (i, 0), pipeline_mode=pl.Buffered(1)),
        out_shape=jax.ShapeDtypeStruct((nt, d), F32),
        scratch_shapes=[pltpu.VMEM((tm, d), F32)],
        compiler_params=_params(("arbitrary", "arbitrary")),
        name="peer",
    )(ft, u, v, r2, e2, n1, e1, h1, mod6, g_final.reshape(1, d))


def kernel(x, c, ctx, c_ctx, w_ada, b_ada, g_norm1, w_in, s5_a_re, s5_a_im, s5_log_dt, s5_b_re, s5_b_im, s5_c_re, s5_c_im, s5_d, s5_w_glu, lru_conv_w, lru_conv_b, lru_lambda, lru_w_r, lru_b_r, lru_w_i, lru_b_i, w_proj_a, w_proj_b, w_out, g_norm2, peer_w_query, peer_sub_keys, peer_u, peer_v, g_final):
    bsz, length, d = x.shape
    assert w_ada.shape[0] == 1, "single-layer kernel"
    rows = length // GRID_W
    w_s5 = s5_d.shape[-1]
    w_lru = lru_lambda.shape[-1]
    assert bsz < 8 and length % (GRID_W * LRU_ROW_BLOCK) == 0 and ctx.shape[1] % LRU_ROW_BLOCK == 0

    cin = jnp.zeros((8, d), F32).at[:bsz].set(c).at[bsz].set(c_ctx)
    mod6 = _ada(cin, w_ada[0], b_ada[0]).reshape(8 * 6, 1, d)

    w_in_b = w_in[0].astype(BF16)
    zu, zv, zg = _inproj(x, g_norm1[0], mod6, lambda b: b, w_in_b, w_s5, w_lru)
    zu_c, zv_c = _inproj(ctx, g_norm1[0], mod6, lambda b: bsz, w_in_b[:, :w_s5 + w_lru], w_s5, w_lru)

    mt, bp, cp, a16 = _s5_operators(s5_a_re[0], s5_a_im[0], s5_log_dt[0], s5_b_re[0], s5_b_im[0],
                                    s5_c_re[0], s5_c_im[0], s5_d[0])
    ys5 = _s5(zu, zu_c, mt, bp, cp, a16)

    lctx = ctx.shape[1]
    vc5 = jnp.zeros((1, lctx, 1, COLS_PER_TILE, w_lru), F32).at[0, :, 0, :bsz].set(zv_c.transpose(1, 0, 2))
    v5 = zv.reshape(bsz, rows, GRID_W // COLS_PER_TILE, COLS_PER_TILE, w_lru)
    zero_h = jnp.zeros((1, 1, w_lru), F32)
    yl = []
    for dr in range(2):
        args = (lru_conv_w[0], lru_conv_b[0], lru_lambda[0, dr], lru_w_r[0, dr], lru_b_r[0, dr],
                lru_w_i[0, dr], lru_b_i[0, dr])
        hc = _lru(vc5, *args, zero_h, rev=dr == 1, chain=False)
        h0 = hc[0, 0, :bsz].reshape(bsz, 1, w_lru)
        yl.append(_lru(v5, *args, h0, rev=dr == 1, chain=True).reshape(bsz, length, w_lru))

    h1, ft = _merge(x, ys5, yl[0], yl[1], zg, mod6, g_norm2[0], s5_w_glu[0].astype(BF16),
                    w_proj_a[0].astype(BF16), w_proj_b[0].astype(BF16), w_out[0].astype(BF16))

    r2, e2, n1, e1 = _route(ft, peer_w_query[0].T.astype(BF16), peer_sub_keys[0].astype(BF16))
    out = _peer(ft, peer_u[0].astype(BF16), peer_v[0].astype(BF16), r2, e2, n1, e1,
                h1.reshape(bsz * length, d), mod6, g_final,
                lambda i, tm: i // (length // tm))
    return out.reshape(bsz, length, d)
```

```python
import functools

import jax
import jax.numpy as jnp
from jax import lax
from jax.experimental import pallas as pl
from jax.experimental.pallas import tpu as pltpu

F32 = jnp.float32
BF16 = jnp.bfloat16
EPS = 1e-6
GRID_W = 64
S5_GROUP = 16
S5_STATE = 64
S5_CHUNK = 16
LRU_HEADS = 16
LRU_C = 8.0
CONV_W = 4
CONV_PAD_LO = (CONV_W - 1) // 2
N_KEYS = 128
PEER_HEADS = 8
PEER_TOPK = 16
COLS_PER_TILE = 8
MXU_TILE = 256
ROW_SLAB = 16
VMEM_LIMIT = 56 * 1024 * 1024


def _params(sem, vmem=VMEM_LIMIT):
    return pltpu.CompilerParams(dimension_semantics=sem, vmem_limit_bytes=vmem)


def _rms(x, g):
    return x * lax.rsqrt(jnp.mean(x * x, axis=-1, keepdims=True) + EPS) * g


def _ada_kernel(c_ref, w_ref, b_ref, o_ref):
    c = c_ref[...]
    sc = c * jax.nn.sigmoid(c)
    o_ref[...] = jnp.dot(sc.astype(BF16), w_ref[...].astype(BF16),
                         preferred_element_type=F32) + b_ref[...]


def _ada(cin, w, b):
    d, n = w.shape
    tn = 1536
    return pl.pallas_call(
        _ada_kernel,
        grid=(n // tn,),
        in_specs=[pl.BlockSpec((8, d), lambda j: (0, 0)),
                  pl.BlockSpec((d, tn), lambda j: (0, j)),
                  pl.BlockSpec((1, tn), lambda j: (0, j))],
        out_specs=pl.BlockSpec((8, tn), lambda j: (0, j)),
        out_shape=jax.ShapeDtypeStruct((8, n), F32),
        compiler_params=_params(("arbitrary",)),
        name="ada",
    )(cin, w, b.reshape(1, n))


def _inproj_kernel(x_ref, g_ref, sh_ref, sc_ref, w_ref, *refs, n_u, n_v):
    if len(refs) == 4:
        zu_ref, zv_ref, zg_ref, n_scr = refs
    else:
        zu_ref, zv_ref, n_scr = refs
        zg_ref = None
    j = pl.program_id(2)

    @pl.when(j == 0)
    def _():
        g, sh, scale1 = g_ref[...], sh_ref[...], 1.0 + sc_ref[...]

        def rows(i, _):
            r = pl.multiple_of(i * ROW_SLAB, ROW_SLAB)
            n_scr[pl.ds(r, ROW_SLAB), :] = (_rms(x_ref[pl.ds(r, ROW_SLAB), :], g) * scale1 + sh).astype(BF16)
            return 0

        lax.fori_loop(0, x_ref.shape[0] // ROW_SLAB, rows, 0)

    def project(out_ref):
        for q in range(w_ref.shape[1] // (2 * MXU_TILE)):
            cols = slice(q * 2 * MXU_TILE, (q + 1) * 2 * MXU_TILE)
            out_ref[:, cols] = jnp.dot(n_scr[...], w_ref[:, cols],
                                       preferred_element_type=F32).astype(out_ref.dtype)

    pl.when(j < n_u)(lambda: project(zu_ref))
    pl.when((j >= n_u) & (j < n_u + n_v))(lambda: project(zv_ref))
    if zg_ref is not None:
        pl.when(j >= n_u + n_v)(lambda: project(zg_ref))


def _inproj(x, g, mod6, mod_row, w_bf16, width_u, width_v):
    bsz, length, d = x.shape
    n = w_bf16.shape[1]
    tm = min(1024, length)
    tn = 1024
    n_u, n_v = width_u // tn, width_v // tn
    n_g = n // tn - n_u - n_v
    kern = functools.partial(_inproj_kernel, n_u=n_u, n_v=n_v)
    row = lambda k: (lambda b, i, j: (mod_row(b) * 6 + k, 0, 0))
    once = pl.Buffered(1)
    out_specs = [pl.BlockSpec((None, tm, tn), lambda b, i, j: (b, i, jnp.minimum(j, n_u - 1)), pipeline_mode=once),
                 pl.BlockSpec((None, tm, tn), lambda b, i, j: (b, i, jnp.clip(j - n_u, 0, n_v - 1)),
                              pipeline_mode=once)]
    out_shape = [jax.ShapeDtypeStruct((bsz, length, width_u), F32),
                 jax.ShapeDtypeStruct((bsz, length, width_v), F32)]
    if n_g:
        out_specs.append(pl.BlockSpec((None, tm, tn), lambda b, i, j: (b, i, jnp.maximum(j - n_u - n_v, 0))))
        out_shape.append(jax.ShapeDtypeStruct((bsz, length, n_g * tn), BF16))
    return pl.pallas_call(
        kern,
        grid=(bsz, length // tm, n // tn),
        in_specs=[pl.BlockSpec((None, tm, d), lambda b, i, j: (b, i, 0)),
                  pl.BlockSpec((1, d), lambda b, i, j: (0, 0)),
                  pl.BlockSpec((None, 1, d), row(0)),
                  pl.BlockSpec((None, 1, d), row(1)),
                  pl.BlockSpec((d, tn), lambda b, i, j: (0, j))],
        out_specs=out_specs,
        out_shape=out_shape,
        scratch_shapes=[pltpu.VMEM((tm, d), BF16)],
        compiler_params=_params(("arbitrary", "arbitrary", "arbitrary")),
        name="inproj",
    )(x, g.reshape(1, d), mod6, mod6, w_bf16)


def _s5_prep_kernel(are_ref, aim_ref, ldt_ref, bre_ref, bim_ref, cre_ref, cim_ref,
                    k_ref, cg_ref, bg_ref, a16_ref):
    nk = S5_CHUNK + 1
    dt = jnp.exp(ldt_ref[...])
    lre, lim = are_ref[...], aim_ref[...]
    xr, xi = lre * dt, lim * dt
    e1 = jnp.exp(xr)
    ar, ai = e1 * jnp.cos(xi), e1 * jnp.sin(xi)
    pows = [(jnp.ones_like(ar), jnp.zeros_like(ai))]
    for _ in range(nk - 1):
        kr, ki = pows[-1]
        pows.append((kr * ar - ki * ai, kr * ai + ki * ar))
    rep = lambda rows: jnp.concatenate([jnp.broadcast_to(r, (S5_GROUP, r.shape[1])) for r in rows], axis=0)
    pr, pim = rep([q[0] for q in pows]), rep([q[1] for q in pows])
    den = lre * lre + lim * lim
    qr = ((ar - 1.0) * lre + ai * lim) / den
    qi = (ai * lre - (ar - 1.0) * lim) / den
    bre, bim = bre_ref[...], bim_ref[...]
    bbr, bbi = qr * bre - qi * bim, qr * bim + qi * bre
    tile = lambda m: jnp.concatenate([m] * nk, axis=0)
    cr_t, ci_t, br_t, bi_t = tile(cre_ref[...]), tile(cim_ref[...]), tile(bbr), tile(bbi)
    cg = jnp.concatenate([cr_t * pr - ci_t * pim, -(cr_t * pim + ci_t * pr)], axis=1)
    bg = jnp.concatenate([br_t * pr - bi_t * pim, br_t * pim + bi_t * pr], axis=1)
    bcat = jnp.concatenate([bbr, bbi], axis=1)
    cg_ref[...] = cg
    bg_ref[...] = bg
    k_ref[...] = lax.dot_general(cg[:S5_CHUNK * S5_GROUP], bcat, (((1,), (1,)), ((), ())),
                                 precision=lax.Precision.HIGHEST, preferred_element_type=F32)
    n = S5_CHUNK * S5_GROUP
    a16_ref[...] = jnp.concatenate([pr[n:n + 1], pim[n:n + 1]], axis=1)


def _s5_operators(a_re, a_im, log_dt, b_re, b_im, c_re, c_im, d_skip):
    _, groups, p = a_re.shape
    h = S5_GROUP
    dg = 2 * groups
    nk = S5_CHUNK + 1
    vec = lambda t: t.reshape(dg, 1, p)
    tr = lambda t: jnp.swapaxes(t, -1, -2).reshape(dg, h, p)
    ldt = jnp.broadcast_to(log_dt[..., None], (2, groups, p))
    spec_v = pl.BlockSpec((None, 1, p), lambda i: (i, 0, 0))
    spec_m = pl.BlockSpec((None, h, p), lambda i: (i, 0, 0))
    kfl, cg, bg, a16 = pl.pallas_call(
        _s5_prep_kernel,
        grid=(dg,),
        in_specs=[spec_v, spec_v, spec_v, spec_m, spec_m, spec_m, spec_m],
        out_specs=[pl.BlockSpec((None, S5_CHUNK * h, h), lambda i: (i, 0, 0)),
                   pl.BlockSpec((None, nk * h, 2 * p), lambda i: (i, 0, 0)),
                   pl.BlockSpec((None, nk * h, 2 * p), lambda i: (i, 0, 0)),
                   pl.BlockSpec((None, 1, 2 * p), lambda i: (i, 0, 0))],
        out_shape=[jax.ShapeDtypeStruct((dg, S5_CHUNK * h, h), F32),
                   jax.ShapeDtypeStruct((dg, nk * h, 2 * p), F32),
                   jax.ShapeDtypeStruct((dg, nk * h, 2 * p), F32),
                   jax.ShapeDtypeStruct((dg, 1, 2 * p), F32)],
        compiler_params=_params(("arbitrary",)),
        name="s5_prep",
    )(vec(a_re), vec(a_im), vec(ldt), tr(b_re), tr(b_im),
      c_re.reshape(dg, h, p), c_im.reshape(dg, h, p))

    t = S5_CHUNK
    kfl = kfl.reshape(2, groups, t, h, h)
    cg = cg.reshape(2, groups, nk, h, 2 * p)
    bg = bg.reshape(2, groups, nk, h, 2 * p)
    ti = jnp.arange(t)
    lag = ti[:, None] - ti[None, :]
    kf = jnp.where((lag >= 0)[None, :, :, None, None], kfl[0][:, jnp.clip(lag, 0, t - 1)], 0.0)
    kb = jnp.where((lag <= 0)[None, :, :, None, None], kfl[1][:, jnp.clip(-lag, 0, t - 1)], 0.0)
    eye = (lag == 0)[None, :, :, None, None] * jnp.eye(h, dtype=F32)[None, None, None]
    m = kf + kb + eye * d_skip.reshape(groups, 1, 1, 1, h)
    mt = m.transpose(0, 2, 4, 1, 3).reshape(groups, t * h, t * h)
    bp_f = bg[0][:, t - 1 - ti].reshape(groups, t * h, 2 * p)
    bp_b = bg[1][:, ti].reshape(groups, t * h, 2 * p)
    bp = jnp.concatenate([bp_f, bp_b], axis=-1)
    cp_f = cg[0][:, ti + 1].reshape(groups, t * h, 2 * p)
    cp_b = cg[1][:, t - ti].reshape(groups, t * h, 2 * p)
    cp = jnp.concatenate([cp_f, cp_b], axis=-1).swapaxes(1, 2)
    return mt.astype(BF16), bp.astype(BF16), cp.astype(BF16), a16.reshape(2, groups, 1, 2 * p)


def _cplx_coef(a):
    p = a.shape[1] // 2
    lane = lax.broadcasted_iota(jnp.int32, a.shape, 1)
    sw = pltpu.roll(a, p, axis=1)
    return jnp.where(lane < p, a, sw), jnp.where(lane < p, -sw, a)


def _cplx_mul(a, x):
    c1, c2 = _cplx_coef(a)
    return c1 * x + c2 * pltpu.roll(x, x.shape[1] // 2, axis=1)


def _chunk_scan(x, a, reverse):
    n = x.shape[0]
    row = lax.broadcasted_iota(jnp.int32, (n, 1), 0)
    o = 1
    while o < n:
        if reverse:
            sh = jnp.where(row < n - o, pltpu.roll(x, n - o, axis=0), 0.0)
        else:
            sh = jnp.where(row >= o, pltpu.roll(x, o, axis=0), 0.0)
        x = x + _cplx_mul(a, sh)
        a = _cplx_mul(a, a)
        o *= 2
    return x


S5_GPB = 128 // S5_GROUP
S5_TPT = 128 // S5_GROUP


def _chunk_perm():
    n = S5_TPT * S5_GPB * S5_GROUP
    i = jnp.arange(n)
    t, g, h = i // (S5_GPB * S5_GROUP), (i // S5_GROUP) % S5_GPB, i % S5_GROUP
    dst = g * (S5_TPT * S5_GROUP) + t * S5_GROUP + h
    return jnp.zeros((n, n), BF16).at[i, dst].set(1.0)


def _load_chunks(src_ref, perm, dst_scr):
    nc = src_ref.shape[0] // S5_CHUNK
    halves = []
    for r in range(S5_CHUNK // S5_TPT):
        xcat = jnp.concatenate(
            [src_ref[pl.ds(r * S5_TPT + t, nc, stride=S5_CHUNK), :].astype(BF16) for t in range(S5_TPT)], axis=1)
        halves.append(jnp.dot(xcat, perm, preferred_element_type=F32).astype(BF16))
    for g in range(S5_GPB):
        dst_scr[g] = jnp.concatenate([hv[:, g * 128:(g + 1) * 128] for hv in halves], axis=1)


def _s5_kernel(u_ref, uc_ref, perm_ref, permt_ref, mt_ref, bp_ref, cp_ref, af_ref, ab_ref, y_ref,
               u_scr, uc_scr, y_scr):
    nc = u_ref.shape[0] // S5_CHUNK
    _load_chunks(u_ref, perm_ref[...], u_scr)
    _load_chunks(uc_ref, perm_ref[...], uc_scr)

    def group(g, _):
        y_scr[g] = _s5_group(u_scr[g], uc_scr[g], mt_ref[g], bp_ref[g], cp_ref[g], af_ref[g], ab_ref[g])
        return 0

    lax.fori_loop(0, S5_GPB, group, 0)

    for r in range(S5_CHUNK // S5_TPT):
        ycat = jnp.concatenate([y_scr[g][:, r * 128:(r + 1) * 128] for g in range(S5_GPB)], axis=1)
        back = jnp.dot(ycat, permt_ref[...], preferred_element_type=F32)
        for t in range(S5_TPT):
            y_ref[pl.ds(r * S5_TPT + t, nc, stride=S5_CHUNK), :] = back[:, t * 128:(t + 1) * 128]


def _s5_group(u, uc, mt, bp, cp, af, ab):
    nc = u.shape[0]
    ps = af.shape[1]
    v = jnp.dot(u, bp, preferred_element_type=F32)
    vc = jnp.dot(uc, bp, preferred_element_type=F32)
    ncc = vc.shape[0]
    hcf = _chunk_scan(vc[:, :ps], af, False)[ncc - 1:ncc]
    hcb = _chunk_scan(vc[:, ps:], ab, True)[0:1]
    row = lax.broadcasted_iota(jnp.int32, (nc, 1), 0)
    vf = v[:, :ps] + jnp.where(row == 0, _cplx_mul(af, hcf), 0.0)
    vb = v[:, ps:] + jnp.where(row == nc - 1, _cplx_mul(ab, hcb), 0.0)
    sf = _chunk_scan(vf, af, False)
    sb = _chunk_scan(vb, ab, True)
    sin_f = jnp.where(row == 0, hcf, pltpu.roll(sf, 1, axis=0))
    sin_b = jnp.where(row == nc - 1, hcb, pltpu.roll(sb, nc - 1, axis=0))
    s_in = jnp.concatenate([sin_f, sin_b], axis=1).astype(BF16)
    y = (jnp.dot(u, mt, preferred_element_type=F32)
         + jnp.dot(s_in, cp, preferred_element_type=F32))
    return y.astype(BF16)


def _s5(zu, zu_c, mt, bp, cp, a16):
    bsz, length, w = zu.shape
    lctx = zu_c.shape[1]
    nc, ncc = length // S5_CHUNK, lctx // S5_CHUNK
    cw = S5_CHUNK * S5_GROUP
    ps = a16.shape[-1]
    perm = _chunk_perm()
    once = lambda shape, imap: pl.BlockSpec(shape, imap, pipeline_mode=pl.Buffered(1))
    wspec = pl.BlockSpec((S5_GPB, cw, cw), lambda b, q: (q, 0, 0))
    return pl.pallas_call(
        _s5_kernel,
        grid=(bsz, w // 128),
        in_specs=[once((None, length, 128), lambda b, q: (b, 0, q)),
                  pl.BlockSpec((None, lctx, 128), lambda b, q: (b, 0, q)),
                  once(perm.shape, lambda b, q: (0, 0)),
                  once(perm.shape, lambda b, q: (0, 0)),
                  wspec, wspec, wspec,
                  pl.BlockSpec((None, S5_GPB, 1, ps), lambda b, q: (0, q, 0, 0)),
                  pl.BlockSpec((None, S5_GPB, 1, ps), lambda b, q: (1, q, 0, 0))],
        out_specs=once((None, length, 128), lambda b, q: (b, 0, q)),
        out_shape=jax.ShapeDtypeStruct((bsz, length, w), F32),
        scratch_shapes=[pltpu.VMEM((S5_GPB, nc, cw), BF16),
                        pltpu.VMEM((S5_GPB, ncc, cw), BF16),
                        pltpu.VMEM((S5_GPB, nc, cw), BF16)],
        compiler_params=_params(("arbitrary", "arbitrary")),
        name="s5",
    )(zu, zu_c, perm, perm.T, mt, bp, cp, a16, a16)


LRU_ROW_BLOCK = 16


def _neg_expm1(y, exp_y):
    series = -y * (1.0 + y * (0.5 + y * (1.0 / 6.0 + y * (1.0 / 24.0))))
    return jnp.where(y > -0.01, series, 1.0 - exp_y)


def _lru_kernel(v_ref, cw_ref, cb_ref, wr_ref, br_ref, wi_ref, bi_ref, lam_ref, h0_ref, o_ref,
                vpad, a_scr, b_scr, carry_scr, *, rows, rev, chain):
    ct = v_ref.shape[-1]
    cpt = COLS_PER_TILE
    rb = LRU_ROW_BLOCK
    nblk = rows // rb
    nslab = ct // MXU_TILE

    if chain:
        @pl.when(pl.program_id(2) == 0)
        def _():
            carry_scr[...] = h0_ref[...]

    zero_row = jnp.zeros((cpt, ct), F32)
    for k in range(CONV_PAD_LO):
        vpad[k] = zero_row
    for k in range(CONV_W - 1 - CONV_PAD_LO):
        vpad[CONV_PAD_LO + rows + k] = zero_row

    def copy(i, _):
        r0 = pl.multiple_of(i * rb, rb)
        vpad[pl.ds(r0 + CONV_PAD_LO, rb)] = v_ref[pl.ds(r0, rb)]
        return 0

    lax.fori_loop(0, nblk, copy, 0)

    cw = cw_ref[...]
    cb = cb_ref[...]
    nl = -lam_ref[...]
    softplus = jnp.maximum(nl, 0.0) + jnp.log1p(jnp.exp(-jnp.abs(nl)))
    c8 = -LRU_C * softplus
    b_r, b_i = br_ref[...], bi_ref[...]

    def block(i, carry):
        acc_a, acc_h = carry
        bi_ = (nblk - 1 - i) if rev else i
        r0 = pl.multiple_of(bi_ * rb, rb)
        x = cb
        for k in range(CONV_W):
            x = x + cw[k:k + 1] * vpad[pl.ds(r0 + k, rb)].reshape(rb * cpt, ct)
        xb = x.astype(BF16)
        pre_r = jnp.concatenate(
            [jnp.dot(xb[:, s * MXU_TILE:(s + 1) * MXU_TILE], wr_ref[s], preferred_element_type=F32)
             for s in range(nslab)], axis=1)
        pre_i = jnp.concatenate(
            [jnp.dot(xb[:, s * MXU_TILE:(s + 1) * MXU_TILE], wi_ref[s], preferred_element_type=F32)
             for s in range(nslab)], axis=1)
        r = jax.nn.sigmoid(pre_r + b_r)
        ig = jax.nn.sigmoid(pre_i + b_i)
        log_a = c8 * r
        a = jnp.exp(log_a)
        bx = jnp.sqrt(_neg_expm1(2.0 * log_a, a * a)) * (ig * x)
        q0 = pl.multiple_of(r0 * cpt, rb * cpt)
        a_scr[pl.ds(q0, rb * cpt), :] = a
        b_scr[pl.ds(q0, rb * cpt), :] = bx
        order = range(rb - 1, -1, -1) if rev else range(rb)
        for j in order:
            aj = a[j * cpt:(j + 1) * cpt]
            acc_h = aj * acc_h + bx[j * cpt:(j + 1) * cpt]
            acc_a = acc_a * aj
        return acc_a, acc_h

    acc_a, acc_h = lax.fori_loop(0, nblk, block, (jnp.ones((cpt, ct), F32), jnp.zeros((cpt, ct), F32)))

    if not chain:
        o_ref[...] = acc_h
        return

    sub = lax.broadcasted_iota(jnp.int32, (cpt, ct), 0)
    carry = carry_scr[...]
    h_in = jnp.zeros((cpt, ct), F32)
    for s in (range(cpt - 1, -1, -1) if rev else range(cpt)):
        h_in = jnp.where(sub == s, carry, h_in)
        carry = acc_a[s:s + 1] * carry + acc_h[s:s + 1]
    carry_scr[...] = carry

    def row(i, h):
        r = (rows - 1 - i) if rev else i
        q = pl.multiple_of(r * cpt, cpt)
        h = a_scr[pl.ds(q, cpt), :] * h + b_scr[pl.ds(q, cpt), :]
        o_ref[r] = h
        return h

    lax.fori_loop(0, rows, row, h_in, unroll=8)


def _block_diag(w, per):
    hh, n, _ = w.shape
    eye = jnp.eye(per, dtype=w.dtype)
    return jnp.einsum('gpij,pq->gpiqj', w.reshape(hh // per, per, n, n), eye).reshape(hh // per, per * n, per * n)


def _lru(v5, conv_w, conv_b, lam, w_r, b_r, w_i, b_i, h0, *, rev, chain):
    bsz, rows, ncg, cpt, w = v5.shape
    ct = 512
    nct = w // ct
    per = MXU_TILE // (w // LRU_HEADS)
    wr = _block_diag(w_r, per).astype(BF16)
    wi = _block_diag(w_i, per).astype(BF16)
    nslab = ct // MXU_TILE
    cgi = (lambda c: ncg - 1 - c) if rev else (lambda c: c)
    vec = lambda n: pl.BlockSpec((n, ct), lambda b, k, c: (0, k))
    in_specs = [pl.BlockSpec((None, rows, None, cpt, ct), lambda b, k, c: (b, 0, cgi(c), 0, k)),
                vec(CONV_W), vec(1),
                pl.BlockSpec((nslab, MXU_TILE, MXU_TILE), lambda b, k, c: (k, 0, 0)), vec(1),
                pl.BlockSpec((nslab, MXU_TILE, MXU_TILE), lambda b, k, c: (k, 0, 0)), vec(1),
                vec(1),
                pl.BlockSpec((None, 1, ct), lambda b, k, c: (b, 0, k))]
    if chain:
        out_spec = pl.BlockSpec((None, rows, None, cpt, ct), lambda b, k, c: (b, 0, cgi(c), 0, k))
        out_shape = jax.ShapeDtypeStruct(v5.shape, F32)
    else:
        out_spec = pl.BlockSpec((None, None, cpt, ct), lambda b, k, c: (b, c, 0, k))
        out_shape = jax.ShapeDtypeStruct((bsz, ncg, cpt, w), F32)
    kern = functools.partial(_lru_kernel, rows=rows, rev=rev, chain=chain)
    return pl.pallas_call(
        kern,
        grid=(bsz, nct, ncg),
        in_specs=in_specs,
        out_specs=out_spec,
        out_shape=out_shape,
        scratch_shapes=[pltpu.VMEM((rows + CONV_W - 1, cpt, ct), F32),
                        pltpu.VMEM((rows * cpt, ct), F32),
                        pltpu.VMEM((rows * cpt, ct), F32),
                        pltpu.VMEM((1, ct), F32)],
        compiler_params=_params(("arbitrary", "arbitrary", "arbitrary")),
        name="lru_rev" if rev else "lru_fwd",
    )(v5, conv_w, conv_b.reshape(1, w), wr, b_r.reshape(1, w), wi, b_i.reshape(1, w),
      lam.reshape(1, w), h0)


def _merge_kernel(x_ref, ys_ref, ylf_ref, ylb_ref, zg_ref, gate_ref, sh_ref, sc_ref, g2_ref,
                  wglu_ref, wa_ref, wb_ref, wo_ref, h1_ref, ft_ref, *, w_lru, d):
    za = jax.nn.gelu(ys_ref[...])
    ya = (za * jax.nn.sigmoid(jnp.dot(za.astype(BF16), wglu_ref[...],
                                      preferred_element_type=F32))).astype(BF16)
    zg = zg_ref[...]
    yb = ((ylf_ref[...] + ylb_ref[...]) * jax.nn.gelu(zg[:, :w_lru].astype(F32))).astype(BF16)
    ga = jax.nn.sigmoid(zg[:, w_lru:w_lru + d].astype(F32))
    gb = jax.nn.sigmoid(zg[:, w_lru + d:].astype(F32))
    m = (ga * jnp.dot(ya, wa_ref[...], preferred_element_type=F32)
         + gb * jnp.dot(yb, wb_ref[...], preferred_element_type=F32))
    o = jnp.dot(m.astype(BF16), wo_ref[...], preferred_element_type=F32)
    h1 = x_ref[...] + gate_ref[...] * o
    h1_ref[...] = h1
    f = _rms(h1, g2_ref[...]) * (1.0 + sc_ref[...]) + sh_ref[...]
    ft_ref[...] = f.T.astype(BF16)


def _merge(x, ys5, ylf, ylb, zg, mod6, g2, w_glu, w_a, w_b, w_o):
    bsz, length, d = x.shape
    w_s5 = ys5.shape[-1]
    w_lru = ylf.shape[-1]
    tm = 256
    nt = length // tm
    row = lambda k: (lambda b, i: (b * 6 + k, 0, 0))
    tok = lambda w: pl.BlockSpec((None, tm, w), lambda b, i: (b, i, 0))
    const = lambda shape: pl.BlockSpec(shape, lambda b, i: (0, 0), pipeline_mode=pl.Buffered(1))
    kern = functools.partial(_merge_kernel, w_lru=w_lru, d=d)
    return pl.pallas_call(
        kern,
        grid=(bsz, nt),
        in_specs=[tok(d), tok(w_s5), tok(w_lru), tok(w_lru), tok(zg.shape[-1]),
                  pl.BlockSpec((None, 1, d), row(2)), pl.BlockSpec((None, 1, d), row(3)),
                  pl.BlockSpec((None, 1, d), row(4)),
                  pl.BlockSpec((1, d), lambda b, i: (0, 0)),
                  const(w_glu.shape), const(w_a.shape), const(w_b.shape), const(w_o.shape)],
        out_specs=[tok(d), pl.BlockSpec((d, tm), lambda b, i: (0, b * nt + i))],
        out_shape=[jax.ShapeDtypeStruct((bsz, length, d), F32),
                   jax.ShapeDtypeStruct((d, bsz * length), BF16)],
        compiler_params=_params(("arbitrary", "arbitrary")),
        name="merge",
    )(x, ys5, ylf, ylb, zg, mod6, mod6, mod6, g2.reshape(1, d), w_glu, w_a, w_b, w_o)


def _top_rows(x, k):
    out = []
    for _ in range(k):
        m = jnp.max(x, axis=0, keepdims=True)
        out.append(m)
        x = jnp.where(x == m, -jnp.inf, x)
    return out


def _sort_pairs(n):
    out = []
    p = 1
    while p < n:
        k = p
        while k >= 1:
            for j in range(k % p, n - k, 2 * k):
                for i in range(min(k, n - j - k)):
                    if (i + j) // (2 * p) == (i + j + k) // (2 * p):
                        out.append((i + j, i + j + k))
            k //= 2
        p *= 2
    return out


def _top_rows_sorted(x, k):
    n_slab = x.shape[0] // 8
    slabs = [x[8 * r:8 * r + 8] for r in range(n_slab)]
    for a, b in _sort_pairs(n_slab):
        slabs[a], slabs[b] = jnp.maximum(slabs[a], slabs[b]), jnp.minimum(slabs[a], slabs[b])
    slabs.append(jnp.full_like(slabs[0], -jnp.inf))
    out = []
    for it in range(k):
        m = jnp.max(slabs[0], axis=0, keepdims=True)
        out.append(m)
        hit = slabs[0] == m
        for r in range(min(k - 1 - it, n_slab)):
            slabs[r] = jnp.where(hit, slabs[r + 1], slabs[r])
    return out


def _count_above(rows, x, strict):
    assert PEER_TOPK == 16
    above = (lambda a: a > x) if strict else (lambda a: a >= x)
    sel = jnp.where
    b3 = above(rows[7])
    b2 = above(sel(b3, rows[11], rows[3]))
    b1 = above(sel(b3, sel(b2, rows[13], rows[9]), sel(b2, rows[5], rows[1])))
    b0 = above(sel(b3, sel(b2, sel(b1, rows[14], rows[12]), sel(b1, rows[10], rows[8])),
                   sel(b2, sel(b1, rows[6], rows[4]), sel(b1, rows[2], rows[0]))))
    return (sel(b3, 8.0, 0.0) + sel(b2, 4.0, 0.0) + sel(b1, 2.0, 0.0) + sel(b0, 1.0, 0.0)
            + sel(above(rows[15]), 1.0, 0.0))


_CAND_PAIRS = [(j, k) for j in range(PEER_TOPK + 1) for k in range(PEER_TOPK + 1)
               if (j + 1) * (k + 1) <= PEER_TOPK + 1]
_CAND_ROWS = -(-len(_CAND_PAIRS) // 8) * 8


def _route_kernel(ft_ref, wq_ref, sk_ref, r2_ref, e2_ref, c1_ref, cand_scr):
    half = sk_ref.shape[2]
    tm = ft_ref.shape[1]
    qt = jnp.dot(wq_ref[...], ft_ref[...], preferred_element_type=F32).astype(BF16)
    for r in range(len(_CAND_PAIRS), _CAND_ROWS):
        cand_scr[r:r + 1, :] = jnp.full((1, 128), -jnp.inf, F32)
    for h in range(PEER_HEADS):
        s_full = [jnp.dot(sk_ref[side], qt[(2 * h + side) * half:(2 * h + side + 1) * half],
                          preferred_element_type=F32) for side in range(2)]
        for l in range(tm // 128):
            lanes = slice(l * 128, (l + 1) * 128)
            s = [sf[:, lanes] for sf in s_full]
            top0 = _top_rows_sorted(s[0], PEER_TOPK + 1)
            top1 = _top_rows_sorted(s[1], PEER_TOPK + 1)
            for r, (j, k) in enumerate(_CAND_PAIRS):
                cand_scr[r:r + 1, :] = top0[j] + top1[k]
            best = _top_rows(cand_scr[...], PEER_TOPK + 1)
            z = jnp.ones_like(best[0])
            for k in range(1, PEER_TOPK):
                z = z + jnp.exp(best[k] - best[0])
            t1 = 0.5 * (best[PEER_TOPK - 1] + best[PEER_TOPK]) - s[0]
            n1 = _count_above(top1, t1, strict=False)
            rank2 = _count_above(top1, s[1], strict=True)
            r2_ref[h, :, lanes] = rank2.astype(BF16)
            e2_ref[h, :, lanes] = jnp.exp(s[1] - top1[0]).astype(BF16)
            c1_ref[h, :, lanes] = n1 + 0.5 * (jnp.exp(s[0] - top0[0]) / z)


def _route(ft, wq_t, sk):
    d, nt = ft.shape
    nk = sk.shape[1]
    tm = 256
    ospec = pl.BlockSpec((PEER_HEADS, nk, tm), lambda i: (0, 0, i))
    return pl.pallas_call(
        _route_kernel,
        grid=(nt // tm,),
        in_specs=[pl.BlockSpec((d, tm), lambda i: (0, i)),
                  pl.BlockSpec(wq_t.shape, lambda i: (0, 0), pipeline_mode=pl.Buffered(1)),
                  pl.BlockSpec(sk.shape, lambda i: (0, 0, 0))],
        out_specs=[ospec] * 3,
        out_shape=[jax.ShapeDtypeStruct((PEER_HEADS, nk, nt), dt) for dt in (BF16, BF16, F32)],
        scratch_shapes=[pltpu.VMEM((_CAND_ROWS, 128), F32)],
        compiler_params=_params(("arbitrary",)),
        name="route",
    )(ft, wq_t, sk)


def _peer_kernel(ft_ref, u_ref, v_ref, r2_ref, e2_ref, c1_ref, h1_ref, gate_ref, gf_ref,
                 o_ref, acc_ref, *, nk):
    j = pl.program_id(1)

    @pl.when(j == 0)
    def _():
        acc_ref[...] = jnp.zeros_like(acc_ref)

    tm = ft_ref.shape[1]
    n_i1 = u_ref.shape[0] // nk
    per = MXU_TILE // nk
    ps = []
    for c in range(n_i1 // per):
        st = jnp.dot(u_ref[c * MXU_TILE:(c + 1) * MXU_TILE, :], ft_ref[...],
                     preferred_element_type=F32)
        for a in range(per):
            i1 = j * n_i1 + c * per + a
            act = jax.nn.gelu(st[a * nk:(a + 1) * nk]).astype(BF16)
            w = jnp.zeros((nk, tm), BF16)
            for h in range(PEER_HEADS):
                packed = c1_ref[h, pl.ds(i1, 1), :]
                cnt = jnp.floor(packed)
                cf = (2.0 * (packed - cnt)).astype(BF16)
                w = w + jnp.where(r2_ref[h] < cnt.astype(BF16), e2_ref[h] * cf, jnp.zeros((), BF16))
            ps.append(w * act)
    p = jnp.concatenate(ps, axis=0)
    acc_ref[...] += lax.dot_general(p, v_ref[...], (((0,), (0,)), ((), ())), preferred_element_type=F32)

    @pl.when(j == pl.num_programs(1) - 1)
    def _():
        gate, gf = gate_ref[...], gf_ref[...]

        def rows(i, _):
            r = pl.multiple_of(i * ROW_SLAB, ROW_SLAB)
            h2 = h1_ref[pl.ds(r, ROW_SLAB), :] + gate * acc_ref[pl.ds(r, ROW_SLAB), :]
            o_ref[pl.ds(r, ROW_SLAB), :] = _rms(h2, gf)
            return 0

        lax.fori_loop(0, tm // ROW_SLAB, rows, 0)


def _peer(ft, u, v, r2, e2, c1, h1, mod6, g_final, tiles_per_batch_fn):
    d, nt = ft.shape
    ne = u.shape[0]
    nk = r2.shape[1]
    tm = min(1024, nt)
    et = 512
    rspec = pl.BlockSpec((PEER_HEADS, nk, tm), lambda i, j: (0, 0, i), pipeline_mode=pl.Buffered(1))
    kern = functools.partial(_peer_kernel, nk=nk)
    return pl.pallas_call(
        kern,
        grid=(nt // tm, ne // et),
        in_specs=[pl.BlockSpec((d, tm), lambda i, j: (0, i)),
                  pl.BlockSpec((et, d), lambda i, j: (j, 0)),
                  pl.BlockSpec((et, d), lambda i, j: (j, 0)),
                  rspec, rspec, rspec,
                  pl.BlockSpec((tm, d), lambda i, j: (i, 0), pipeline_mode=pl.Buffered(1)),
                  pl.BlockSpec((None, 1, d), lambda i, j: (tiles_per_batch_fn(i, tm) * 6 + 5, 0, 0)),
                  pl.BlockSpec((1, d), lambda i, j: (0, 0))],
        out_specs=pl.BlockSpec((tm, d), lambda i, j: (i, 0), pipeline_mode=pl.Buffered(1)),
        out_shape=jax.ShapeDtypeStruct((nt, d), F32),
        scratch_shapes=[pltpu.VMEM((tm, d), F32)],
        compiler_params=_params(("arbitrary", "arbitrary")),
        name="peer",
    )(ft, u, v, r2, e2, c1, h1, mod6, g_final.reshape(1, d))


def kernel(x, c, ctx, c_ctx, w_ada, b_ada, g_norm1, w_in, s5_a_re, s5_a_im, s5_log_dt, s5_b_re, s5_b_im, s5_c_re, s5_c_im, s5_d, s5_w_glu, lru_conv_w, lru_conv_b, lru_lambda, lru_w_r, lru_b_r, lru_w_i, lru_b_i, w_proj_a, w_proj_b, w_out, g_norm2, peer_w_query, peer_sub_keys, peer_u, peer_v, g_final):
    bsz, length, d = x.shape
    assert w_ada.shape[0] == 1, "single-layer kernel"
    rows = length // GRID_W
    w_s5 = s5_d.shape[-1]
    w_lru = lru_lambda.shape[-1]
    assert bsz < 8 and length % (GRID_W * LRU_ROW_BLOCK) == 0 and ctx.shape[1] % LRU_ROW_BLOCK == 0

    cin = jnp.zeros((8, d), F32).at[:bsz].set(c).at[bsz].set(c_ctx)
    mod6 = _ada(cin, w_ada[0], b_ada[0]).reshape(8 * 6, 1, d)

    w_in_b = w_in[0].astype(BF16)
    zu, zv, zg = _inproj(x, g_norm1[0], mod6, lambda b: b, w_in_b, w_s5, w_lru)
    zu_c, zv_c = _inproj(ctx, g_norm1[0], mod6, lambda b: bsz, w_in_b[:, :w_s5 + w_lru], w_s5, w_lru)

    mt, bp, cp, a16 = _s5_operators(s5_a_re[0], s5_a_im[0], s5_log_dt[0], s5_b_re[0], s5_b_im[0],
                                    s5_c_re[0], s5_c_im[0], s5_d[0])
    ys5 = _s5(zu, zu_c, mt, bp, cp, a16)

    lctx = ctx.shape[1]
    vc5 = jnp.zeros((1, lctx, 1, COLS_PER_TILE, w_lru), F32).at[0, :, 0, :bsz].set(zv_c.transpose(1, 0, 2))
    v5 = zv.reshape(bsz, rows, GRID_W // COLS_PER_TILE, COLS_PER_TILE, w_lru)
    zero_h = jnp.zeros((1, 1, w_lru), F32)
    yl = []
    for dr in range(2):
        args = (lru_conv_w[0], lru_conv_b[0], lru_lambda[0, dr], lru_w_r[0, dr], lru_b_r[0, dr],
                lru_w_i[0, dr], lru_b_i[0, dr])
        hc = _lru(vc5, *args, zero_h, rev=dr == 1, chain=False)
        h0 = hc[0, 0, :bsz].reshape(bsz, 1, w_lru)
        yl.append(_lru(v5, *args, h0, rev=dr == 1, chain=True).reshape(bsz, length, w_lru))

    h1, ft = _merge(x, ys5, yl[0], yl[1], zg, mod6, g_norm2[0], s5_w_glu[0].astype(BF16),
                    w_proj_a[0].astype(BF16), w_proj_b[0].astype(BF16), w_out[0].astype(BF16))

    r2, e2, c1 = _route(ft, peer_w_query[0].T.astype(BF16), peer_sub_keys[0].astype(BF16))
    out = _peer(ft, peer_u[0].astype(BF16), peer_v[0].astype(BF16), r2, e2, c1,
                h1.reshape(bsz * length, d), mod6, g_final,
                lambda i, tm: i // (length // tm))
    return out.reshape(bsz, length, d)
```

```python
import functools

import jax
import jax.numpy as jnp
from jax import lax
from jax.experimental import pallas as pl
from jax.experimental.pallas import tpu as pltpu

F32 = jnp.float32
BF16 = jnp.bfloat16
EPS = 1e-6
GRID_W = 64
S5_GROUP = 16
S5_STATE = 64
S5_CHUNK = 16
LRU_HEADS = 16
LRU_C = 8.0
CONV_W = 4
CONV_PAD_LO = (CONV_W - 1) // 2
N_KEYS = 128
PEER_HEADS = 8
PEER_TOPK = 16
COLS_PER_TILE = 8
MXU_TILE = 256
NORM_ROWS = 128
VMEM_LIMIT = 56 * 1024 * 1024


def _params(sem, vmem=VMEM_LIMIT):
    return pltpu.CompilerParams(dimension_semantics=sem, vmem_limit_bytes=vmem)


def _rms(x, g):
    return x * lax.rsqrt(jnp.mean(x * x, axis=-1, keepdims=True) + EPS) * g


def _ada_kernel(c_ref, w_ref, b_ref, o_ref):
    c = c_ref[...]
    sc = c * jax.nn.sigmoid(c)
    o_ref[...] = jnp.dot(sc.astype(BF16), w_ref[...].astype(BF16),
                         preferred_element_type=F32) + b_ref[...]


def _ada(cin, w, b):
    d, n = w.shape
    tn = 1536
    return pl.pallas_call(
        _ada_kernel,
        grid=(n // tn,),
        in_specs=[pl.BlockSpec((8, d), lambda j: (0, 0)),
                  pl.BlockSpec((d, tn), lambda j: (0, j)),
                  pl.BlockSpec((1, tn), lambda j: (0, j))],
        out_specs=pl.BlockSpec((8, tn), lambda j: (0, j)),
        out_shape=jax.ShapeDtypeStruct((8, n), F32),
        compiler_params=_params(("arbitrary",)),
        name="ada",
    )(cin, w, b.reshape(1, n))


def _inproj_kernel(x_ref, g_ref, sh_ref, sc_ref, w_ref, *refs, n_u, n_v):
    if len(refs) == 4:
        zu_ref, zv_ref, zg_ref, n_scr = refs
    else:
        zu_ref, zv_ref, n_scr = refs
        zg_ref = None
    j = pl.program_id(2)

    @pl.when(j == 0)
    def _():
        g, sh, scale1 = g_ref[...], sh_ref[...], 1.0 + sc_ref[...]
        for r in range(0, x_ref.shape[0], NORM_ROWS):
            rows = slice(r, r + NORM_ROWS)
            n_scr[rows, :] = (_rms(x_ref[rows, :], g) * scale1 + sh).astype(BF16)

    z = jnp.dot(n_scr[...], w_ref[...], preferred_element_type=F32)

    @pl.when(j < n_u)
    def _():
        zu_ref[...] = z

    @pl.when((j >= n_u) & (j < n_u + n_v))
    def _():
        zv_ref[...] = z

    if zg_ref is not None:
        @pl.when(j >= n_u + n_v)
        def _():
            zg_ref[...] = z.astype(zg_ref.dtype)


def _inproj(x, g, mod6, mod_row, w_bf16, width_u, width_v):
    bsz, length, d = x.shape
    n = w_bf16.shape[1]
    tm = min(1024, length)
    tn = 512
    n_u, n_v = width_u // tn, width_v // tn
    n_g = n // tn - n_u - n_v
    kern = functools.partial(_inproj_kernel, n_u=n_u, n_v=n_v)
    row = lambda k: (lambda b, i, j: (mod_row(b) * 6 + k, 0, 0))
    out_specs = [pl.BlockSpec((None, tm, tn), lambda b, i, j: (b, i, jnp.minimum(j, n_u - 1))),
                 pl.BlockSpec((None, tm, tn), lambda b, i, j: (b, i, jnp.clip(j - n_u, 0, n_v - 1)))]
    out_shape = [jax.ShapeDtypeStruct((bsz, length, width_u), F32),
                 jax.ShapeDtypeStruct((bsz, length, width_v), F32)]
    if n_g:
        out_specs.append(pl.BlockSpec((None, tm, tn), lambda b, i, j: (b, i, jnp.maximum(j - n_u - n_v, 0))))
        out_shape.append(jax.ShapeDtypeStruct((bsz, length, n_g * tn), BF16))
    return pl.pallas_call(
        kern,
        grid=(bsz, length // tm, n // tn),
        in_specs=[pl.BlockSpec((None, tm, d), lambda b, i, j: (b, i, 0)),
                  pl.BlockSpec((1, d), lambda b, i, j: (0, 0)),
                  pl.BlockSpec((None, 1, d), row(0)),
                  pl.BlockSpec((None, 1, d), row(1)),
                  pl.BlockSpec((d, tn), lambda b, i, j: (0, j))],
        out_specs=out_specs,
        out_shape=out_shape,
        scratch_shapes=[pltpu.VMEM((tm, d), BF16)],
        compiler_params=_params(("arbitrary", "arbitrary", "arbitrary")),
        name="inproj",
    )(x, g.reshape(1, d), mod6, mod6, w_bf16)


def _s5_prep_kernel(are_ref, aim_ref, ldt_ref, bre_ref, bim_ref, cre_ref, cim_ref,
                    k_ref, cg_ref, bg_ref, a16_ref):
    nk = S5_CHUNK + 1
    dt = jnp.exp(ldt_ref[...])
    lre, lim = are_ref[...], aim_ref[...]
    xr, xi = lre * dt, lim * dt
    e1 = jnp.exp(xr)
    ar, ai = e1 * jnp.cos(xi), e1 * jnp.sin(xi)
    pows = [(jnp.ones_like(ar), jnp.zeros_like(ai))]
    for _ in range(nk - 1):
        kr, ki = pows[-1]
        pows.append((kr * ar - ki * ai, kr * ai + ki * ar))
    rep = lambda rows: jnp.concatenate([jnp.broadcast_to(r, (S5_GROUP, r.shape[1])) for r in rows], axis=0)
    pr, pim = rep([q[0] for q in pows]), rep([q[1] for q in pows])
    den = lre * lre + lim * lim
    qr = ((ar - 1.0) * lre + ai * lim) / den
    qi = (ai * lre - (ar - 1.0) * lim) / den
    bre, bim = bre_ref[...], bim_ref[...]
    bbr, bbi = qr * bre - qi * bim, qr * bim + qi * bre
    tile = lambda m: jnp.concatenate([m] * nk, axis=0)
    cr_t, ci_t, br_t, bi_t = tile(cre_ref[...]), tile(cim_ref[...]), tile(bbr), tile(bbi)
    cg = jnp.concatenate([cr_t * pr - ci_t * pim, -(cr_t * pim + ci_t * pr)], axis=1)
    bg = jnp.concatenate([br_t * pr - bi_t * pim, br_t * pim + bi_t * pr], axis=1)
    bcat = jnp.concatenate([bbr, bbi], axis=1)
    cg_ref[...] = cg
    bg_ref[...] = bg
    k_ref[...] = lax.dot_general(cg[:S5_CHUNK * S5_GROUP], bcat, (((1,), (1,)), ((), ())),
                                 precision=lax.Precision.HIGHEST, preferred_element_type=F32)
    n = S5_CHUNK * S5_GROUP
    a16_ref[...] = jnp.concatenate([pr[n:n + 1], pim[n:n + 1]], axis=1)


def _s5_operators(a_re, a_im, log_dt, b_re, b_im, c_re, c_im, d_skip):
    _, groups, p = a_re.shape
    h = S5_GROUP
    dg = 2 * groups
    nk = S5_CHUNK + 1
    vec = lambda t: t.reshape(dg, 1, p)
    tr = lambda t: jnp.swapaxes(t, -1, -2).reshape(dg, h, p)
    ldt = jnp.broadcast_to(log_dt[..., None], (2, groups, p))
    spec_v = pl.BlockSpec((None, 1, p), lambda i: (i, 0, 0))
    spec_m = pl.BlockSpec((None, h, p), lambda i: (i, 0, 0))
    kfl, cg, bg, a16 = pl.pallas_call(
        _s5_prep_kernel,
        grid=(dg,),
        in_specs=[spec_v, spec_v, spec_v, spec_m, spec_m, spec_m, spec_m],
        out_specs=[pl.BlockSpec((None, S5_CHUNK * h, h), lambda i: (i, 0, 0)),
                   pl.BlockSpec((None, nk * h, 2 * p), lambda i: (i, 0, 0)),
                   pl.BlockSpec((None, nk * h, 2 * p), lambda i: (i, 0, 0)),
                   pl.BlockSpec((None, 1, 2 * p), lambda i: (i, 0, 0))],
        out_shape=[jax.ShapeDtypeStruct((dg, S5_CHUNK * h, h), F32),
                   jax.ShapeDtypeStruct((dg, nk * h, 2 * p), F32),
                   jax.ShapeDtypeStruct((dg, nk * h, 2 * p), F32),
                   jax.ShapeDtypeStruct((dg, 1, 2 * p), F32)],
        compiler_params=_params(("arbitrary",)),
        name="s5_prep",
    )(vec(a_re), vec(a_im), vec(ldt), tr(b_re), tr(b_im),
      c_re.reshape(dg, h, p), c_im.reshape(dg, h, p))

    t = S5_CHUNK
    kfl = kfl.reshape(2, groups, t, h, h)
    cg = cg.reshape(2, groups, nk, h, 2 * p)
    bg = bg.reshape(2, groups, nk, h, 2 * p)
    ti = jnp.arange(t)
    lag = ti[:, None] - ti[None, :]
    kf = jnp.where((lag >= 0)[None, :, :, None, None], kfl[0][:, jnp.clip(lag, 0, t - 1)], 0.0)
    kb = jnp.where((lag <= 0)[None, :, :, None, None], kfl[1][:, jnp.clip(-lag, 0, t - 1)], 0.0)
    eye = (lag == 0)[None, :, :, None, None] * jnp.eye(h, dtype=F32)[None, None, None]
    m = kf + kb + eye * d_skip.reshape(groups, 1, 1, 1, h)
    mt = m.transpose(0, 2, 4, 1, 3).reshape(groups, t * h, t * h)
    bp_f = bg[0][:, t - 1 - ti].reshape(groups, t * h, 2 * p)
    bp_b = bg[1][:, ti].reshape(groups, t * h, 2 * p)
    bp = jnp.concatenate([bp_f, bp_b], axis=-1)
    cp_f = cg[0][:, ti + 1].reshape(groups, t * h, 2 * p)
    cp_b = cg[1][:, t - ti].reshape(groups, t * h, 2 * p)
    cp = jnp.concatenate([cp_f, cp_b], axis=-1).swapaxes(1, 2)
    return mt.astype(BF16), bp.astype(BF16), cp.astype(BF16), a16.reshape(2, groups, 1, 2 * p)


def _cplx_coef(a):
    p = a.shape[1] // 2
    lane = lax.broadcasted_iota(jnp.int32, a.shape, 1)
    sw = pltpu.roll(a, p, axis=1)
    return jnp.where(lane < p, a, sw), jnp.where(lane < p, -sw, a)


def _cplx_mul(a, x):
    c1, c2 = _cplx_coef(a)
    return c1 * x + c2 * pltpu.roll(x, x.shape[1] // 2, axis=1)


def _chunk_scan(x, a, reverse):
    n = x.shape[0]
    row = lax.broadcasted_iota(jnp.int32, (n, 1), 0)
    o = 1
    while o < n:
        if reverse:
            sh = jnp.where(row < n - o, pltpu.roll(x, n - o, axis=0), 0.0)
        else:
            sh = jnp.where(row >= o, pltpu.roll(x, o, axis=0), 0.0)
        x = x + _cplx_mul(a, sh)
        a = _cplx_mul(a, a)
        o *= 2
    return x


S5_GPB = 128 // S5_GROUP
S5_TPT = 128 // S5_GROUP


def _chunk_perm():
    n = S5_TPT * S5_GPB * S5_GROUP
    i = jnp.arange(n)
    t, g, h = i // (S5_GPB * S5_GROUP), (i // S5_GROUP) % S5_GPB, i % S5_GROUP
    dst = g * (S5_TPT * S5_GROUP) + t * S5_GROUP + h
    return jnp.zeros((n, n), BF16).at[i, dst].set(1.0)


def _load_chunks(src_ref, perm, dst_scr):
    nc = src_ref.shape[0] // S5_CHUNK
    halves = []
    for r in range(S5_CHUNK // S5_TPT):
        xcat = jnp.concatenate(
            [src_ref[pl.ds(r * S5_TPT + t, nc, stride=S5_CHUNK), :].astype(BF16) for t in range(S5_TPT)], axis=1)
        halves.append(jnp.dot(xcat, perm, preferred_element_type=F32).astype(BF16))
    for g in range(S5_GPB):
        dst_scr[g] = jnp.concatenate([hv[:, g * 128:(g + 1) * 128] for hv in halves], axis=1)


def _s5_kernel(u_ref, uc_ref, perm_ref, permt_ref, mt_ref, bp_ref, cp_ref, af_ref, ab_ref, y_ref,
               u_scr, uc_scr, y_scr):
    nc = u_ref.shape[0] // S5_CHUNK
    _load_chunks(u_ref, perm_ref[...], u_scr)
    _load_chunks(uc_ref, perm_ref[...], uc_scr)

    def group(g, _):
        y_scr[g] = _s5_group(u_scr[g], uc_scr[g], mt_ref[g], bp_ref[g], cp_ref[g], af_ref[g], ab_ref[g])
        return 0

    lax.fori_loop(0, S5_GPB, group, 0)

    for r in range(S5_CHUNK // S5_TPT):
        ycat = jnp.concatenate([y_scr[g][:, r * 128:(r + 1) * 128] for g in range(S5_GPB)], axis=1)
        back = jnp.dot(ycat, permt_ref[...], preferred_element_type=F32)
        for t in range(S5_TPT):
            y_ref[pl.ds(r * S5_TPT + t, nc, stride=S5_CHUNK), :] = back[:, t * 128:(t + 1) * 128]


def _s5_group(u, uc, mt, bp, cp, af, ab):
    nc = u.shape[0]
    ps = af.shape[1]
    v = jnp.dot(u, bp, preferred_element_type=F32)
    vc = jnp.dot(uc, bp, preferred_element_type=F32)
    ncc = vc.shape[0]
    hcf = _chunk_scan(vc[:, :ps], af, False)[ncc - 1:ncc]
    hcb = _chunk_scan(vc[:, ps:], ab, True)[0:1]
    row = lax.broadcasted_iota(jnp.int32, (nc, 1), 0)
    vf = v[:, :ps] + jnp.where(row == 0, _cplx_mul(af, hcf), 0.0)
    vb = v[:, ps:] + jnp.where(row == nc - 1, _cplx_mul(ab, hcb), 0.0)
    sf = _chunk_scan(vf, af, False)
    sb = _chunk_scan(vb, ab, True)
    sin_f = jnp.where(row == 0, hcf, pltpu.roll(sf, 1, axis=0))
    sin_b = jnp.where(row == nc - 1, hcb, pltpu.roll(sb, nc - 1, axis=0))
    s_in = jnp.concatenate([sin_f, sin_b], axis=1).astype(BF16)
    y = (jnp.dot(u, mt, preferred_element_type=F32)
         + jnp.dot(s_in, cp, preferred_element_type=F32))
    return y.astype(BF16)


def _s5(zu, zu_c, mt, bp, cp, a16):
    bsz, length, w = zu.shape
    lctx = zu_c.shape[1]
    nc, ncc = length // S5_CHUNK, lctx // S5_CHUNK
    cw = S5_CHUNK * S5_GROUP
    ps = a16.shape[-1]
    perm = _chunk_perm()
    once = lambda shape, imap: pl.BlockSpec(shape, imap, pipeline_mode=pl.Buffered(1))
    wspec = pl.BlockSpec((S5_GPB, cw, cw), lambda b, q: (q, 0, 0))
    return pl.pallas_call(
        _s5_kernel,
        grid=(bsz, w // 128),
        in_specs=[once((None, length, 128), lambda b, q: (b, 0, q)),
                  pl.BlockSpec((None, lctx, 128), lambda b, q: (b, 0, q)),
                  once(perm.shape, lambda b, q: (0, 0)),
                  once(perm.shape, lambda b, q: (0, 0)),
                  wspec, wspec, wspec,
                  pl.BlockSpec((None, S5_GPB, 1, ps), lambda b, q: (0, q, 0, 0)),
                  pl.BlockSpec((None, S5_GPB, 1, ps), lambda b, q: (1, q, 0, 0))],
        out_specs=once((None, length, 128), lambda b, q: (b, 0, q)),
        out_shape=jax.ShapeDtypeStruct((bsz, length, w), F32),
        scratch_shapes=[pltpu.VMEM((S5_GPB, nc, cw), BF16),
                        pltpu.VMEM((S5_GPB, ncc, cw), BF16),
                        pltpu.VMEM((S5_GPB, nc, cw), BF16)],
        compiler_params=_params(("arbitrary", "arbitrary")),
        name="s5",
    )(zu, zu_c, perm, perm.T, mt, bp, cp, a16, a16)


LRU_ROW_BLOCK = 16


def _lru_kernel(v_ref, cw_ref, cb_ref, wr_ref, br_ref, wi_ref, bi_ref, lam_ref, h0_ref, o_ref,
                vpad, a_scr, b_scr, carry_scr, *, rows, rev, chain):
    ct = v_ref.shape[-1]
    cpt = COLS_PER_TILE
    rb = LRU_ROW_BLOCK
    nblk = rows // rb
    nslab = ct // MXU_TILE

    if chain:
        @pl.when(pl.program_id(2) == 0)
        def _():
            carry_scr[...] = h0_ref[...]

    zero_row = jnp.zeros((cpt, ct), F32)
    for k in range(CONV_PAD_LO):
        vpad[k] = zero_row
    for k in range(CONV_W - 1 - CONV_PAD_LO):
        vpad[CONV_PAD_LO + rows + k] = zero_row

    def copy(i, _):
        r0 = pl.multiple_of(i * rb, rb)
        vpad[pl.ds(r0 + CONV_PAD_LO, rb)] = v_ref[pl.ds(r0, rb)]
        return 0

    lax.fori_loop(0, nblk, copy, 0)

    cw = cw_ref[...]
    cb = cb_ref[...]
    nl = -lam_ref[...]
    softplus = jnp.maximum(nl, 0.0) + jnp.log1p(jnp.exp(-jnp.abs(nl)))
    c8 = -LRU_C * softplus
    b_r, b_i = br_ref[...], bi_ref[...]

    def block(i, carry):
        acc_a, acc_h = carry
        bi_ = (nblk - 1 - i) if rev else i
        r0 = pl.multiple_of(bi_ * rb, rb)
        x = cb
        for k in range(CONV_W):
            x = x + cw[k:k + 1] * vpad[pl.ds(r0 + k, rb)].reshape(rb * cpt, ct)
        xb = x.astype(BF16)
        pre_r = jnp.concatenate(
            [jnp.dot(xb[:, s * MXU_TILE:(s + 1) * MXU_TILE], wr_ref[s], preferred_element_type=F32)
             for s in range(nslab)], axis=1)
        pre_i = jnp.concatenate(
            [jnp.dot(xb[:, s * MXU_TILE:(s + 1) * MXU_TILE], wi_ref[s], preferred_element_type=F32)
             for s in range(nslab)], axis=1)
        r = jax.nn.sigmoid(pre_r + b_r)
        ig = jax.nn.sigmoid(pre_i + b_i)
        log_a = c8 * r
        a = jnp.exp(log_a)
        bx = jnp.sqrt(1.0 - a * a) * (ig * x)
        q0 = pl.multiple_of(r0 * cpt, rb * cpt)
        a_scr[pl.ds(q0, rb * cpt), :] = a
        b_scr[pl.ds(q0, rb * cpt), :] = bx
        order = range(rb - 1, -1, -1) if rev else range(rb)
        for j in order:
            aj = a[j * cpt:(j + 1) * cpt]
            acc_h = aj * acc_h + bx[j * cpt:(j + 1) * cpt]
            acc_a = acc_a * aj
        return acc_a, acc_h

    acc_a, acc_h = lax.fori_loop(0, nblk, block, (jnp.ones((cpt, ct), F32), jnp.zeros((cpt, ct), F32)))

    if not chain:
        o_ref[...] = acc_h
        return

    sub = lax.broadcasted_iota(jnp.int32, (cpt, ct), 0)
    carry = carry_scr[...]
    h_in = jnp.zeros((cpt, ct), F32)
    for s in (range(cpt - 1, -1, -1) if rev else range(cpt)):
        h_in = jnp.where(sub == s, carry, h_in)
        carry = acc_a[s:s + 1] * carry + acc_h[s:s + 1]
    carry_scr[...] = carry

    def row(i, h):
        r = (rows - 1 - i) if rev else i
        q = pl.multiple_of(r * cpt, cpt)
        h = a_scr[pl.ds(q, cpt), :] * h + b_scr[pl.ds(q, cpt), :]
        o_ref[r] = h
        return h

    lax.fori_loop(0, rows, row, h_in, unroll=8)


def _block_diag(w, per):
    hh, n, _ = w.shape
    eye = jnp.eye(per, dtype=w.dtype)
    return jnp.einsum('gpij,pq->gpiqj', w.reshape(hh // per, per, n, n), eye).reshape(hh // per, per * n, per * n)


def _lru(v5, conv_w, conv_b, lam, w_r, b_r, w_i, b_i, h0, *, rev, chain):
    bsz, rows, ncg, cpt, w = v5.shape
    ct = 512
    nct = w // ct
    per = MXU_TILE // (w // LRU_HEADS)
    wr = _block_diag(w_r, per).astype(BF16)
    wi = _block_diag(w_i, per).astype(BF16)
    nslab = ct // MXU_TILE
    cgi = (lambda c: ncg - 1 - c) if rev else (lambda c: c)
    vec = lambda n: pl.BlockSpec((n, ct), lambda b, k, c: (0, k))
    in_specs = [pl.BlockSpec((None, rows, None, cpt, ct), lambda b, k, c: (b, 0, cgi(c), 0, k)),
                vec(CONV_W), vec(1),
                pl.BlockSpec((nslab, MXU_TILE, MXU_TILE), lambda b, k, c: (k, 0, 0)), vec(1),
                pl.BlockSpec((nslab, MXU_TILE, MXU_TILE), lambda b, k, c: (k, 0, 0)), vec(1),
                vec(1),
                pl.BlockSpec((None, 1, ct), lambda b, k, c: (b, 0, k))]
    if chain:
        out_spec = pl.BlockSpec((None, rows, None, cpt, ct), lambda b, k, c: (b, 0, cgi(c), 0, k))
        out_shape = jax.ShapeDtypeStruct(v5.shape, F32)
    else:
        out_spec = pl.BlockSpec((None, None, cpt, ct), lambda b, k, c: (b, c, 0, k))
        out_shape = jax.ShapeDtypeStruct((bsz, ncg, cpt, w), F32)
    kern = functools.partial(_lru_kernel, rows=rows, rev=rev, chain=chain)
    return pl.pallas_call(
        kern,
        grid=(bsz, nct, ncg),
        in_specs=in_specs,
        out_specs=out_spec,
        out_shape=out_shape,
        scratch_shapes=[pltpu.VMEM((rows + CONV_W - 1, cpt, ct), F32),
                        pltpu.VMEM((rows * cpt, ct), F32),
                        pltpu.VMEM((rows * cpt, ct), F32),
                        pltpu.VMEM((1, ct), F32)],
        compiler_params=_params(("arbitrary", "arbitrary", "arbitrary")),
        name="lru_rev" if rev else "lru_fwd",
    )(v5, conv_w, conv_b.reshape(1, w), wr, b_r.reshape(1, w), wi, b_i.reshape(1, w),
      lam.reshape(1, w), h0)


def _merge_kernel(x_ref, ys_ref, ylf_ref, ylb_ref, zg_ref, gate_ref, sh_ref, sc_ref, g2_ref,
                  wglu_ref, wa_ref, wb_ref, wo_ref, h1_ref, ft_ref, *, w_lru, d):
    za = jax.nn.gelu(ys_ref[...])
    ya = (za * jax.nn.sigmoid(jnp.dot(za.astype(BF16), wglu_ref[...],
                                      preferred_element_type=F32))).astype(BF16)
    zg = zg_ref[...]
    yb = ((ylf_ref[...] + ylb_ref[...]) * jax.nn.gelu(zg[:, :w_lru].astype(F32))).astype(BF16)
    ga = jax.nn.sigmoid(zg[:, w_lru:w_lru + d].astype(F32))
    gb = jax.nn.sigmoid(zg[:, w_lru + d:].astype(F32))
    m = (ga * jnp.dot(ya, wa_ref[...], preferred_element_type=F32)
         + gb * jnp.dot(yb, wb_ref[...], preferred_element_type=F32))
    o = jnp.dot(m.astype(BF16), wo_ref[...], preferred_element_type=F32)
    h1 = x_ref[...] + gate_ref[...] * o
    h1_ref[...] = h1
    f = _rms(h1, g2_ref[...]) * (1.0 + sc_ref[...]) + sh_ref[...]
    ft_ref[...] = f.T.astype(BF16)


def _merge(x, ys5, ylf, ylb, zg, mod6, g2, w_glu, w_a, w_b, w_o):
    bsz, length, d = x.shape
    w_s5 = ys5.shape[-1]
    w_lru = ylf.shape[-1]
    tm = 256
    nt = length // tm
    row = lambda k: (lambda b, i: (b * 6 + k, 0, 0))
    tok = lambda w: pl.BlockSpec((None, tm, w), lambda b, i: (b, i, 0))
    const = lambda shape: pl.BlockSpec(shape, lambda b, i: (0, 0), pipeline_mode=pl.Buffered(1))
    kern = functools.partial(_merge_kernel, w_lru=w_lru, d=d)
    return pl.pallas_call(
        kern,
        grid=(bsz, nt),
        in_specs=[tok(d), tok(w_s5), tok(w_lru), tok(w_lru), tok(zg.shape[-1]),
                  pl.BlockSpec((None, 1, d), row(2)), pl.BlockSpec((None, 1, d), row(3)),
                  pl.BlockSpec((None, 1, d), row(4)),
                  pl.BlockSpec((1, d), lambda b, i: (0, 0)),
                  const(w_glu.shape), const(w_a.shape), const(w_b.shape), const(w_o.shape)],
        out_specs=[tok(d), pl.BlockSpec((d, tm), lambda b, i: (0, b * nt + i))],
        out_shape=[jax.ShapeDtypeStruct((bsz, length, d), F32),
                   jax.ShapeDtypeStruct((d, bsz * length), BF16)],
        compiler_params=_params(("arbitrary", "arbitrary")),
        name="merge",
    )(x, ys5, ylf, ylb, zg, mod6, mod6, mod6, g2.reshape(1, d), w_glu, w_a, w_b, w_o)


def _top_rows(x, k):
    out = []
    for _ in range(k):
        m = jnp.max(x, axis=0, keepdims=True)
        out.append(m)
        x = jnp.where(x == m, -jnp.inf, x)
    return out


def _sort_pairs(n):
    out = []
    p = 1
    while p < n:
        k = p
        while k >= 1:
            for j in range(k % p, n - k, 2 * k):
                for i in range(min(k, n - j - k)):
                    if (i + j) // (2 * p) == (i + j + k) // (2 * p):
                        out.append((i + j, i + j + k))
            k //= 2
        p *= 2
    return out


def _top_rows_sorted(x, k):
    n_slab = x.shape[0] // 8
    slabs = [x[8 * r:8 * r + 8] for r in range(n_slab)]
    for a, b in _sort_pairs(n_slab):
        slabs[a], slabs[b] = jnp.maximum(slabs[a], slabs[b]), jnp.minimum(slabs[a], slabs[b])
    slabs.append(jnp.full_like(slabs[0], -jnp.inf))
    out = []
    for it in range(k):
        m = jnp.max(slabs[0], axis=0, keepdims=True)
        out.append(m)
        hit = slabs[0] == m
        for r in range(min(k - 1 - it, n_slab)):
            slabs[r] = jnp.where(hit, slabs[r + 1], slabs[r])
    return out


def _count_above(rows, x, strict):
    assert PEER_TOPK == 16
    above = (lambda a: a > x) if strict else (lambda a: a >= x)
    sel = jnp.where
    b3 = above(rows[7])
    b2 = above(sel(b3, rows[11], rows[3]))
    b1 = above(sel(b3, sel(b2, rows[13], rows[9]), sel(b2, rows[5], rows[1])))
    b0 = above(sel(b3, sel(b2, sel(b1, rows[14], rows[12]), sel(b1, rows[10], rows[8])),
                   sel(b2, sel(b1, rows[6], rows[4]), sel(b1, rows[2], rows[0]))))
    return (sel(b3, 8.0, 0.0) + sel(b2, 4.0, 0.0) + sel(b1, 2.0, 0.0) + sel(b0, 1.0, 0.0)
            + sel(above(rows[15]), 1.0, 0.0))


_CAND_PAIRS = [(j, k) for j in range(PEER_TOPK + 1) for k in range(PEER_TOPK + 1)
               if (j + 1) * (k + 1) <= PEER_TOPK + 1]
_CAND_ROWS = -(-len(_CAND_PAIRS) // 8) * 8


def _route_kernel(ft_ref, wq_ref, sk_ref, r2_ref, e2_ref, n1_ref, e1_ref, cand_scr):
    half = sk_ref.shape[2]
    tm = ft_ref.shape[1]
    qt = jnp.dot(wq_ref[...], ft_ref[...], preferred_element_type=F32).astype(BF16)
    for r in range(len(_CAND_PAIRS), _CAND_ROWS):
        cand_scr[r:r + 1, :] = jnp.full((1, 128), -jnp.inf, F32)
    for h in range(PEER_HEADS):
        s_full = [jnp.dot(sk_ref[side], qt[(2 * h + side) * half:(2 * h + side + 1) * half],
                          preferred_element_type=F32) for side in range(2)]
        for l in range(tm // 128):
            lanes = slice(l * 128, (l + 1) * 128)
            s = [sf[:, lanes] for sf in s_full]
            top0 = _top_rows_sorted(s[0], PEER_TOPK + 1)
            top1 = _top_rows_sorted(s[1], PEER_TOPK + 1)
            for r, (j, k) in enumerate(_CAND_PAIRS):
                cand_scr[r:r + 1, :] = top0[j] + top1[k]
            best = _top_rows(cand_scr[...], PEER_TOPK + 1)
            z = jnp.ones_like(best[0])
            for k in range(1, PEER_TOPK):
                z = z + jnp.exp(best[k] - best[0])
            t1 = 0.5 * (best[PEER_TOPK - 1] + best[PEER_TOPK]) - s[0]
            n1 = _count_above(top1, t1, strict=False)
            rank2 = _count_above(top1, s[1], strict=True)
            r2_ref[h, :, lanes] = rank2.astype(BF16)
            e2_ref[h, :, lanes] = jnp.exp(s[1] - top1[0]).astype(BF16)
            n1_ref[h, :, lanes] = n1
            e1_ref[h, :, lanes] = jnp.exp(s[0] - top0[0]) / z


def _route(ft, wq_t, sk):
    d, nt = ft.shape
    nk = sk.shape[1]
    tm = 256
    ospec = pl.BlockSpec((PEER_HEADS, nk, tm), lambda i: (0, 0, i))
    return pl.pallas_call(
        _route_kernel,
        grid=(nt // tm,),
        in_specs=[pl.BlockSpec((d, tm), lambda i: (0, i)),
                  pl.BlockSpec(wq_t.shape, lambda i: (0, 0), pipeline_mode=pl.Buffered(1)),
                  pl.BlockSpec(sk.shape, lambda i: (0, 0, 0))],
        out_specs=[ospec] * 4,
        out_shape=[jax.ShapeDtypeStruct((PEER_HEADS, nk, nt), dt) for dt in (BF16, BF16, F32, F32)],
        scratch_shapes=[pltpu.VMEM((_CAND_ROWS, 128), F32)],
        compiler_params=_params(("arbitrary",)),
        name="route",
    )(ft, wq_t, sk)


def _peer_kernel(ft_ref, u_ref, v_ref, r2_ref, e2_ref, n1_ref, e1_ref, h1_ref, gate_ref, gf_ref,
                 o_ref, acc_ref, *, nk):
    j = pl.program_id(1)

    @pl.when(j == 0)
    def _():
        acc_ref[...] = jnp.zeros_like(acc_ref)

    tm = ft_ref.shape[1]
    n_i1 = u_ref.shape[0] // nk
    per = MXU_TILE // nk
    ps = []
    for c in range(n_i1 // per):
        st = jnp.dot(u_ref[c * MXU_TILE:(c + 1) * MXU_TILE, :], ft_ref[...],
                     preferred_element_type=F32)
        for a in range(per):
            i1 = j * n_i1 + c * per + a
            act = jax.nn.gelu(st[a * nk:(a + 1) * nk].astype(BF16))
            w = jnp.zeros((nk, tm), BF16)
            for h in range(PEER_HEADS):
                cnt = n1_ref[h, pl.ds(i1, 1), :].astype(BF16)
                cf = e1_ref[h, pl.ds(i1, 1), :].astype(BF16)
                w = w + jnp.where(r2_ref[h] < cnt, e2_ref[h] * cf, jnp.zeros((), BF16))
            ps.append(w * act)
    p = jnp.concatenate(ps, axis=0)
    acc_ref[...] += lax.dot_general(p, v_ref[...], (((0,), (0,)), ((), ())), preferred_element_type=F32)

    @pl.when(j == pl.num_programs(1) - 1)
    def _():
        h2 = h1_ref[...] + gate_ref[...] * acc_ref[...]
        o_ref[...] = _rms(h2, gf_ref[...])


def _peer(ft, u, v, r2, e2, n1, e1, h1, mod6, g_final, tiles_per_batch_fn):
    d, nt = ft.shape
    ne = u.shape[0]
    nk = r2.shape[1]
    tm = 512
    et = 1024
    rspec = pl.BlockSpec((PEER_HEADS, nk, tm), lambda i, j: (0, 0, i), pipeline_mode=pl.Buffered(1))
    kern = functools.partial(_peer_kernel, nk=nk)
    return pl.pallas_call(
        kern,
        grid=(nt // tm, ne // et),
        in_specs=[pl.BlockSpec((d, tm), lambda i, j: (0, i)),
                  pl.BlockSpec((et, d), lambda i, j: (j, 0)),
                  pl.BlockSpec((et, d), lambda i, j: (j, 0)),
                  rspec, rspec, rspec, rspec,
                  pl.BlockSpec((tm, d), lambda i, j: (i, 0), pipeline_mode=pl.Buffered(1)),
                  pl.BlockSpec((None, 1, d), lambda i, j: (tiles_per_batch_fn(i, tm) * 6 + 5, 0, 0)),
                  pl.BlockSpec((1, d), lambda i, j: (0, 0))],
        out_specs=pl.BlockSpec((tm, d), lambda i, j: (i, 0), pipeline_mode=pl.Buffered(1)),
        out_shape=jax.ShapeDtypeStruct((nt, d), F32),
        scratch_shapes=[pltpu.VMEM((tm, d), F32)],
        compiler_params=_params(("arbitrary", "arbitrary")),
        name="peer",
    )(ft, u, v, r2, e2, n1, e1, h1, mod6, g_final.reshape(1, d))


def kernel(x, c, ctx, c_ctx, w_ada, b_ada, g_norm1, w_in, s5_a_re, s5_a_im, s5_log_dt, s5_b_re, s5_b_im, s5_c_re, s5_c_im, s5_d, s5_w_glu, lru_conv_w, lru_conv_b, lru_lambda, lru_w_r, lru_b_r, lru_w_i, lru_b_i, w_proj_a, w_proj_b, w_out, g_norm2, peer_w_query, peer_sub_keys, peer_u, peer_v, g_final):
    bsz, length, d = x.shape
    assert w_ada.shape[0] == 1, "single-layer kernel"
    rows = length // GRID_W
    w_s5 = s5_d.shape[-1]
    w_lru = lru_lambda.shape[-1]
    assert bsz < 8 and length % (GRID_W * LRU_ROW_BLOCK) == 0 and ctx.shape[1] % LRU_ROW_BLOCK == 0

    cin = jnp.zeros((8, d), F32).at[:bsz].set(c).at[bsz].set(c_ctx)
    mod6 = _ada(cin, w_ada[0], b_ada[0]).reshape(8 * 6, 1, d)

    w_in_b = w_in[0].astype(BF16)
    zu, zv, zg = _inproj(x, g_norm1[0], mod6, lambda b: b, w_in_b, w_s5, w_lru)
    zu_c, zv_c = _inproj(ctx, g_norm1[0], mod6, lambda b: bsz, w_in_b[:, :w_s5 + w_lru], w_s5, w_lru)

    mt, bp, cp, a16 = _s5_operators(s5_a_re[0], s5_a_im[0], s5_log_dt[0], s5_b_re[0], s5_b_im[0],
                                    s5_c_re[0], s5_c_im[0], s5_d[0])
    ys5 = _s5(zu, zu_c, mt, bp, cp, a16)

    lctx = ctx.shape[1]
    vc5 = jnp.zeros((1, lctx, 1, COLS_PER_TILE, w_lru), F32).at[0, :, 0, :bsz].set(zv_c.transpose(1, 0, 2))
    v5 = zv.reshape(bsz, rows, GRID_W // COLS_PER_TILE, COLS_PER_TILE, w_lru)
    zero_h = jnp.zeros((1, 1, w_lru), F32)
    yl = []
    for dr in range(2):
        args = (lru_conv_w[0], lru_conv_b[0], lru_lambda[0, dr], lru_w_r[0, dr], lru_b_r[0, dr],
                lru_w_i[0, dr], lru_b_i[0, dr])
        hc = _lru(vc5, *args, zero_h, rev=dr == 1, chain=False)
        h0 = hc[0, 0, :bsz].reshape(bsz, 1, w_lru)
        yl.append(_lru(v5, *args, h0, rev=dr == 1, chain=True).reshape(bsz, length, w_lru))

    h1, ft = _merge(x, ys5, yl[0], yl[1], zg, mod6, g_norm2[0], s5_w_glu[0].astype(BF16),
                    w_proj_a[0].astype(BF16), w_proj_b[0].astype(BF16), w_out[0].astype(BF16))

    r2, e2, n1, e1 = _route(ft, peer_w_query[0].T.astype(BF16), peer_sub_keys[0].astype(BF16))
    out = _peer(ft, peer_u[0].astype(BF16), peer_v[0].astype(BF16), r2, e2, n1, e1,
                h1.reshape(bsz * length, d), mod6, g_final,
                lambda i, tm: i // (length // tm))
    return out.reshape(bsz, length, d)
```

```python
import functools

import jax
import jax.numpy as jnp
from jax import lax
from jax.experimental import pallas as pl
from jax.experimental.pallas import tpu as pltpu

F32 = jnp.float32
BF16 = jnp.bfloat16
EPS = 1e-6
GRID_W = 64
S5_GROUP = 16
S5_STATE = 64
S5_CHUNK = 16
LRU_HEADS = 16
LRU_C = 8.0
CONV_W = 4
CONV_PAD_LO = (CONV_W - 1) // 2
N_KEYS = 128
PEER_HEADS = 8
PEER_TOPK = 16
COLS_PER_TILE = 8
MXU_TILE = 256
NORM_ROWS = 128
VMEM_LIMIT = 56 * 1024 * 1024


def _params(sem, vmem=VMEM_LIMIT):
    return pltpu.CompilerParams(dimension_semantics=sem, vmem_limit_bytes=vmem)


def _rms(x, g):
    return x * lax.rsqrt(jnp.mean(x * x, axis=-1, keepdims=True) + EPS) * g


def _ada_kernel(c_ref, w_ref, b_ref, o_ref):
    c = c_ref[...]
    sc = c * jax.nn.sigmoid(c)
    o_ref[...] = jnp.dot(sc.astype(BF16), w_ref[...].astype(BF16),
                         preferred_element_type=F32) + b_ref[...]


def _ada(cin, w, b):
    d, n = w.shape
    tn = 1536
    return pl.pallas_call(
        _ada_kernel,
        grid=(n // tn,),
        in_specs=[pl.BlockSpec((8, d), lambda j: (0, 0)),
                  pl.BlockSpec((d, tn), lambda j: (0, j)),
                  pl.BlockSpec((1, tn), lambda j: (0, j))],
        out_specs=pl.BlockSpec((8, tn), lambda j: (0, j)),
        out_shape=jax.ShapeDtypeStruct((8, n), F32),
        compiler_params=_params(("arbitrary",)),
        name="ada",
    )(cin, w, b.reshape(1, n))


def _inproj_kernel(x_ref, g_ref, sh_ref, sc_ref, w_ref, *refs, n_u, n_v):
    if len(refs) == 4:
        zu_ref, zv_ref, zg_ref, n_scr = refs
    else:
        zu_ref, zv_ref, n_scr = refs
        zg_ref = None
    j = pl.program_id(2)

    @pl.when(j == 0)
    def _():
        g, sh, scale1 = g_ref[...], sh_ref[...], 1.0 + sc_ref[...]
        for r in range(0, x_ref.shape[0], NORM_ROWS):
            rows = slice(r, r + NORM_ROWS)
            n_scr[rows, :] = (_rms(x_ref[rows, :], g) * scale1 + sh).astype(BF16)

    def project(out_ref):
        out_ref[...] = jnp.dot(n_scr[...], w_ref[...], preferred_element_type=F32).astype(out_ref.dtype)

    pl.when(j < n_u)(lambda: project(zu_ref))
    pl.when((j >= n_u) & (j < n_u + n_v))(lambda: project(zv_ref))
    if zg_ref is not None:
        pl.when(j >= n_u + n_v)(lambda: project(zg_ref))


def _inproj(x, g, mod6, mod_row, w_bf16, width_u, width_v):
    bsz, length, d = x.shape
    n = w_bf16.shape[1]
    tm = min(1024, length)
    tn = 512
    n_u, n_v = width_u // tn, width_v // tn
    n_g = n // tn - n_u - n_v
    kern = functools.partial(_inproj_kernel, n_u=n_u, n_v=n_v)
    row = lambda k: (lambda b, i, j: (mod_row(b) * 6 + k, 0, 0))
    out_specs = [pl.BlockSpec((None, tm, tn), lambda b, i, j: (b, i, jnp.minimum(j, n_u - 1))),
                 pl.BlockSpec((None, tm, tn), lambda b, i, j: (b, i, jnp.clip(j - n_u, 0, n_v - 1)))]
    out_shape = [jax.ShapeDtypeStruct((bsz, length, width_u), F32),
                 jax.ShapeDtypeStruct((bsz, length, width_v), F32)]
    if n_g:
        out_specs.append(pl.BlockSpec((None, tm, tn), lambda b, i, j: (b, i, jnp.maximum(j - n_u - n_v, 0))))
        out_shape.append(jax.ShapeDtypeStruct((bsz, length, n_g * tn), BF16))
    return pl.pallas_call(
        kern,
        grid=(bsz, length // tm, n // tn),
        in_specs=[pl.BlockSpec((None, tm, d), lambda b, i, j: (b, i, 0)),
                  pl.BlockSpec((1, d), lambda b, i, j: (0, 0)),
                  pl.BlockSpec((None, 1, d), row(0)),
                  pl.BlockSpec((None, 1, d), row(1)),
                  pl.BlockSpec((d, tn), lambda b, i, j: (0, j))],
        out_specs=out_specs,
        out_shape=out_shape,
        scratch_shapes=[pltpu.VMEM((tm, d), BF16)],
        compiler_params=_params(("arbitrary", "arbitrary", "arbitrary")),
        name="inproj",
    )(x, g.reshape(1, d), mod6, mod6, w_bf16)


S5_PREP_BATCH = 8


def _s5_prep_kernel(*refs):
    for i in range(S5_PREP_BATCH):
        _s5_prep_one(*[r.at[i] for r in refs])


def _s5_prep_one(are_ref, aim_ref, ldt_ref, bre_ref, bim_ref, cre_ref, cim_ref,
                 k_ref, cg_ref, bg_ref, a16_ref):
    nk = S5_CHUNK + 1
    dt = jnp.exp(ldt_ref[...])
    lre, lim = are_ref[...], aim_ref[...]
    xr, xi = lre * dt, lim * dt
    e1 = jnp.exp(xr)
    ar, ai = e1 * jnp.cos(xi), e1 * jnp.sin(xi)
    pows = [(jnp.ones_like(ar), jnp.zeros_like(ai))]
    for _ in range(nk - 1):
        kr, ki = pows[-1]
        pows.append((kr * ar - ki * ai, kr * ai + ki * ar))
    rep = lambda rows: jnp.concatenate([jnp.broadcast_to(r, (S5_GROUP, r.shape[1])) for r in rows], axis=0)
    pr, pim = rep([q[0] for q in pows]), rep([q[1] for q in pows])
    den = lre * lre + lim * lim
    qr = ((ar - 1.0) * lre + ai * lim) / den
    qi = (ai * lre - (ar - 1.0) * lim) / den
    bre, bim = bre_ref[...], bim_ref[...]
    bbr, bbi = qr * bre - qi * bim, qr * bim + qi * bre
    tile = lambda m: jnp.concatenate([m] * nk, axis=0)
    cr_t, ci_t, br_t, bi_t = tile(cre_ref[...]), tile(cim_ref[...]), tile(bbr), tile(bbi)
    cg = jnp.concatenate([cr_t * pr - ci_t * pim, -(cr_t * pim + ci_t * pr)], axis=1)
    bg = jnp.concatenate([br_t * pr - bi_t * pim, br_t * pim + bi_t * pr], axis=1)
    bcat = jnp.concatenate([bbr, bbi], axis=1)
    cg_ref[...] = cg
    bg_ref[...] = bg
    k_ref[...] = lax.dot_general(cg[:S5_CHUNK * S5_GROUP], bcat, (((1,), (1,)), ((), ())),
                                 precision=lax.Precision.HIGHEST, preferred_element_type=F32)
    n = S5_CHUNK * S5_GROUP
    a16_ref[...] = jnp.concatenate([pr[n:n + 1], pim[n:n + 1]], axis=1)


def _s5_operators(a_re, a_im, log_dt, b_re, b_im, c_re, c_im, d_skip):
    _, groups, p = a_re.shape
    h = S5_GROUP
    dg = 2 * groups
    nk = S5_CHUNK + 1
    vec = lambda t: t.reshape(dg, 1, p)
    tr = lambda t: jnp.swapaxes(t, -1, -2).reshape(dg, h, p)
    ldt = jnp.broadcast_to(log_dt[..., None], (2, groups, p))
    nb = S5_PREP_BATCH
    batch = lambda *tail: pl.BlockSpec((nb,) + tail, lambda i: (i,) + (0,) * len(tail))
    spec_v, spec_m = batch(1, p), batch(h, p)
    kfl, cg, bg, a16 = pl.pallas_call(
        _s5_prep_kernel,
        grid=(dg // nb,),
        in_specs=[spec_v, spec_v, spec_v, spec_m, spec_m, spec_m, spec_m],
        out_specs=[batch(S5_CHUNK * h, h), batch(nk * h, 2 * p), batch(nk * h, 2 * p), batch(1, 2 * p)],
        out_shape=[jax.ShapeDtypeStruct((dg, S5_CHUNK * h, h), F32),
                   jax.ShapeDtypeStruct((dg, nk * h, 2 * p), F32),
                   jax.ShapeDtypeStruct((dg, nk * h, 2 * p), F32),
                   jax.ShapeDtypeStruct((dg, 1, 2 * p), F32)],
        compiler_params=_params(("arbitrary",)),
        name="s5_prep",
    )(vec(a_re), vec(a_im), vec(ldt), tr(b_re), tr(b_im),
      c_re.reshape(dg, h, p), c_im.reshape(dg, h, p))

    t = S5_CHUNK
    kfl = kfl.reshape(2, groups, t, h, h)
    cg = cg.reshape(2, groups, nk, h, 2 * p)
    bg = bg.reshape(2, groups, nk, h, 2 * p)
    ti = jnp.arange(t)
    lag = ti[:, None] - ti[None, :]
    kf = jnp.where((lag >= 0)[None, :, :, None, None], kfl[0][:, jnp.clip(lag, 0, t - 1)], 0.0)
    kb = jnp.where((lag <= 0)[None, :, :, None, None], kfl[1][:, jnp.clip(-lag, 0, t - 1)], 0.0)
    eye = (lag == 0)[None, :, :, None, None] * jnp.eye(h, dtype=F32)[None, None, None]
    m = kf + kb + eye * d_skip.reshape(groups, 1, 1, 1, h)
    mt = m.transpose(0, 2, 4, 1, 3).reshape(groups, t * h, t * h)
    bp_f = bg[0][:, t - 1 - ti].reshape(groups, t * h, 2 * p)
    bp_b = bg[1][:, ti].reshape(groups, t * h, 2 * p)
    bp = jnp.concatenate([bp_f, bp_b], axis=-1)
    cp_f = cg[0][:, ti + 1].reshape(groups, t * h, 2 * p)
    cp_b = cg[1][:, t - ti].reshape(groups, t * h, 2 * p)
    cp = jnp.concatenate([cp_f, cp_b], axis=-1).swapaxes(1, 2)
    return mt.astype(BF16), bp.astype(BF16), cp.astype(BF16), a16.reshape(2, groups, 1, 2 * p)


def _cplx_coef(a):
    p = a.shape[1] // 2
    lane = lax.broadcasted_iota(jnp.int32, a.shape, 1)
    sw = pltpu.roll(a, p, axis=1)
    return jnp.where(lane < p, a, sw), jnp.where(lane < p, -sw, a)


def _cplx_mul(a, x):
    c1, c2 = _cplx_coef(a)
    return c1 * x + c2 * pltpu.roll(x, x.shape[1] // 2, axis=1)


def _chunk_scan(x, a, reverse):
    n = x.shape[0]
    row = lax.broadcasted_iota(jnp.int32, (n, 1), 0)
    o = 1
    while o < n:
        if reverse:
            sh = jnp.where(row < n - o, pltpu.roll(x, n - o, axis=0), 0.0)
        else:
            sh = jnp.where(row >= o, pltpu.roll(x, o, axis=0), 0.0)
        x = x + _cplx_mul(a, sh)
        a = _cplx_mul(a, a)
        o *= 2
    return x


S5_GPB = 128 // S5_GROUP
S5_TPT = 128 // S5_GROUP


def _chunk_perm():
    n = S5_TPT * S5_GPB * S5_GROUP
    i = jnp.arange(n)
    t, g, h = i // (S5_GPB * S5_GROUP), (i // S5_GROUP) % S5_GPB, i % S5_GROUP
    dst = g * (S5_TPT * S5_GROUP) + t * S5_GROUP + h
    return jnp.zeros((n, n), BF16).at[i, dst].set(1.0)


def _load_chunks(src_ref, perm, dst_scr):
    nc = src_ref.shape[0] // S5_CHUNK
    halves = []
    for r in range(S5_CHUNK // S5_TPT):
        xcat = jnp.concatenate(
            [src_ref[pl.ds(r * S5_TPT + t, nc, stride=S5_CHUNK), :].astype(BF16) for t in range(S5_TPT)], axis=1)
        halves.append(jnp.dot(xcat, perm, preferred_element_type=F32).astype(BF16))
    for g in range(S5_GPB):
        dst_scr[g] = jnp.concatenate([hv[:, g * 128:(g + 1) * 128] for hv in halves], axis=1)


def _s5_kernel(u_ref, uc_ref, perm_ref, permt_ref, mt_ref, bp_ref, cp_ref, af_ref, ab_ref, y_ref,
               u_scr, uc_scr, y_scr):
    nc = u_ref.shape[0] // S5_CHUNK
    _load_chunks(u_ref, perm_ref[...], u_scr)
    _load_chunks(uc_ref, perm_ref[...], uc_scr)

    def group(g, _):
        y_scr[g] = _s5_group(u_scr[g], uc_scr[g], mt_ref[g], bp_ref[g], cp_ref[g], af_ref[g], ab_ref[g])
        return 0

    lax.fori_loop(0, S5_GPB, group, 0)

    for r in range(S5_CHUNK // S5_TPT):
        ycat = jnp.concatenate([y_scr[g][:, r * 128:(r + 1) * 128] for g in range(S5_GPB)], axis=1)
        back = jnp.dot(ycat, permt_ref[...], preferred_element_type=F32)
        for t in range(S5_TPT):
            y_ref[pl.ds(r * S5_TPT + t, nc, stride=S5_CHUNK), :] = back[:, t * 128:(t + 1) * 128]


def _s5_group(u, uc, mt, bp, cp, af, ab):
    nc = u.shape[0]
    ps = af.shape[1]
    v = jnp.dot(u, bp, preferred_element_type=F32)
    vc = jnp.dot(uc, bp, preferred_element_type=F32)
    ncc = vc.shape[0]
    hcf = _chunk_scan(vc[:, :ps], af, False)[ncc - 1:ncc]
    hcb = _chunk_scan(vc[:, ps:], ab, True)[0:1]
    row = lax.broadcasted_iota(jnp.int32, (nc, 1), 0)
    vf = v[:, :ps] + jnp.where(row == 0, _cplx_mul(af, hcf), 0.0)
    vb = v[:, ps:] + jnp.where(row == nc - 1, _cplx_mul(ab, hcb), 0.0)
    sf = _chunk_scan(vf, af, False)
    sb = _chunk_scan(vb, ab, True)
    sin_f = jnp.where(row == 0, hcf, pltpu.roll(sf, 1, axis=0))
    sin_b = jnp.where(row == nc - 1, hcb, pltpu.roll(sb, nc - 1, axis=0))
    s_in = jnp.concatenate([sin_f, sin_b], axis=1).astype(BF16)
    y = (jnp.dot(u, mt, preferred_element_type=F32)
         + jnp.dot(s_in, cp, preferred_element_type=F32))
    return y.astype(BF16)


def _s5(zu, zu_c, mt, bp, cp, a16):
    bsz, length, w = zu.shape
    lctx = zu_c.shape[1]
    nc, ncc = length // S5_CHUNK, lctx // S5_CHUNK
    cw = S5_CHUNK * S5_GROUP
    ps = a16.shape[-1]
    perm = _chunk_perm()
    once = lambda shape, imap: pl.BlockSpec(shape, imap, pipeline_mode=pl.Buffered(1))
    wspec = pl.BlockSpec((S5_GPB, cw, cw), lambda b, q: (q, 0, 0))
    return pl.pallas_call(
        _s5_kernel,
        grid=(bsz, w // 128),
        in_specs=[once((None, length, 128), lambda b, q: (b, 0, q)),
                  pl.BlockSpec((None, lctx, 128), lambda b, q: (b, 0, q)),
                  once(perm.shape, lambda b, q: (0, 0)),
                  once(perm.shape, lambda b, q: (0, 0)),
                  wspec, wspec, wspec,
                  pl.BlockSpec((None, S5_GPB, 1, ps), lambda b, q: (0, q, 0, 0)),
                  pl.BlockSpec((None, S5_GPB, 1, ps), lambda b, q: (1, q, 0, 0))],
        out_specs=once((None, length, 128), lambda b, q: (b, 0, q)),
        out_shape=jax.ShapeDtypeStruct((bsz, length, w), F32),
        scratch_shapes=[pltpu.VMEM((S5_GPB, nc, cw), BF16),
                        pltpu.VMEM((S5_GPB, ncc, cw), BF16),
                        pltpu.VMEM((S5_GPB, nc, cw), BF16)],
        compiler_params=_params(("arbitrary", "arbitrary")),
        name="s5",
    )(zu, zu_c, perm, perm.T, mt, bp, cp, a16, a16)


LRU_ROW_BLOCK = 16


def _lru_kernel(v_ref, cw_ref, cb_ref, wr_ref, br_ref, wi_ref, bi_ref, lam_ref, h0_ref, o_ref,
                vpad, a_scr, b_scr, carry_scr, *, rows, rev, chain):
    ct = v_ref.shape[-1]
    cpt = COLS_PER_TILE
    rb = LRU_ROW_BLOCK
    nblk = rows // rb
    nslab = ct // MXU_TILE

    if chain:
        @pl.when(pl.program_id(2) == 0)
        def _():
            carry_scr[...] = h0_ref[...]

    zero_row = jnp.zeros((cpt, ct), F32)
    for k in range(CONV_PAD_LO):
        vpad[k] = zero_row
    for k in range(CONV_W - 1 - CONV_PAD_LO):
        vpad[CONV_PAD_LO + rows + k] = zero_row

    def copy(i, _):
        r0 = pl.multiple_of(i * rb, rb)
        vpad[pl.ds(r0 + CONV_PAD_LO, rb)] = v_ref[pl.ds(r0, rb)]
        return 0

    lax.fori_loop(0, nblk, copy, 0)

    cw = cw_ref[...]
    cb = cb_ref[...]
    nl = -lam_ref[...]
    softplus = jnp.maximum(nl, 0.0) + jnp.log1p(jnp.exp(-jnp.abs(nl)))
    c8 = -LRU_C * softplus
    b_r, b_i = br_ref[...], bi_ref[...]

    def block(i, carry):
        acc_a, acc_h = carry
        bi_ = (nblk - 1 - i) if rev else i
        r0 = pl.multiple_of(bi_ * rb, rb)
        x = cb
        for k in range(CONV_W):
            x = x + cw[k:k + 1] * vpad[pl.ds(r0 + k, rb)].reshape(rb * cpt, ct)
        xb = x.astype(BF16)
        pre_r = jnp.concatenate(
            [jnp.dot(xb[:, s * MXU_TILE:(s + 1) * MXU_TILE], wr_ref[s], preferred_element_type=F32)
             for s in range(nslab)], axis=1)
        pre_i = jnp.concatenate(
            [jnp.dot(xb[:, s * MXU_TILE:(s + 1) * MXU_TILE], wi_ref[s], preferred_element_type=F32)
             for s in range(nslab)], axis=1)
        r = jax.nn.sigmoid(pre_r + b_r)
        ig = jax.nn.sigmoid(pre_i + b_i)
        log_a = c8 * r
        a = jnp.exp(log_a)
        bx = jnp.sqrt(1.0 - a * a) * (ig * x)
        q0 = pl.multiple_of(r0 * cpt, rb * cpt)
        a_scr[pl.ds(q0, rb * cpt), :] = a
        b_scr[pl.ds(q0, rb * cpt), :] = bx
        order = range(rb - 1, -1, -1) if rev else range(rb)
        for j in order:
            aj = a[j * cpt:(j + 1) * cpt]
            acc_h = aj * acc_h + bx[j * cpt:(j + 1) * cpt]
            acc_a = acc_a * aj
        return acc_a, acc_h

    acc_a, acc_h = lax.fori_loop(0, nblk, block, (jnp.ones((cpt, ct), F32), jnp.zeros((cpt, ct), F32)))

    if not chain:
        o_ref[...] = acc_h
        return

    sub = lax.broadcasted_iota(jnp.int32, (cpt, ct), 0)
    carry = carry_scr[...]
    h_in = jnp.zeros((cpt, ct), F32)
    for s in (range(cpt - 1, -1, -1) if rev else range(cpt)):
        h_in = jnp.where(sub == s, carry, h_in)
        carry = acc_a[s:s + 1] * carry + acc_h[s:s + 1]
    carry_scr[...] = carry

    def row(i, h):
        r = (rows - 1 - i) if rev else i
        q = pl.multiple_of(r * cpt, cpt)
        h = a_scr[pl.ds(q, cpt), :] * h + b_scr[pl.ds(q, cpt), :]
        o_ref[r] = h
        return h

    lax.fori_loop(0, rows, row, h_in, unroll=8)


def _block_diag(w, per):
    hh, n, _ = w.shape
    eye = jnp.eye(per, dtype=w.dtype)
    return jnp.einsum('gpij,pq->gpiqj', w.reshape(hh // per, per, n, n), eye).reshape(hh // per, per * n, per * n)


def _lru(v5, conv_w, conv_b, lam, w_r, b_r, w_i, b_i, h0, *, rev, chain):
    bsz, rows, ncg, cpt, w = v5.shape
    ct = 512
    nct = w // ct
    per = MXU_TILE // (w // LRU_HEADS)
    wr = _block_diag(w_r, per).astype(BF16)
    wi = _block_diag(w_i, per).astype(BF16)
    nslab = ct // MXU_TILE
    cgi = (lambda c: ncg - 1 - c) if rev else (lambda c: c)
    vec = lambda n: pl.BlockSpec((n, ct), lambda b, k, c: (0, k))
    in_specs = [pl.BlockSpec((None, rows, None, cpt, ct), lambda b, k, c: (b, 0, cgi(c), 0, k)),
                vec(CONV_W), vec(1),
                pl.BlockSpec((nslab, MXU_TILE, MXU_TILE), lambda b, k, c: (k, 0, 0)), vec(1),
                pl.BlockSpec((nslab, MXU_TILE, MXU_TILE), lambda b, k, c: (k, 0, 0)), vec(1),
                vec(1),
                pl.BlockSpec((None, 1, ct), lambda b, k, c: (b, 0, k))]
    if chain:
        out_spec = pl.BlockSpec((None, rows, None, cpt, ct), lambda b, k, c: (b, 0, cgi(c), 0, k))
        out_shape = jax.ShapeDtypeStruct(v5.shape, F32)
    else:
        out_spec = pl.BlockSpec((None, None, cpt, ct), lambda b, k, c: (b, c, 0, k))
        out_shape = jax.ShapeDtypeStruct((bsz, ncg, cpt, w), F32)
    kern = functools.partial(_lru_kernel, rows=rows, rev=rev, chain=chain)
    return pl.pallas_call(
        kern,
        grid=(bsz, nct, ncg),
        in_specs=in_specs,
        out_specs=out_spec,
        out_shape=out_shape,
        scratch_shapes=[pltpu.VMEM((rows + CONV_W - 1, cpt, ct), F32),
                        pltpu.VMEM((rows * cpt, ct), F32),
                        pltpu.VMEM((rows * cpt, ct), F32),
                        pltpu.VMEM((1, ct), F32)],
        compiler_params=_params(("arbitrary", "arbitrary", "arbitrary")),
        name="lru_rev" if rev else "lru_fwd",
    )(v5, conv_w, conv_b.reshape(1, w), wr, b_r.reshape(1, w), wi, b_i.reshape(1, w),
      lam.reshape(1, w), h0)


def _merge_kernel(x_ref, ys_ref, ylf_ref, ylb_ref, zg_ref, gate_ref, sh_ref, sc_ref, g2_ref,
                  wglu_ref, wa_ref, wb_ref, wo_ref, h1_ref, ft_ref, *, w_lru, d):
    za = jax.nn.gelu(ys_ref[...])
    ya = (za * jax.nn.sigmoid(jnp.dot(za.astype(BF16), wglu_ref[...],
                                      preferred_element_type=F32))).astype(BF16)
    zg = zg_ref[...]
    yb = ((ylf_ref[...] + ylb_ref[...]) * jax.nn.gelu(zg[:, :w_lru].astype(F32))).astype(BF16)
    ga = jax.nn.sigmoid(zg[:, w_lru:w_lru + d].astype(F32))
    gb = jax.nn.sigmoid(zg[:, w_lru + d:].astype(F32))
    m = (ga * jnp.dot(ya, wa_ref[...], preferred_element_type=F32)
         + gb * jnp.dot(yb, wb_ref[...], preferred_element_type=F32))
    o = jnp.dot(m.astype(BF16), wo_ref[...], preferred_element_type=F32)
    h1 = x_ref[...] + gate_ref[...] * o
    h1_ref[...] = h1
    f = _rms(h1, g2_ref[...]) * (1.0 + sc_ref[...]) + sh_ref[...]
    ft_ref[...] = f.T.astype(BF16)


def _merge(x, ys5, ylf, ylb, zg, mod6, g2, w_glu, w_a, w_b, w_o):
    bsz, length, d = x.shape
    w_s5 = ys5.shape[-1]
    w_lru = ylf.shape[-1]
    tm = 256
    nt = length // tm
    row = lambda k: (lambda b, i: (b * 6 + k, 0, 0))
    tok = lambda w: pl.BlockSpec((None, tm, w), lambda b, i: (b, i, 0))
    const = lambda shape: pl.BlockSpec(shape, lambda b, i: (0, 0), pipeline_mode=pl.Buffered(1))
    kern = functools.partial(_merge_kernel, w_lru=w_lru, d=d)
    return pl.pallas_call(
        kern,
        grid=(bsz, nt),
        in_specs=[tok(d), tok(w_s5), tok(w_lru), tok(w_lru), tok(zg.shape[-1]),
                  pl.BlockSpec((None, 1, d), row(2)), pl.BlockSpec((None, 1, d), row(3)),
                  pl.BlockSpec((None, 1, d), row(4)),
                  pl.BlockSpec((1, d), lambda b, i: (0, 0)),
                  const(w_glu.shape), const(w_a.shape), const(w_b.shape), const(w_o.shape)],
        out_specs=[tok(d), pl.BlockSpec((d, tm), lambda b, i: (0, b * nt + i))],
        out_shape=[jax.ShapeDtypeStruct((bsz, length, d), F32),
                   jax.ShapeDtypeStruct((d, bsz * length), BF16)],
        compiler_params=_params(("arbitrary", "arbitrary")),
        name="merge",
    )(x, ys5, ylf, ylb, zg, mod6, mod6, mod6, g2.reshape(1, d), w_glu, w_a, w_b, w_o)


def _sort_pairs(n):
    out = []
    p = 1
    while p < n:
        k = p
        while k >= 1:
            for j in range(k % p, n - k, 2 * k):
                for i in range(min(k, n - j - k)):
                    if (i + j) // (2 * p) == (i + j + k) // (2 * p):
                        out.append((i + j, i + j + k))
            k //= 2
        p *= 2
    return out


def _top_rows_sorted(x, k):
    n_slab = x.shape[0] // 8
    slabs = [x[8 * r:8 * r + 8] for r in range(n_slab)]
    for a, b in _sort_pairs(n_slab):
        slabs[a], slabs[b] = jnp.maximum(slabs[a], slabs[b]), jnp.minimum(slabs[a], slabs[b])
    slabs.append(jnp.full_like(slabs[0], -jnp.inf))
    out = []
    for it in range(k):
        m = jnp.max(slabs[0], axis=0, keepdims=True)
        out.append(m)
        hit = slabs[0] == m
        for r in range(min(k - 1 - it, n_slab)):
            slabs[r] = jnp.where(hit, slabs[r + 1], slabs[r])
    return out


def _count_above(rows, x, strict):
    assert PEER_TOPK == 16
    above = (lambda a: a > x) if strict else (lambda a: a >= x)
    sel = jnp.where
    b3 = above(rows[7])
    b2 = above(sel(b3, rows[11], rows[3]))
    b1 = above(sel(b3, sel(b2, rows[13], rows[9]), sel(b2, rows[5], rows[1])))
    b0 = above(sel(b3, sel(b2, sel(b1, rows[14], rows[12]), sel(b1, rows[10], rows[8])),
                   sel(b2, sel(b1, rows[6], rows[4]), sel(b1, rows[2], rows[0]))))
    return (sel(b3, 8.0, 0.0) + sel(b2, 4.0, 0.0) + sel(b1, 2.0, 0.0) + sel(b0, 1.0, 0.0)
            + sel(above(rows[15]), 1.0, 0.0))


_CAND_PAIRS = [(j, k) for j in range(PEER_TOPK + 1) for k in range(PEER_TOPK + 1)
               if (j + 1) * (k + 1) <= PEER_TOPK + 1]
_CAND_ROWS = 64
assert len(_CAND_PAIRS) <= _CAND_ROWS


def _route_kernel(ft_ref, wq_ref, sk_ref, r2_ref, e2_ref, n1_ref, e1_ref, cand_scr):
    half = sk_ref.shape[2]
    tm = ft_ref.shape[1]
    qt = jnp.dot(wq_ref[...], ft_ref[...], preferred_element_type=F32).astype(BF16)
    for r in range(len(_CAND_PAIRS), _CAND_ROWS):
        cand_scr[r:r + 1, :] = jnp.full((1, 128), -jnp.inf, F32)
    for h in range(PEER_HEADS):
        s_full = [jnp.dot(sk_ref[side], qt[(2 * h + side) * half:(2 * h + side + 1) * half],
                          preferred_element_type=F32) for side in range(2)]
        for l in range(tm // 128):
            lanes = slice(l * 128, (l + 1) * 128)
            s = [sf[:, lanes] for sf in s_full]
            top0 = _top_rows_sorted(s[0], PEER_TOPK + 1)
            top1 = _top_rows_sorted(s[1], PEER_TOPK + 1)
            for r, (j, k) in enumerate(_CAND_PAIRS):
                cand_scr[r:r + 1, :] = top0[j] + top1[k]
            best = _top_rows_sorted(cand_scr[...], PEER_TOPK + 1)
            z = jnp.ones_like(best[0])
            for k in range(1, PEER_TOPK):
                z = z + jnp.exp(best[k] - best[0])
            t1 = 0.5 * (best[PEER_TOPK - 1] + best[PEER_TOPK]) - s[0]
            n1 = _count_above(top1, t1, strict=False)
            rank2 = _count_above(top1, s[1], strict=True)
            r2_ref[h, :, lanes] = rank2.astype(BF16)
            e2_ref[h, :, lanes] = jnp.exp(s[1] - top1[0]).astype(BF16)
            n1_ref[h, :, lanes] = n1
            e1_ref[h, :, lanes] = jnp.exp(s[0] - top0[0]) / z


def _route(ft, wq_t, sk):
    d, nt = ft.shape
    nk = sk.shape[1]
    tm = 256
    ospec = pl.BlockSpec((PEER_HEADS, nk, tm), lambda i: (0, 0, i))
    return pl.pallas_call(
        _route_kernel,
        grid=(nt // tm,),
        in_specs=[pl.BlockSpec((d, tm), lambda i: (0, i)),
                  pl.BlockSpec(wq_t.shape, lambda i: (0, 0), pipeline_mode=pl.Buffered(1)),
                  pl.BlockSpec(sk.shape, lambda i: (0, 0, 0))],
        out_specs=[ospec] * 4,
        out_shape=[jax.ShapeDtypeStruct((PEER_HEADS, nk, nt), dt) for dt in (BF16, BF16, F32, F32)],
        scratch_shapes=[pltpu.VMEM((_CAND_ROWS, 128), F32)],
        compiler_params=_params(("arbitrary",)),
        name="route",
    )(ft, wq_t, sk)


def _peer_kernel(ft_ref, u_ref, v_ref, r2_ref, e2_ref, n1_ref, e1_ref, h1_ref, gate_ref, gf_ref,
                 o_ref, acc_ref, *, nk):
    j = pl.program_id(1)

    @pl.when(j == 0)
    def _():
        acc_ref[...] = jnp.zeros_like(acc_ref)

    tm = ft_ref.shape[1]
    n_i1 = u_ref.shape[0] // nk
    per = MXU_TILE // nk
    ps = []
    for c in range(n_i1 // per):
        st = jnp.dot(u_ref[c * MXU_TILE:(c + 1) * MXU_TILE, :], ft_ref[...],
                     preferred_element_type=F32)
        for a in range(per):
            i1 = j * n_i1 + c * per + a
            act = jax.nn.gelu(st[a * nk:(a + 1) * nk].astype(BF16))
            w = jnp.zeros((nk, tm), BF16)
            for h in range(PEER_HEADS):
                cnt = n1_ref[h, pl.ds(i1, 1), :].astype(BF16)
                cf = e1_ref[h, pl.ds(i1, 1), :].astype(BF16)
                w = w + jnp.where(r2_ref[h] < cnt, e2_ref[h] * cf, jnp.zeros((), BF16))
            ps.append(w * act)
    p = jnp.concatenate(ps, axis=0)
    acc_ref[...] += lax.dot_general(p, v_ref[...], (((0,), (0,)), ((), ())), preferred_element_type=F32)

    @pl.when(j == pl.num_programs(1) - 1)
    def _():
        h2 = h1_ref[...] + gate_ref[...] * acc_ref[...]
        o_ref[...] = _rms(h2, gf_ref[...])


def _peer(ft, u, v, r2, e2, n1, e1, h1, mod6, g_final, tiles_per_batch_fn):
    d, nt = ft.shape
    ne = u.shape[0]
    nk = r2.shape[1]
    tm = 512
    et = 1024
    rspec = pl.BlockSpec((PEER_HEADS, nk, tm), lambda i, j: (0, 0, i), pipeline_mode=pl.Buffered(1))
    kern = functools.partial(_peer_kernel, nk=nk)
    return pl.pallas_call(
        kern,
        grid=(nt // tm, ne // et),
        in_specs=[pl.BlockSpec((d, tm), lambda i, j: (0, i)),
                  pl.BlockSpec((et, d), lambda i, j: (j, 0)),
                  pl.BlockSpec((et, d), lambda i, j: (j, 0)),
                  rspec, rspec, rspec, rspec,
                  pl.BlockSpec((tm, d), lambda i, j: (i, 0), pipeline_mode=pl.Buffered(1)),
                  pl.BlockSpec((None, 1, d), lambda i, j: (tiles_per_batch_fn(i, tm) * 6 + 5, 0, 0)),
                  pl.BlockSpec((1, d), lambda i, j: (0, 0))],
        out_specs=pl.BlockSpec((tm, d), lambda i, j: (i, 0), pipeline_mode=pl.Buffered(1)),
        out_shape=jax.ShapeDtypeStruct((nt, d), F32),
        scratch_shapes=[pltpu.VMEM((tm, d), F32)],
        compiler_params=_params(("arbitrary", "arbitrary")),
        name="peer",
    )(ft, u, v, r2, e2, n1, e1, h1, mod6, g_final.reshape(1, d))


def kernel(x, c, ctx, c_ctx, w_ada, b_ada, g_norm1, w_in, s5_a_re, s5_a_im, s5_log_dt, s5_b_re, s5_b_im, s5_c_re, s5_c_im, s5_d, s5_w_glu, lru_conv_w, lru_conv_b, lru_lambda, lru_w_r, lru_b_r, lru_w_i, lru_b_i, w_proj_a, w_proj_b, w_out, g_norm2, peer_w_query, peer_sub_keys, peer_u, peer_v, g_final):
    bsz, length, d = x.shape
    assert w_ada.shape[0] == 1, "single-layer kernel"
    rows = length // GRID_W
    w_s5 = s5_d.shape[-1]
    w_lru = lru_lambda.shape[-1]
    assert bsz < 8 and length % (GRID_W * LRU_ROW_BLOCK) == 0 and ctx.shape[1] % LRU_ROW_BLOCK == 0

    cin = jnp.zeros((8, d), F32).at[:bsz].set(c).at[bsz].set(c_ctx)
    mod6 = _ada(cin, w_ada[0], b_ada[0]).reshape(8 * 6, 1, d)

    w_in_b = w_in[0].astype(BF16)
    zu, zv, zg = _inproj(x, g_norm1[0], mod6, lambda b: b, w_in_b, w_s5, w_lru)
    zu_c, zv_c = _inproj(ctx, g_norm1[0], mod6, lambda b: bsz, w_in_b[:, :w_s5 + w_lru], w_s5, w_lru)

    mt, bp, cp, a16 = _s5_operators(s5_a_re[0], s5_a_im[0], s5_log_dt[0], s5_b_re[0], s5_b_im[0],
                                    s5_c_re[0], s5_c_im[0], s5_d[0])
    ys5 = _s5(zu, zu_c, mt, bp, cp, a16)

    lctx = ctx.shape[1]
    vc5 = jnp.zeros((1, lctx, 1, COLS_PER_TILE, w_lru), F32).at[0, :, 0, :bsz].set(zv_c.transpose(1, 0, 2))
    v5 = zv.reshape(bsz, rows, GRID_W // COLS_PER_TILE, COLS_PER_TILE, w_lru)
    zero_h = jnp.zeros((1, 1, w_lru), F32)
    yl = []
    for dr in range(2):
        args = (lru_conv_w[0], lru_conv_b[0], lru_lambda[0, dr], lru_w_r[0, dr], lru_b_r[0, dr],
                lru_w_i[0, dr], lru_b_i[0, dr])
        hc = _lru(vc5, *args, zero_h, rev=dr == 1, chain=False)
        h0 = hc[0, 0, :bsz].reshape(bsz, 1, w_lru)
        yl.append(_lru(v5, *args, h0, rev=dr == 1, chain=True).reshape(bsz, length, w_lru))

    h1, ft = _merge(x, ys5, yl[0], yl[1], zg, mod6, g_norm2[0], s5_w_glu[0].astype(BF16),
                    w_proj_a[0].astype(BF16), w_proj_b[0].astype(BF16), w_out[0].astype(BF16))

    r2, e2, n1, e1 = _route(ft, peer_w_query[0].T.astype(BF16), peer_sub_keys[0].astype(BF16))
    out = _peer(ft, peer_u[0].astype(BF16), peer_v[0].astype(BF16), r2, e2, n1, e1,
                h1.reshape(bsz * length, d), mod6, g_final,
                lambda i, tm: i // (length // tm))
    return out.reshape(bsz, length, d)
```

```python
import functools

import jax
import jax.numpy as jnp
from jax import lax
from jax.experimental import pallas as pl
from jax.experimental.pallas import tpu as pltpu

F32 = jnp.float32
BF16 = jnp.bfloat16
EPS = 1e-6
GRID_W = 64
S5_GROUP = 16
S5_STATE = 64
S5_CHUNK = 16
LRU_HEADS = 16
LRU_C = 8.0
CONV_W = 4
CONV_PAD_LO = (CONV_W - 1) // 2
N_KEYS = 128
PEER_HEADS = 8
PEER_TOPK = 16
COLS_PER_TILE = 8
MXU_TILE = 256
NORM_ROWS = 128
VMEM_LIMIT = 56 * 1024 * 1024


def _params(sem, vmem=VMEM_LIMIT):
    return pltpu.CompilerParams(dimension_semantics=sem, vmem_limit_bytes=vmem)


def _rms(x, g):
    return x * lax.rsqrt(jnp.mean(x * x, axis=-1, keepdims=True) + EPS) * g


def _ada_kernel(c_ref, w_ref, b_ref, o_ref):
    c = c_ref[...]
    sc = c * jax.nn.sigmoid(c)
    o_ref[...] = jnp.dot(sc.astype(BF16), w_ref[...].astype(BF16),
                         preferred_element_type=F32) + b_ref[...]


def _ada(cin, w, b):
    d, n = w.shape
    tn = 1536
    return pl.pallas_call(
        _ada_kernel,
        grid=(n // tn,),
        in_specs=[pl.BlockSpec((8, d), lambda j: (0, 0)),
                  pl.BlockSpec((d, tn), lambda j: (0, j)),
                  pl.BlockSpec((1, tn), lambda j: (0, j))],
        out_specs=pl.BlockSpec((8, tn), lambda j: (0, j)),
        out_shape=jax.ShapeDtypeStruct((8, n), F32),
        compiler_params=_params(("arbitrary",)),
        name="ada",
    )(cin, w, b.reshape(1, n))


def _inproj_kernel(x_ref, g_ref, sh_ref, sc_ref, w_ref, *refs, n_u, n_v):
    if len(refs) == 4:
        zu_ref, zv_ref, zg_ref, n_scr = refs
    else:
        zu_ref, zv_ref, n_scr = refs
        zg_ref = None
    j = pl.program_id(2)

    @pl.when(j == 0)
    def _():
        g, sh, scale1 = g_ref[...], sh_ref[...], 1.0 + sc_ref[...]
        for r in range(0, x_ref.shape[0], NORM_ROWS):
            rows = slice(r, r + NORM_ROWS)
            n_scr[rows, :] = (_rms(x_ref[rows, :], g) * scale1 + sh).astype(BF16)

    def project(out_ref):
        out_ref[...] = jnp.dot(n_scr[...], w_ref[...], preferred_element_type=F32).astype(out_ref.dtype)

    pl.when(j < n_u)(lambda: project(zu_ref))
    pl.when((j >= n_u) & (j < n_u + n_v))(lambda: project(zv_ref))
    if zg_ref is not None:
        pl.when(j >= n_u + n_v)(lambda: project(zg_ref))


def _inproj(x, g, mod6, mod_row, w_bf16, width_u, width_v):
    bsz, length, d = x.shape
    n = w_bf16.shape[1]
    tm = min(1024, length)
    tn = 512
    n_u, n_v = width_u // tn, width_v // tn
    n_g = n // tn - n_u - n_v
    kern = functools.partial(_inproj_kernel, n_u=n_u, n_v=n_v)
    row = lambda k: (lambda b, i, j: (mod_row(b) * 6 + k, 0, 0))
    out_specs = [pl.BlockSpec((None, tm, tn), lambda b, i, j: (b, i, jnp.minimum(j, n_u - 1))),
                 pl.BlockSpec((None, tm, tn), lambda b, i, j: (b, i, jnp.clip(j - n_u, 0, n_v - 1)))]
    out_shape = [jax.ShapeDtypeStruct((bsz, length, width_u), F32),
                 jax.ShapeDtypeStruct((bsz, length, width_v), F32)]
    if n_g:
        out_specs.append(pl.BlockSpec((None, tm, tn), lambda b, i, j: (b, i, jnp.maximum(j - n_u - n_v, 0))))
        out_shape.append(jax.ShapeDtypeStruct((bsz, length, n_g * tn), BF16))
    return pl.pallas_call(
        kern,
        grid=(bsz, length // tm, n // tn),
        in_specs=[pl.BlockSpec((None, tm, d), lambda b, i, j: (b, i, 0)),
                  pl.BlockSpec((1, d), lambda b, i, j: (0, 0)),
                  pl.BlockSpec((None, 1, d), row(0)),
                  pl.BlockSpec((None, 1, d), row(1)),
                  pl.BlockSpec((d, tn), lambda b, i, j: (0, j))],
        out_specs=out_specs,
        out_shape=out_shape,
        scratch_shapes=[pltpu.VMEM((tm, d), BF16)],
        compiler_params=_params(("arbitrary", "arbitrary", "arbitrary")),
        name="inproj",
    )(x, g.reshape(1, d), mod6, mod6, w_bf16)


S5_PREP_BATCH = 8


def _s5_prep_kernel(*refs):
    for i in range(S5_PREP_BATCH):
        _s5_prep_one(*[r.at[i] for r in refs])


def _s5_prep_one(are_ref, aim_ref, ldt_ref, bre_ref, bim_ref, cre_ref, cim_ref,
                 k_ref, cg_ref, bg_ref, a16_ref):
    nk = S5_CHUNK + 1
    dt = jnp.exp(ldt_ref[...])
    lre, lim = are_ref[...], aim_ref[...]
    xr, xi = lre * dt, lim * dt
    e1 = jnp.exp(xr)
    ar, ai = e1 * jnp.cos(xi), e1 * jnp.sin(xi)
    pows = [(jnp.ones_like(ar), jnp.zeros_like(ai))]
    for _ in range(nk - 1):
        kr, ki = pows[-1]
        pows.append((kr * ar - ki * ai, kr * ai + ki * ar))
    rep = lambda rows: jnp.concatenate([jnp.broadcast_to(r, (S5_GROUP, r.shape[1])) for r in rows], axis=0)
    pr, pim = rep([q[0] for q in pows]), rep([q[1] for q in pows])
    den = lre * lre + lim * lim
    qr = ((ar - 1.0) * lre + ai * lim) / den
    qi = (ai * lre - (ar - 1.0) * lim) / den
    bre, bim = bre_ref[...], bim_ref[...]
    bbr, bbi = qr * bre - qi * bim, qr * bim + qi * bre
    tile = lambda m: jnp.concatenate([m] * nk, axis=0)
    cr_t, ci_t, br_t, bi_t = tile(cre_ref[...]), tile(cim_ref[...]), tile(bbr), tile(bbi)
    cg = jnp.concatenate([cr_t * pr - ci_t * pim, -(cr_t * pim + ci_t * pr)], axis=1)
    bg = jnp.concatenate([br_t * pr - bi_t * pim, br_t * pim + bi_t * pr], axis=1)
    bcat = jnp.concatenate([bbr, bbi], axis=1)
    cg_ref[...] = cg
    bg_ref[...] = bg
    k_ref[...] = lax.dot_general(cg[:S5_CHUNK * S5_GROUP], bcat, (((1,), (1,)), ((), ())),
                                 precision=lax.Precision.HIGHEST, preferred_element_type=F32)
    n = S5_CHUNK * S5_GROUP
    a16_ref[...] = jnp.concatenate([pr[n:n + 1], pim[n:n + 1]], axis=1)


def _s5_operators(a_re, a_im, log_dt, b_re, b_im, c_re, c_im, d_skip):
    _, groups, p = a_re.shape
    h = S5_GROUP
    dg = 2 * groups
    nk = S5_CHUNK + 1
    vec = lambda t: t.reshape(dg, 1, p)
    tr = lambda t: jnp.swapaxes(t, -1, -2).reshape(dg, h, p)
    ldt = jnp.broadcast_to(log_dt[..., None], (2, groups, p))
    nb = S5_PREP_BATCH
    batch = lambda *tail: pl.BlockSpec((nb,) + tail, lambda i: (i,) + (0,) * len(tail))
    spec_v, spec_m = batch(1, p), batch(h, p)
    kfl, cg, bg, a16 = pl.pallas_call(
        _s5_prep_kernel,
        grid=(dg // nb,),
        in_specs=[spec_v, spec_v, spec_v, spec_m, spec_m, spec_m, spec_m],
        out_specs=[batch(S5_CHUNK * h, h), batch(nk * h, 2 * p), batch(nk * h, 2 * p), batch(1, 2 * p)],
        out_shape=[jax.ShapeDtypeStruct((dg, S5_CHUNK * h, h), F32),
                   jax.ShapeDtypeStruct((dg, nk * h, 2 * p), F32),
                   jax.ShapeDtypeStruct((dg, nk * h, 2 * p), F32),
                   jax.ShapeDtypeStruct((dg, 1, 2 * p), F32)],
        compiler_params=_params(("arbitrary",)),
        name="s5_prep",
    )(vec(a_re), vec(a_im), vec(ldt), tr(b_re), tr(b_im),
      c_re.reshape(dg, h, p), c_im.reshape(dg, h, p))

    t = S5_CHUNK
    kfl = kfl.reshape(2, groups, t, h, h)
    cg = cg.reshape(2, groups, nk, h, 2 * p)
    bg = bg.reshape(2, groups, nk, h, 2 * p)
    ti = jnp.arange(t)
    lag = ti[:, None] - ti[None, :]
    kf = jnp.where((lag >= 0)[None, :, :, None, None], kfl[0][:, jnp.clip(lag, 0, t - 1)], 0.0)
    kb = jnp.where((lag <= 0)[None, :, :, None, None], kfl[1][:, jnp.clip(-lag, 0, t - 1)], 0.0)
    eye = (lag == 0)[None, :, :, None, None] * jnp.eye(h, dtype=F32)[None, None, None]
    m = kf + kb + eye * d_skip.reshape(groups, 1, 1, 1, h)
    mt = m.transpose(0, 2, 4, 1, 3).reshape(groups, t * h, t * h)
    bp_f = bg[0][:, t - 1 - ti].reshape(groups, t * h, 2 * p)
    bp_b = bg[1][:, ti].reshape(groups, t * h, 2 * p)
    bp = jnp.concatenate([bp_f, bp_b], axis=-1)
    cp_f = cg[0][:, ti + 1].reshape(groups, t * h, 2 * p)
    cp_b = cg[1][:, t - ti].reshape(groups, t * h, 2 * p)
    cp = jnp.concatenate([cp_f, cp_b], axis=-1).swapaxes(1, 2)
    return mt.astype(BF16), bp.astype(BF16), cp.astype(BF16), a16.reshape(2, groups, 1, 2 * p)


def _cplx_coef(a):
    p = a.shape[1] // 2
    lane = lax.broadcasted_iota(jnp.int32, a.shape, 1)
    sw = pltpu.roll(a, p, axis=1)
    return jnp.where(lane < p, a, sw), jnp.where(lane < p, -sw, a)


def _cplx_mul(a, x):
    c1, c2 = _cplx_coef(a)
    return c1 * x + c2 * pltpu.roll(x, x.shape[1] // 2, axis=1)


def _chunk_scan(x, a, reverse):
    n = x.shape[0]
    row = lax.broadcasted_iota(jnp.int32, (n, 1), 0)
    o = 1
    while o < n:
        if reverse:
            sh = jnp.where(row < n - o, pltpu.roll(x, n - o, axis=0), 0.0)
        else:
            sh = jnp.where(row >= o, pltpu.roll(x, o, axis=0), 0.0)
        x = x + _cplx_mul(a, sh)
        a = _cplx_mul(a, a)
        o *= 2
    return x


S5_SUPER = 16


def _entering_states(v_ref, sin_ref, a, h0, reverse):
    k_n = S5_SUPER
    ns = v_ref.shape[0] // k_n
    half = a.shape[1] // 2
    order = list(range(k_n - 1, -1, -1)) if reverse else list(range(k_n))
    rows_at = lambda k: pl.ds(k, ns, stride=k_n)
    pows = [None, a]
    for _ in range(k_n - 1):
        pows.append(_cplx_mul(a, pows[-1]))
    c1, c2 = _cplx_coef(a)
    loc = {}
    s = ss = None
    for k in order:
        x = v_ref[rows_at(k), :]
        xs = pltpu.roll(x, half, axis=1)
        s, ss = (x, xs) if s is None else (c1 * s + c2 * ss + x, c1 * ss - c2 * s + xs)
        loc[k] = s
    row = lax.broadcasted_iota(jnp.int32, (ns, 1), 0)
    edge = ns - 1 if reverse else 0
    ends = loc[order[-1]] + jnp.where(row == edge, _cplx_mul(pows[k_n], h0), 0.0)
    true_ends = _chunk_scan(ends, pows[k_n], reverse)
    cin = jnp.where(row == edge, h0, pltpu.roll(true_ends, ns - 1 if reverse else 1, axis=0))
    sin_ref[rows_at(order[0]), :] = cin
    for i in range(1, k_n):
        sin_ref[rows_at(order[i]), :] = loc[order[i - 1]] + _cplx_mul(pows[i], cin)


S5_GPB = 128 // S5_GROUP
S5_TPT = 128 // S5_GROUP


def _chunk_perm():
    n = S5_TPT * S5_GPB * S5_GROUP
    i = jnp.arange(n)
    t, g, h = i // (S5_GPB * S5_GROUP), (i // S5_GROUP) % S5_GPB, i % S5_GROUP
    dst = g * (S5_TPT * S5_GROUP) + t * S5_GROUP + h
    return jnp.zeros((n, n), BF16).at[i, dst].set(1.0)


def _load_chunks(src_ref, perm, dst_scr):
    nc = src_ref.shape[0] // S5_CHUNK
    halves = []
    for r in range(S5_CHUNK // S5_TPT):
        xcat = jnp.concatenate(
            [src_ref[pl.ds(r * S5_TPT + t, nc, stride=S5_CHUNK), :].astype(BF16) for t in range(S5_TPT)], axis=1)
        halves.append(jnp.dot(xcat, perm, preferred_element_type=F32).astype(BF16))
    for g in range(S5_GPB):
        dst_scr[g] = jnp.concatenate([hv[:, g * 128:(g + 1) * 128] for hv in halves], axis=1)


def _s5_kernel(u_ref, uc_ref, perm_ref, permt_ref, mt_ref, bp_ref, cp_ref, af_ref, ab_ref, y_ref,
               u_scr, uc_scr, y_scr, v_scr, sin_scr):
    nc = u_ref.shape[0] // S5_CHUNK
    _load_chunks(u_ref, perm_ref[...], u_scr)
    _load_chunks(uc_ref, perm_ref[...], uc_scr)

    def group(g, _):
        y_scr[g] = _s5_group(u_scr[g], uc_scr[g], mt_ref[g], bp_ref[g], cp_ref[g], af_ref[g], ab_ref[g],
                             v_scr.at[g % 2], sin_scr.at[g % 2])
        return 0

    lax.fori_loop(0, S5_GPB, group, 0, unroll=2)

    for r in range(S5_CHUNK // S5_TPT):
        ycat = jnp.concatenate([y_scr[g][:, r * 128:(r + 1) * 128] for g in range(S5_GPB)], axis=1)
        back = jnp.dot(ycat, permt_ref[...], preferred_element_type=F32)
        for t in range(S5_TPT):
            y_ref[pl.ds(r * S5_TPT + t, nc, stride=S5_CHUNK), :] = back[:, t * 128:(t + 1) * 128]


def _s5_group(u, uc, mt, bp, cp, af, ab, v_scr, sin_scr):
    ps = af.shape[1]
    v = jnp.dot(u, bp, preferred_element_type=F32)
    v_scr[0], v_scr[1] = v[:, :ps], v[:, ps:]
    vc = jnp.dot(uc, bp, preferred_element_type=F32)
    ncc = vc.shape[0]
    hcf = _chunk_scan(vc[:, :ps], af, False)[ncc - 1:ncc]
    hcb = _chunk_scan(vc[:, ps:], ab, True)[0:1]
    _entering_states(v_scr.at[0], sin_scr.at[0], af, hcf, False)
    _entering_states(v_scr.at[1], sin_scr.at[1], ab, hcb, True)
    s_in = jnp.concatenate([sin_scr[0], sin_scr[1]], axis=1).astype(BF16)
    y = (jnp.dot(u, mt, preferred_element_type=F32)
         + jnp.dot(s_in, cp, preferred_element_type=F32))
    return y.astype(BF16)


def _s5(zu, zu_c, mt, bp, cp, a16):
    bsz, length, w = zu.shape
    lctx = zu_c.shape[1]
    nc, ncc = length // S5_CHUNK, lctx // S5_CHUNK
    cw = S5_CHUNK * S5_GROUP
    ps = a16.shape[-1]
    perm = _chunk_perm()
    once = lambda shape, imap: pl.BlockSpec(shape, imap, pipeline_mode=pl.Buffered(1))
    wspec = pl.BlockSpec((S5_GPB, cw, cw), lambda b, q: (q, 0, 0))
    return pl.pallas_call(
        _s5_kernel,
        grid=(bsz, w // 128),
        in_specs=[once((None, length, 128), lambda b, q: (b, 0, q)),
                  pl.BlockSpec((None, lctx, 128), lambda b, q: (b, 0, q)),
                  once(perm.shape, lambda b, q: (0, 0)),
                  once(perm.shape, lambda b, q: (0, 0)),
                  wspec, wspec, wspec,
                  pl.BlockSpec((None, S5_GPB, 1, ps), lambda b, q: (0, q, 0, 0)),
                  pl.BlockSpec((None, S5_GPB, 1, ps), lambda b, q: (1, q, 0, 0))],
        out_specs=once((None, length, 128), lambda b, q: (b, 0, q)),
        out_shape=jax.ShapeDtypeStruct((bsz, length, w), F32),
        scratch_shapes=[pltpu.VMEM((S5_GPB, nc, cw), BF16),
                        pltpu.VMEM((S5_GPB, ncc, cw), BF16),
                        pltpu.VMEM((S5_GPB, nc, cw), BF16),
                        pltpu.VMEM((2, 2, nc, ps), F32),
                        pltpu.VMEM((2, 2, nc, ps), F32)],
        compiler_params=_params(("arbitrary", "arbitrary")),
        name="s5",
    )(zu, zu_c, perm, perm.T, mt, bp, cp, a16, a16)


LRU_ROW_BLOCK = 16


def _lru_kernel(v_ref, cw_ref, cb_ref, wr_ref, br_ref, wi_ref, bi_ref, lam_ref, h0_ref, o_ref,
                vpad, a_scr, b_scr, carry_scr, *, rows, rev, chain):
    ct = v_ref.shape[-1]
    cpt = COLS_PER_TILE
    rb = LRU_ROW_BLOCK
    nblk = rows // rb
    nslab = ct // MXU_TILE

    if chain:
        @pl.when(pl.program_id(2) == 0)
        def _():
            carry_scr[...] = h0_ref[...]

    zero_row = jnp.zeros((cpt, ct), F32)
    for k in range(CONV_PAD_LO):
        vpad[k] = zero_row
    for k in range(CONV_W - 1 - CONV_PAD_LO):
        vpad[CONV_PAD_LO + rows + k] = zero_row

    def copy(i, _):
        r0 = pl.multiple_of(i * rb, rb)
        vpad[pl.ds(r0 + CONV_PAD_LO, rb)] = v_ref[pl.ds(r0, rb)]
        return 0

    lax.fori_loop(0, nblk, copy, 0)

    cw = cw_ref[...]
    cb = cb_ref[...]
    nl = -lam_ref[...]
    softplus = jnp.maximum(nl, 0.0) + jnp.log1p(jnp.exp(-jnp.abs(nl)))
    c8 = -LRU_C * softplus
    b_r, b_i = br_ref[...], bi_ref[...]

    def block(i, carry):
        acc_a, acc_h = carry
        bi_ = (nblk - 1 - i) if rev else i
        r0 = pl.multiple_of(bi_ * rb, rb)
        x = cb
        for k in range(CONV_W):
            x = x + cw[k:k + 1] * vpad[pl.ds(r0 + k, rb)].reshape(rb * cpt, ct)
        xb = x.astype(BF16)
        pre_r = jnp.concatenate(
            [jnp.dot(xb[:, s * MXU_TILE:(s + 1) * MXU_TILE], wr_ref[s], preferred_element_type=F32)
             for s in range(nslab)], axis=1)
        pre_i = jnp.concatenate(
            [jnp.dot(xb[:, s * MXU_TILE:(s + 1) * MXU_TILE], wi_ref[s], preferred_element_type=F32)
             for s in range(nslab)], axis=1)
        r = jax.nn.sigmoid(pre_r + b_r)
        ig = jax.nn.sigmoid(pre_i + b_i)
        log_a = c8 * r
        a = jnp.exp(log_a)
        bx = jnp.sqrt(1.0 - a * a) * (ig * x)
        q0 = pl.multiple_of(r0 * cpt, rb * cpt)
        a_scr[pl.ds(q0, rb * cpt), :] = a
        b_scr[pl.ds(q0, rb * cpt), :] = bx
        order = range(rb - 1, -1, -1) if rev else range(rb)
        for j in order:
            aj = a[j * cpt:(j + 1) * cpt]
            acc_h = aj * acc_h + bx[j * cpt:(j + 1) * cpt]
            acc_a = acc_a * aj
        return acc_a, acc_h

    acc_a, acc_h = lax.fori_loop(0, nblk, block, (jnp.ones((cpt, ct), F32), jnp.zeros((cpt, ct), F32)))

    if not chain:
        o_ref[...] = acc_h
        return

    sub = lax.broadcasted_iota(jnp.int32, (cpt, ct), 0)
    carry = carry_scr[...]
    h_in = jnp.zeros((cpt, ct), F32)
    for s in (range(cpt - 1, -1, -1) if rev else range(cpt)):
        h_in = jnp.where(sub == s, carry, h_in)
        carry = acc_a[s:s + 1] * carry + acc_h[s:s + 1]
    carry_scr[...] = carry

    def row(i, h):
        r = (rows - 1 - i) if rev else i
        q = pl.multiple_of(r * cpt, cpt)
        h = a_scr[pl.ds(q, cpt), :] * h + b_scr[pl.ds(q, cpt), :]
        o_ref[r] = h
        return h

    lax.fori_loop(0, rows, row, h_in, unroll=8)


def _block_diag(w, per):
    hh, n, _ = w.shape
    eye = jnp.eye(per, dtype=w.dtype)
    return jnp.einsum('gpij,pq->gpiqj', w.reshape(hh // per, per, n, n), eye).reshape(hh // per, per * n, per * n)


def _lru(v5, conv_w, conv_b, lam, w_r, b_r, w_i, b_i, h0, *, rev, chain):
    bsz, rows, ncg, cpt, w = v5.shape
    ct = 512
    nct = w // ct
    per = MXU_TILE // (w // LRU_HEADS)
    wr = _block_diag(w_r, per).astype(BF16)
    wi = _block_diag(w_i, per).astype(BF16)
    nslab = ct // MXU_TILE
    cgi = (lambda c: ncg - 1 - c) if rev else (lambda c: c)
    vec = lambda n: pl.BlockSpec((n, ct), lambda b, k, c: (0, k))
    in_specs = [pl.BlockSpec((None, rows, None, cpt, ct), lambda b, k, c: (b, 0, cgi(c), 0, k)),
                vec(CONV_W), vec(1),
                pl.BlockSpec((nslab, MXU_TILE, MXU_TILE), lambda b, k, c: (k, 0, 0)), vec(1),
                pl.BlockSpec((nslab, MXU_TILE, MXU_TILE), lambda b, k, c: (k, 0, 0)), vec(1),
                vec(1),
                pl.BlockSpec((None, 1, ct), lambda b, k, c: (b, 0, k))]
    if chain:
        out_spec = pl.BlockSpec((None, rows, None, cpt, ct), lambda b, k, c: (b, 0, cgi(c), 0, k))
        out_shape = jax.ShapeDtypeStruct(v5.shape, F32)
    else:
        out_spec = pl.BlockSpec((None, None, cpt, ct), lambda b, k, c: (b, c, 0, k))
        out_shape = jax.ShapeDtypeStruct((bsz, ncg, cpt, w), F32)
    kern = functools.partial(_lru_kernel, rows=rows, rev=rev, chain=chain)
    return pl.pallas_call(
        kern,
        grid=(bsz, nct, ncg),
        in_specs=in_specs,
        out_specs=out_spec,
        out_shape=out_shape,
        scratch_shapes=[pltpu.VMEM((rows + CONV_W - 1, cpt, ct), F32),
                        pltpu.VMEM((rows * cpt, ct), F32),
                        pltpu.VMEM((rows * cpt, ct), F32),
                        pltpu.VMEM((1, ct), F32)],
        compiler_params=_params(("arbitrary", "arbitrary", "arbitrary")),
        name="lru_rev" if rev else "lru_fwd",
    )(v5, conv_w, conv_b.reshape(1, w), wr, b_r.reshape(1, w), wi, b_i.reshape(1, w),
      lam.reshape(1, w), h0)


def _merge_kernel(x_ref, ys_ref, ylf_ref, ylb_ref, zg_ref, gate_ref, sh_ref, sc_ref, g2_ref,
                  wglu_ref, wa_ref, wb_ref, wo_ref, h1_ref, ft_ref, *, w_lru, d):
    za = jax.nn.gelu(ys_ref[...])
    ya = (za * jax.nn.sigmoid(jnp.dot(za.astype(BF16), wglu_ref[...],
                                      preferred_element_type=F32))).astype(BF16)
    zg = zg_ref[...]
    yb = ((ylf_ref[...] + ylb_ref[...]) * jax.nn.gelu(zg[:, :w_lru].astype(F32))).astype(BF16)
    ga = jax.nn.sigmoid(zg[:, w_lru:w_lru + d].astype(F32))
    gb = jax.nn.sigmoid(zg[:, w_lru + d:].astype(F32))
    m = (ga * jnp.dot(ya, wa_ref[...], preferred_element_type=F32)
         + gb * jnp.dot(yb, wb_ref[...], preferred_element_type=F32))
    o = jnp.dot(m.astype(BF16), wo_ref[...], preferred_element_type=F32)
    h1 = x_ref[...] + gate_ref[...] * o
    h1_ref[...] = h1
    f = _rms(h1, g2_ref[...]) * (1.0 + sc_ref[...]) + sh_ref[...]
    ft_ref[...] = f.T.astype(BF16)


def _merge(x, ys5, ylf, ylb, zg, mod6, g2, w_glu, w_a, w_b, w_o):
    bsz, length, d = x.shape
    w_s5 = ys5.shape[-1]
    w_lru = ylf.shape[-1]
    tm = 256
    nt = length // tm
    row = lambda k: (lambda b, i: (b * 6 + k, 0, 0))
    tok = lambda w: pl.BlockSpec((None, tm, w), lambda b, i: (b, i, 0))
    const = lambda shape: pl.BlockSpec(shape, lambda b, i: (0, 0), pipeline_mode=pl.Buffered(1))
    kern = functools.partial(_merge_kernel, w_lru=w_lru, d=d)
    return pl.pallas_call(
        kern,
        grid=(bsz, nt),
        in_specs=[tok(d), tok(w_s5), tok(w_lru), tok(w_lru), tok(zg.shape[-1]),
                  pl.BlockSpec((None, 1, d), row(2)), pl.BlockSpec((None, 1, d), row(3)),
                  pl.BlockSpec((None, 1, d), row(4)),
                  pl.BlockSpec((1, d), lambda b, i: (0, 0)),
                  const(w_glu.shape), const(w_a.shape), const(w_b.shape), const(w_o.shape)],
        out_specs=[tok(d), pl.BlockSpec((d, tm), lambda b, i: (0, b * nt + i))],
        out_shape=[jax.ShapeDtypeStruct((bsz, length, d), F32),
                   jax.ShapeDtypeStruct((d, bsz * length), BF16)],
        compiler_params=_params(("arbitrary", "arbitrary")),
        name="merge",
    )(x, ys5, ylf, ylb, zg, mod6, mod6, mod6, g2.reshape(1, d), w_glu, w_a, w_b, w_o)


def _sort_pairs(n):
    out = []
    p = 1
    while p < n:
        k = p
        while k >= 1:
            for j in range(k % p, n - k, 2 * k):
                for i in range(min(k, n - j - k)):
                    if (i + j) // (2 * p) == (i + j + k) // (2 * p):
                        out.append((i + j, i + j + k))
            k //= 2
        p *= 2
    return out


def _top_rows_sorted(x, k):
    n_slab = x.shape[0] // 8
    slabs = [x[8 * r:8 * r + 8] for r in range(n_slab)]
    for a, b in _sort_pairs(n_slab):
        slabs[a], slabs[b] = jnp.maximum(slabs[a], slabs[b]), jnp.minimum(slabs[a], slabs[b])
    slabs.append(jnp.full_like(slabs[0], -jnp.inf))
    out = []
    for it in range(k):
        m = jnp.max(slabs[0], axis=0, keepdims=True)
        out.append(m)
        hit = slabs[0] == m
        for r in range(min(k - 1 - it, n_slab)):
            slabs[r] = jnp.where(hit, slabs[r + 1], slabs[r])
    return out


def _count_above(rows, x, strict):
    assert PEER_TOPK == 16
    above = (lambda a: a > x) if strict else (lambda a: a >= x)
    sel = jnp.where
    b3 = above(rows[7])
    b2 = above(sel(b3, rows[11], rows[3]))
    b1 = above(sel(b3, sel(b2, rows[13], rows[9]), sel(b2, rows[5], rows[1])))
    b0 = above(sel(b3, sel(b2, sel(b1, rows[14], rows[12]), sel(b1, rows[10], rows[8])),
                   sel(b2, sel(b1, rows[6], rows[4]), sel(b1, rows[2], rows[0]))))
    return (sel(b3, 8.0, 0.0) + sel(b2, 4.0, 0.0) + sel(b1, 2.0, 0.0) + sel(b0, 1.0, 0.0)
            + sel(above(rows[15]), 1.0, 0.0))


_CAND_PAIRS = [(j, k) for j in range(PEER_TOPK + 1) for k in range(PEER_TOPK + 1)
               if (j + 1) * (k + 1) <= PEER_TOPK + 1]
_CAND_ROWS = 64
assert len(_CAND_PAIRS) <= _CAND_ROWS


def _route_kernel(ft_ref, wq_ref, sk_ref, r2_ref, e2_ref, n1_ref, e1_ref, cand_scr):
    half = sk_ref.shape[2]
    tm = ft_ref.shape[1]
    qt = jnp.dot(wq_ref[...], ft_ref[...], preferred_element_type=F32).astype(BF16)
    for r in range(len(_CAND_PAIRS), _CAND_ROWS):
        cand_scr[r:r + 1, :] = jnp.full((1, 128), -jnp.inf, F32)
    for h in range(PEER_HEADS):
        s_full = [jnp.dot(sk_ref[side], qt[(2 * h + side) * half:(2 * h + side + 1) * half],
                          preferred_element_type=F32) for side in range(2)]
        for l in range(tm // 128):
            lanes = slice(l * 128, (l + 1) * 128)
            s = [sf[:, lanes] for sf in s_full]
            top0 = _top_rows_sorted(s[0], PEER_TOPK + 1)
            top1 = _top_rows_sorted(s[1], PEER_TOPK + 1)
            for r, (j, k) in enumerate(_CAND_PAIRS):
                cand_scr[r:r + 1, :] = top0[j] + top1[k]
            best = _top_rows_sorted(cand_scr[...], PEER_TOPK + 1)
            z = jnp.ones_like(best[0])
            for k in range(1, PEER_TOPK):
                z = z + jnp.exp(best[k] - best[0])
            t1 = 0.5 * (best[PEER_TOPK - 1] + best[PEER_TOPK]) - s[0]
            n1 = _count_above(top1, t1, strict=False)
            rank2 = _count_above(top1, s[1], strict=True)
            r2_ref[h, :, lanes] = rank2.astype(BF16)
            e2_ref[h, :, lanes] = jnp.exp(s[1] - top1[0]).astype(BF16)
            n1_ref[h, :, lanes] = n1
            e1_ref[h, :, lanes] = jnp.exp(s[0] - top0[0]) / z


def _route(ft, wq_t, sk):
    d, nt = ft.shape
    nk = sk.shape[1]
    tm = 256
    ospec = pl.BlockSpec((PEER_HEADS, nk, tm), lambda i: (0, 0, i))
    return pl.pallas_call(
        _route_kernel,
        grid=(nt // tm,),
        in_specs=[pl.BlockSpec((d, tm), lambda i: (0, i)),
                  pl.BlockSpec(wq_t.shape, lambda i: (0, 0), pipeline_mode=pl.Buffered(1)),
                  pl.BlockSpec(sk.shape, lambda i: (0, 0, 0))],
        out_specs=[ospec] * 4,
        out_shape=[jax.ShapeDtypeStruct((PEER_HEADS, nk, nt), dt) for dt in (BF16, BF16, F32, F32)],
        scratch_shapes=[pltpu.VMEM((_CAND_ROWS, 128), F32)],
        compiler_params=_params(("arbitrary",)),
        name="route",
    )(ft, wq_t, sk)


def _rows_bf16(row, n):
    tile = jnp.broadcast_to(row, (16, row.shape[1])).astype(BF16)
    return jnp.concatenate([tile] * (n // 16), axis=0)


def _peer_kernel(ft_ref, u_ref, v_ref, r2_ref, e2_ref, n1_ref, e1_ref, h1_ref, gate_ref, gf_ref,
                 o_ref, acc_ref, *, nk):
    j = pl.program_id(1)

    @pl.when(j == 0)
    def _():
        acc_ref[...] = jnp.zeros_like(acc_ref)

    tm = ft_ref.shape[1]
    n_i1 = u_ref.shape[0] // nk
    per = MXU_TILE // nk
    ps = []
    for c in range(n_i1 // per):
        st = jnp.dot(u_ref[c * MXU_TILE:(c + 1) * MXU_TILE, :], ft_ref[...],
                     preferred_element_type=F32)
        for a in range(per):
            i1 = j * n_i1 + c * per + a
            act = jax.nn.gelu(st[a * nk:(a + 1) * nk].astype(BF16))
            w = jnp.zeros((nk, tm), BF16)
            for h in range(PEER_HEADS):
                cnt = _rows_bf16(n1_ref[h, pl.ds(i1, 1), :], nk)
                cf = _rows_bf16(e1_ref[h, pl.ds(i1, 1), :], nk)
                w = w + jnp.where(r2_ref[h] < cnt, e2_ref[h] * cf, jnp.zeros((), BF16))
            ps.append(w * act)
    p = jnp.concatenate(ps, axis=0)
    acc_ref[...] += lax.dot_general(p, v_ref[...], (((0,), (0,)), ((), ())), preferred_element_type=F32)

    @pl.when(j == pl.num_programs(1) - 1)
    def _():
        h2 = h1_ref[...] + gate_ref[...] * acc_ref[...]
        o_ref[...] = _rms(h2, gf_ref[...])


def _peer(ft, u, v, r2, e2, n1, e1, h1, mod6, g_final, tiles_per_batch_fn):
    d, nt = ft.shape
    ne = u.shape[0]
    nk = r2.shape[1]
    tm = 512
    et = 1024
    rspec = pl.BlockSpec((PEER_HEADS, nk, tm), lambda i, j: (0, 0, i), pipeline_mode=pl.Buffered(1))
    kern = functools.partial(_peer_kernel, nk=nk)
    return pl.pallas_call(
        kern,
        grid=(nt // tm, ne // et),
        in_specs=[pl.BlockSpec((d, tm), lambda i, j: (0, i)),
                  pl.BlockSpec((et, d), lambda i, j: (j, 0)),
                  pl.BlockSpec((et, d), lambda i, j: (j, 0)),
                  rspec, rspec, rspec, rspec,
                  pl.BlockSpec((tm, d), lambda i, j: (i, 0), pipeline_mode=pl.Buffered(1)),
                  pl.BlockSpec((None, 1, d), lambda i, j: (tiles_per_batch_fn(i, tm) * 6 + 5, 0, 0)),
                  pl.BlockSpec((1, d), lambda i, j: (0, 0))],
        out_specs=pl.BlockSpec((tm, d), lambda i, j: (i, 0), pipeline_mode=pl.Buffered(1)),
        out_shape=jax.ShapeDtypeStruct((nt, d), F32),
        scratch_shapes=[pltpu.VMEM((tm, d), F32)],
        compiler_params=_params(("arbitrary", "arbitrary")),
        name="peer",
    )(ft, u, v, r2, e2, n1, e1, h1, mod6, g_final.reshape(1, d))


def kernel(x, c, ctx, c_ctx, w_ada, b_ada, g_norm1, w_in, s5_a_re, s5_a_im, s5_log_dt, s5_b_re, s5_b_im, s5_c_re, s5_c_im, s5_d, s5_w_glu, lru_conv_w, lru_conv_b, lru_lambda, lru_w_r, lru_b_r, lru_w_i, lru_b_i, w_proj_a, w_proj_b, w_out, g_norm2, peer_w_query, peer_sub_keys, peer_u, peer_v, g_final):
    bsz, length, d = x.shape
    assert w_ada.shape[0] == 1, "single-layer kernel"
    rows = length // GRID_W
    w_s5 = s5_d.shape[-1]
    w_lru = lru_lambda.shape[-1]
    assert bsz < 8 and length % (GRID_W * LRU_ROW_BLOCK) == 0 and ctx.shape[1] % LRU_ROW_BLOCK == 0

    cin = jnp.zeros((8, d), F32).at[:bsz].set(c).at[bsz].set(c_ctx)
    mod6 = _ada(cin, w_ada[0], b_ada[0]).reshape(8 * 6, 1, d)

    w_in_b = w_in[0].astype(BF16)
    zu, zv, zg = _inproj(x, g_norm1[0], mod6, lambda b: b, w_in_b, w_s5, w_lru)
    zu_c, zv_c = _inproj(ctx, g_norm1[0], mod6, lambda b: bsz, w_in_b[:, :w_s5 + w_lru], w_s5, w_lru)

    mt, bp, cp, a16 = _s5_operators(s5_a_re[0], s5_a_im[0], s5_log_dt[0], s5_b_re[0], s5_b_im[0],
                                    s5_c_re[0], s5_c_im[0], s5_d[0])
    ys5 = _s5(zu, zu_c, mt, bp, cp, a16)

    lctx = ctx.shape[1]
    vc5 = jnp.zeros((1, lctx, 1, COLS_PER_TILE, w_lru), F32).at[0, :, 0, :bsz].set(zv_c.transpose(1, 0, 2))
    v5 = zv.reshape(bsz, rows, GRID_W // COLS_PER_TILE, COLS_PER_TILE, w_lru)
    zero_h = jnp.zeros((1, 1, w_lru), F32)
    yl = []
    for dr in range(2):
        args = (lru_conv_w[0], lru_conv_b[0], lru_lambda[0, dr], lru_w_r[0, dr], lru_b_r[0, dr],
                lru_w_i[0, dr], lru_b_i[0, dr])
        hc = _lru(vc5, *args, zero_h, rev=dr == 1, chain=False)
        h0 = hc[0, 0, :bsz].reshape(bsz, 1, w_lru)
        yl.append(_lru(v5, *args, h0, rev=dr == 1, chain=True).reshape(bsz, length, w_lru))

    h1, ft = _merge(x, ys5, yl[0], yl[1], zg, mod6, g_norm2[0], s5_w_glu[0].astype(BF16),
                    w_proj_a[0].astype(BF16), w_proj_b[0].astype(BF16), w_out[0].astype(BF16))

    r2, e2, n1, e1 = _route(ft, peer_w_query[0].T.astype(BF16), peer_sub_keys[0].astype(BF16))
    out = _peer(ft, peer_u[0].astype(BF16), peer_v[0].astype(BF16), r2, e2, n1, e1,
                h1.reshape(bsz * length, d), mod6, g_final,
                lambda i, tm: i // (length // tm))
    return out.reshape(bsz, length, d)
```

```python
import functools

import jax
import jax.numpy as jnp
from jax import lax
from jax.experimental import pallas as pl
from jax.experimental.pallas import tpu as pltpu

F32 = jnp.float32
BF16 = jnp.bfloat16
EPS = 1e-6
GRID_W = 64
S5_GROUP = 16
S5_STATE = 64
S5_CHUNK = 16
LRU_HEADS = 16
LRU_C = 8.0
CONV_W = 4
CONV_PAD_LO = (CONV_W - 1) // 2
N_KEYS = 128
PEER_HEADS = 8
PEER_TOPK = 16
COLS_PER_TILE = 8
MXU_TILE = 256
NORM_ROWS = 16
VMEM_LIMIT = 56 * 1024 * 1024


def _params(sem, vmem=VMEM_LIMIT):
    return pltpu.CompilerParams(dimension_semantics=sem, vmem_limit_bytes=vmem)


def _rms(x, g):
    return x * lax.rsqrt(jnp.mean(x * x, axis=-1, keepdims=True) + EPS) * g


def _ada_kernel(c_ref, w_ref, b_ref, o_ref):
    c = c_ref[...]
    sc = c * jax.nn.sigmoid(c)
    o_ref[...] = jnp.dot(sc.astype(BF16), w_ref[...].astype(BF16),
                         preferred_element_type=F32) + b_ref[...]


def _ada(cin, w, b):
    d, n = w.shape
    tn = 1536
    return pl.pallas_call(
        _ada_kernel,
        grid=(n // tn,),
        in_specs=[pl.BlockSpec((8, d), lambda j: (0, 0)),
                  pl.BlockSpec((d, tn), lambda j: (0, j)),
                  pl.BlockSpec((1, tn), lambda j: (0, j))],
        out_specs=pl.BlockSpec((8, tn), lambda j: (0, j)),
        out_shape=jax.ShapeDtypeStruct((8, n), F32),
        compiler_params=_params(("arbitrary",)),
        name="ada",
    )(cin, w, b.reshape(1, n))


def _inproj_kernel(x_ref, g_ref, sh_ref, sc_ref, w_ref, *refs, n_u, n_v):
    if len(refs) == 4:
        zu_ref, zv_ref, zg_ref, n_scr = refs
    else:
        zu_ref, zv_ref, n_scr = refs
        zg_ref = None
    j = pl.program_id(2)

    @pl.when(j == 0)
    def _():
        g, sh, scale1 = g_ref[...], sh_ref[...], 1.0 + sc_ref[...]
        for r in range(0, x_ref.shape[0], NORM_ROWS):
            rows = slice(r, r + NORM_ROWS)
            n_scr[rows, :] = (_rms(x_ref[rows, :], g) * scale1 + sh).astype(BF16)

    def project(out_ref):
        out_ref[...] = jnp.dot(n_scr[...], w_ref[...], preferred_element_type=F32).astype(out_ref.dtype)

    pl.when(j < n_u)(lambda: project(zu_ref))
    pl.when((j >= n_u) & (j < n_u + n_v))(lambda: project(zv_ref))
    if zg_ref is not None:
        pl.when(j >= n_u + n_v)(lambda: project(zg_ref))


def _inproj(x, g, mod6, mod_row, w_bf16, width_u, width_v):
    bsz, length, d = x.shape
    n = w_bf16.shape[1]
    tm = min(1024, length)
    tn = 512
    n_u, n_v = width_u // tn, width_v // tn
    n_g = n // tn - n_u - n_v
    kern = functools.partial(_inproj_kernel, n_u=n_u, n_v=n_v)
    row = lambda k: (lambda b, i, j: (mod_row(b) * 6 + k, 0, 0))
    out_specs = [pl.BlockSpec((None, tm, tn), lambda b, i, j: (b, i, jnp.minimum(j, n_u - 1))),
                 pl.BlockSpec((None, tm, tn), lambda b, i, j: (b, i, jnp.clip(j - n_u, 0, n_v - 1)))]
    out_shape = [jax.ShapeDtypeStruct((bsz, length, width_u), F32),
                 jax.ShapeDtypeStruct((bsz, length, width_v), F32)]
    if n_g:
        out_specs.append(pl.BlockSpec((None, tm, tn), lambda b, i, j: (b, i, jnp.maximum(j - n_u - n_v, 0))))
        out_shape.append(jax.ShapeDtypeStruct((bsz, length, n_g * tn), BF16))
    return pl.pallas_call(
        kern,
        grid=(bsz, length // tm, n // tn),
        in_specs=[pl.BlockSpec((None, tm, d), lambda b, i, j: (b, i, 0)),
                  pl.BlockSpec((1, d), lambda b, i, j: (0, 0)),
                  pl.BlockSpec((None, 1, d), row(0)),
                  pl.BlockSpec((None, 1, d), row(1)),
                  pl.BlockSpec((d, tn), lambda b, i, j: (0, j))],
        out_specs=out_specs,
        out_shape=out_shape,
        scratch_shapes=[pltpu.VMEM((tm, d), BF16)],
        compiler_params=_params(("arbitrary", "arbitrary", "arbitrary")),
        name="inproj",
    )(x, g.reshape(1, d), mod6, mod6, w_bf16)


S5_PREP_BATCH = 8


def _s5_prep_kernel(*refs):
    for i in range(S5_PREP_BATCH):
        _s5_prep_one(*[r.at[i] for r in refs])


def _s5_prep_one(are_ref, aim_ref, ldt_ref, bre_ref, bim_ref, cre_ref, cim_ref,
                 k_ref, cg_ref, bg_ref, a16_ref):
    nk = S5_CHUNK + 1
    dt = jnp.exp(ldt_ref[...])
    lre, lim = are_ref[...], aim_ref[...]
    xr, xi = lre * dt, lim * dt
    e1 = jnp.exp(xr)
    ar, ai = e1 * jnp.cos(xi), e1 * jnp.sin(xi)
    pows = [(jnp.ones_like(ar), jnp.zeros_like(ai))]
    for _ in range(nk - 1):
        kr, ki = pows[-1]
        pows.append((kr * ar - ki * ai, kr * ai + ki * ar))
    rep = lambda rows: jnp.concatenate([jnp.broadcast_to(r, (S5_GROUP, r.shape[1])) for r in rows], axis=0)
    pr, pim = rep([q[0] for q in pows]), rep([q[1] for q in pows])
    den = lre * lre + lim * lim
    qr = ((ar - 1.0) * lre + ai * lim) / den
    qi = (ai * lre - (ar - 1.0) * lim) / den
    bre, bim = bre_ref[...], bim_ref[...]
    bbr, bbi = qr * bre - qi * bim, qr * bim + qi * bre
    tile = lambda m: jnp.concatenate([m] * nk, axis=0)
    cr_t, ci_t, br_t, bi_t = tile(cre_ref[...]), tile(cim_ref[...]), tile(bbr), tile(bbi)
    cg = jnp.concatenate([cr_t * pr - ci_t * pim, -(cr_t * pim + ci_t * pr)], axis=1)
    bg = jnp.concatenate([br_t * pr - bi_t * pim, br_t * pim + bi_t * pr], axis=1)
    bcat = jnp.concatenate([bbr, bbi], axis=1)
    cg_ref[...] = cg
    bg_ref[...] = bg
    k_ref[...] = lax.dot_general(cg[:S5_CHUNK * S5_GROUP], bcat, (((1,), (1,)), ((), ())),
                                 precision=lax.Precision.HIGHEST, preferred_element_type=F32)
    n = S5_CHUNK * S5_GROUP
    a16_ref[...] = jnp.concatenate([pr[n:n + 1], pim[n:n + 1]], axis=1)


def _s5_operators(a_re, a_im, log_dt, b_re, b_im, c_re, c_im, d_skip):
    _, groups, p = a_re.shape
    h = S5_GROUP
    dg = 2 * groups
    nk = S5_CHUNK + 1
    vec = lambda t: t.reshape(dg, 1, p)
    tr = lambda t: jnp.swapaxes(t, -1, -2).reshape(dg, h, p)
    ldt = jnp.broadcast_to(log_dt[..., None], (2, groups, p))
    nb = S5_PREP_BATCH
    batch = lambda *tail: pl.BlockSpec((nb,) + tail, lambda i: (i,) + (0,) * len(tail))
    spec_v, spec_m = batch(1, p), batch(h, p)
    kfl, cg, bg, a16 = pl.pallas_call(
        _s5_prep_kernel,
        grid=(dg // nb,),
        in_specs=[spec_v, spec_v, spec_v, spec_m, spec_m, spec_m, spec_m],
        out_specs=[batch(S5_CHUNK * h, h), batch(nk * h, 2 * p), batch(nk * h, 2 * p), batch(1, 2 * p)],
        out_shape=[jax.ShapeDtypeStruct((dg, S5_CHUNK * h, h), F32),
                   jax.ShapeDtypeStruct((dg, nk * h, 2 * p), F32),
                   jax.ShapeDtypeStruct((dg, nk * h, 2 * p), F32),
                   jax.ShapeDtypeStruct((dg, 1, 2 * p), F32)],
        compiler_params=_params(("arbitrary",)),
        name="s5_prep",
    )(vec(a_re), vec(a_im), vec(ldt), tr(b_re), tr(b_im),
      c_re.reshape(dg, h, p), c_im.reshape(dg, h, p))

    t = S5_CHUNK
    kfl = kfl.reshape(2, groups, t, h, h)
    cg = cg.reshape(2, groups, nk, h, 2 * p)
    bg = bg.reshape(2, groups, nk, h, 2 * p)
    ti = jnp.arange(t)
    lag = ti[:, None] - ti[None, :]
    kf = jnp.where((lag >= 0)[None, :, :, None, None], kfl[0][:, jnp.clip(lag, 0, t - 1)], 0.0)
    kb = jnp.where((lag <= 0)[None, :, :, None, None], kfl[1][:, jnp.clip(-lag, 0, t - 1)], 0.0)
    eye = (lag == 0)[None, :, :, None, None] * jnp.eye(h, dtype=F32)[None, None, None]
    m = kf + kb + eye * d_skip.reshape(groups, 1, 1, 1, h)
    mt = m.transpose(0, 2, 4, 1, 3).reshape(groups, t * h, t * h)
    bp_f = bg[0][:, t - 1 - ti].reshape(groups, t * h, 2 * p)
    bp_b = bg[1][:, ti].reshape(groups, t * h, 2 * p)
    bp = jnp.concatenate([bp_f, bp_b], axis=-1)
    cp_f = cg[0][:, ti + 1].reshape(groups, t * h, 2 * p)
    cp_b = cg[1][:, t - ti].reshape(groups, t * h, 2 * p)
    cp = jnp.concatenate([cp_f, cp_b], axis=-1).swapaxes(1, 2)
    return mt.astype(BF16), bp.astype(BF16), cp.astype(BF16), a16.reshape(2, groups, 1, 2 * p)


def _cplx_coef(a):
    p = a.shape[1] // 2
    lane = lax.broadcasted_iota(jnp.int32, a.shape, 1)
    sw = pltpu.roll(a, p, axis=1)
    return jnp.where(lane < p, a, sw), jnp.where(lane < p, -sw, a)


def _cplx_mul(a, x):
    c1, c2 = _cplx_coef(a)
    return c1 * x + c2 * pltpu.roll(x, x.shape[1] // 2, axis=1)


def _chunk_scan(x, a, reverse):
    n = x.shape[0]
    row = lax.broadcasted_iota(jnp.int32, (n, 1), 0)
    o = 1
    while o < n:
        if reverse:
            sh = jnp.where(row < n - o, pltpu.roll(x, n - o, axis=0), 0.0)
        else:
            sh = jnp.where(row >= o, pltpu.roll(x, o, axis=0), 0.0)
        x = x + _cplx_mul(a, sh)
        a = _cplx_mul(a, a)
        o *= 2
    return x


S5_SUPER = 16
S5_IN_FLIGHT = 4


def _entering_states(v_ref, sin_ref, a, h0, reverse):
    k_n = S5_SUPER
    ns = v_ref.shape[0] // k_n
    half = a.shape[1] // 2
    order = list(range(k_n - 1, -1, -1)) if reverse else list(range(k_n))
    rows_at = lambda k: pl.ds(k, ns, stride=k_n)
    pows = [None, a]
    for _ in range(k_n - 1):
        pows.append(_cplx_mul(a, pows[-1]))
    c1, c2 = _cplx_coef(a)
    loc = {}
    s = ss = None
    for k in order:
        x = v_ref[rows_at(k), :]
        xs = pltpu.roll(x, half, axis=1)
        s, ss = (x, xs) if s is None else (c1 * s + c2 * ss + x, c1 * ss - c2 * s + xs)
        loc[k] = s
    row = lax.broadcasted_iota(jnp.int32, (ns, 1), 0)
    edge = ns - 1 if reverse else 0
    ends = loc[order[-1]] + jnp.where(row == edge, _cplx_mul(pows[k_n], h0), 0.0)
    true_ends = _chunk_scan(ends, pows[k_n], reverse)
    cin = jnp.where(row == edge, h0, pltpu.roll(true_ends, ns - 1 if reverse else 1, axis=0))
    sin_ref[rows_at(order[0]), :] = cin
    for i in range(1, k_n):
        sin_ref[rows_at(order[i]), :] = loc[order[i - 1]] + _cplx_mul(pows[i], cin)


S5_GPB = 128 // S5_GROUP
S5_TPT = 128 // S5_GROUP


def _chunk_perm():
    n = S5_TPT * S5_GPB * S5_GROUP
    i = jnp.arange(n)
    t, g, h = i // (S5_GPB * S5_GROUP), (i // S5_GROUP) % S5_GPB, i % S5_GROUP
    dst = g * (S5_TPT * S5_GROUP) + t * S5_GROUP + h
    return jnp.zeros((n, n), BF16).at[i, dst].set(1.0)


def _load_chunks(src_ref, perm, dst_scr):
    nc = src_ref.shape[0] // S5_CHUNK
    halves = []
    for r in range(S5_CHUNK // S5_TPT):
        xcat = jnp.concatenate(
            [src_ref[pl.ds(r * S5_TPT + t, nc, stride=S5_CHUNK), :].astype(BF16) for t in range(S5_TPT)], axis=1)
        halves.append(jnp.dot(xcat, perm, preferred_element_type=F32).astype(BF16))
    for g in range(S5_GPB):
        dst_scr[g] = jnp.concatenate([hv[:, g * 128:(g + 1) * 128] for hv in halves], axis=1)


def _s5_kernel(u_ref, uc_ref, perm_ref, permt_ref, mt_ref, bp_ref, cp_ref, af_ref, ab_ref, y_ref,
               u_scr, uc_scr, y_scr, v_scr, sin_scr):
    nc = u_ref.shape[0] // S5_CHUNK
    _load_chunks(u_ref, perm_ref[...], u_scr)
    _load_chunks(uc_ref, perm_ref[...], uc_scr)

    def group(g, _):
        y_scr[g] = _s5_group(u_scr[g], uc_scr[g], mt_ref[g], bp_ref[g], cp_ref[g], af_ref[g], ab_ref[g],
                             v_scr.at[g % S5_IN_FLIGHT], sin_scr.at[g % S5_IN_FLIGHT])
        return 0

    lax.fori_loop(0, S5_GPB, group, 0, unroll=S5_IN_FLIGHT)

    for r in range(S5_CHUNK // S5_TPT):
        ycat = jnp.concatenate([y_scr[g][:, r * 128:(r + 1) * 128] for g in range(S5_GPB)], axis=1)
        back = jnp.dot(ycat, permt_ref[...], preferred_element_type=F32)
        for t in range(S5_TPT):
            y_ref[pl.ds(r * S5_TPT + t, nc, stride=S5_CHUNK), :] = back[:, t * 128:(t + 1) * 128]


def _s5_group(u, uc, mt, bp, cp, af, ab, v_scr, sin_scr):
    ps = af.shape[1]
    v = jnp.dot(u, bp, preferred_element_type=F32)
    v_scr[0], v_scr[1] = v[:, :ps], v[:, ps:]
    vc = jnp.dot(uc, bp, preferred_element_type=F32)
    ncc = vc.shape[0]
    hcf = _chunk_scan(vc[:, :ps], af, False)[ncc - 1:ncc]
    hcb = _chunk_scan(vc[:, ps:], ab, True)[0:1]
    _entering_states(v_scr.at[0], sin_scr.at[0], af, hcf, False)
    _entering_states(v_scr.at[1], sin_scr.at[1], ab, hcb, True)
    s_in = jnp.concatenate([sin_scr[0], sin_scr[1]], axis=1).astype(BF16)
    y = (jnp.dot(u, mt, preferred_element_type=F32)
         + jnp.dot(s_in, cp, preferred_element_type=F32))
    return y.astype(BF16)


def _s5(zu, zu_c, mt, bp, cp, a16):
    bsz, length, w = zu.shape
    lctx = zu_c.shape[1]
    nc, ncc = length // S5_CHUNK, lctx // S5_CHUNK
    cw = S5_CHUNK * S5_GROUP
    ps = a16.shape[-1]
    perm = _chunk_perm()
    once = lambda shape, imap: pl.BlockSpec(shape, imap, pipeline_mode=pl.Buffered(1))
    wspec = pl.BlockSpec((S5_GPB, cw, cw), lambda b, q: (q, 0, 0))
    return pl.pallas_call(
        _s5_kernel,
        grid=(bsz, w // 128),
        in_specs=[once((None, length, 128), lambda b, q: (b, 0, q)),
                  pl.BlockSpec((None, lctx, 128), lambda b, q: (b, 0, q)),
                  once(perm.shape, lambda b, q: (0, 0)),
                  once(perm.shape, lambda b, q: (0, 0)),
                  wspec, wspec, wspec,
                  pl.BlockSpec((None, S5_GPB, 1, ps), lambda b, q: (0, q, 0, 0)),
                  pl.BlockSpec((None, S5_GPB, 1, ps), lambda b, q: (1, q, 0, 0))],
        out_specs=once((None, length, 128), lambda b, q: (b, 0, q)),
        out_shape=jax.ShapeDtypeStruct((bsz, length, w), F32),
        scratch_shapes=[pltpu.VMEM((S5_GPB, nc, cw), BF16),
                        pltpu.VMEM((S5_GPB, ncc, cw), BF16),
                        pltpu.VMEM((S5_GPB, nc, cw), BF16),
                        pltpu.VMEM((S5_IN_FLIGHT, 2, nc, ps), F32),
                        pltpu.VMEM((S5_IN_FLIGHT, 2, nc, ps), F32)],
        compiler_params=_params(("arbitrary", "arbitrary")),
        name="s5",
    )(zu, zu_c, perm, perm.T, mt, bp, cp, a16, a16)


LRU_ROW_BLOCK = 16


def _lru_kernel(v_ref, cw_ref, cb_ref, wr_ref, br_ref, wi_ref, bi_ref, lam_ref, h0_ref, o_ref,
                vpad, a_scr, b_scr, carry_scr, *, rows, rev, chain):
    ct = v_ref.shape[-1]
    cpt = COLS_PER_TILE
    rb = LRU_ROW_BLOCK
    nblk = rows // rb
    nslab = ct // MXU_TILE

    if chain:
        @pl.when(pl.program_id(2) == 0)
        def _():
            carry_scr[...] = h0_ref[...]

    zero_row = jnp.zeros((cpt, ct), F32)
    for k in range(CONV_PAD_LO):
        vpad[k] = zero_row
    for k in range(CONV_W - 1 - CONV_PAD_LO):
        vpad[CONV_PAD_LO + rows + k] = zero_row

    def copy(i, _):
        r0 = pl.multiple_of(i * rb, rb)
        vpad[pl.ds(r0 + CONV_PAD_LO, rb)] = v_ref[pl.ds(r0, rb)]
        return 0

    lax.fori_loop(0, nblk, copy, 0)

    cw = cw_ref[...]
    cb = cb_ref[...]
    nl = -lam_ref[...]
    softplus = jnp.maximum(nl, 0.0) + jnp.log1p(jnp.exp(-jnp.abs(nl)))
    c8 = -LRU_C * softplus
    b_r, b_i = br_ref[...], bi_ref[...]

    def block(i, carry):
        acc_a, acc_h = carry
        bi_ = (nblk - 1 - i) if rev else i
        r0 = pl.multiple_of(bi_ * rb, rb)
        x = cb
        for k in range(CONV_W):
            x = x + cw[k:k + 1] * vpad[pl.ds(r0 + k, rb)].reshape(rb * cpt, ct)
        xb = x.astype(BF16)
        pre_r = jnp.concatenate(
            [jnp.dot(xb[:, s * MXU_TILE:(s + 1) * MXU_TILE], wr_ref[s], preferred_element_type=F32)
             for s in range(nslab)], axis=1)
        pre_i = jnp.concatenate(
            [jnp.dot(xb[:, s * MXU_TILE:(s + 1) * MXU_TILE], wi_ref[s], preferred_element_type=F32)
             for s in range(nslab)], axis=1)
        r = jax.nn.sigmoid(pre_r + b_r)
        ig = jax.nn.sigmoid(pre_i + b_i)
        log_a = c8 * r
        a = jnp.exp(log_a)
        z = 1.0 - a * a
        bx = jnp.where(z > 0.0, z * lax.rsqrt(z), 0.0) * (ig * x)
        q0 = pl.multiple_of(r0 * cpt, rb * cpt)
        a_scr[pl.ds(q0, rb * cpt), :] = a
        b_scr[pl.ds(q0, rb * cpt), :] = bx
        order = range(rb - 1, -1, -1) if rev else range(rb)
        for j in order:
            aj = a[j * cpt:(j + 1) * cpt]
            acc_h = aj * acc_h + bx[j * cpt:(j + 1) * cpt]
            acc_a = acc_a * aj
        return acc_a, acc_h

    acc_a, acc_h = lax.fori_loop(0, nblk, block, (jnp.ones((cpt, ct), F32), jnp.zeros((cpt, ct), F32)))

    if not chain:
        o_ref[...] = acc_h
        return

    sub = lax.broadcasted_iota(jnp.int32, (cpt, ct), 0)
    carry = carry_scr[...]
    h_in = jnp.zeros((cpt, ct), F32)
    for s in (range(cpt - 1, -1, -1) if rev else range(cpt)):
        h_in = jnp.where(sub == s, carry, h_in)
        carry = acc_a[s:s + 1] * carry + acc_h[s:s + 1]
    carry_scr[...] = carry

    def row(i, h):
        r = (rows - 1 - i) if rev else i
        q = pl.multiple_of(r * cpt, cpt)
        h = a_scr[pl.ds(q, cpt), :] * h + b_scr[pl.ds(q, cpt), :]
        o_ref[r] = h
        return h

    lax.fori_loop(0, rows, row, h_in, unroll=8)


def _block_diag(w, per):
    hh, n, _ = w.shape
    eye = jnp.eye(per, dtype=w.dtype)
    return jnp.einsum('gpij,pq->gpiqj', w.reshape(hh // per, per, n, n), eye).reshape(hh // per, per * n, per * n)


def _lru(v5, conv_w, conv_b, lam, w_r, b_r, w_i, b_i, h0, *, rev, chain):
    bsz, rows, ncg, cpt, w = v5.shape
    ct = 512
    nct = w // ct
    per = MXU_TILE // (w // LRU_HEADS)
    wr = _block_diag(w_r, per).astype(BF16)
    wi = _block_diag(w_i, per).astype(BF16)
    nslab = ct // MXU_TILE
    cgi = (lambda c: ncg - 1 - c) if rev else (lambda c: c)
    vec = lambda n: pl.BlockSpec((n, ct), lambda b, k, c: (0, k))
    in_specs = [pl.BlockSpec((None, rows, None, cpt, ct), lambda b, k, c: (b, 0, cgi(c), 0, k)),
                vec(CONV_W), vec(1),
                pl.BlockSpec((nslab, MXU_TILE, MXU_TILE), lambda b, k, c: (k, 0, 0)), vec(1),
                pl.BlockSpec((nslab, MXU_TILE, MXU_TILE), lambda b, k, c: (k, 0, 0)), vec(1),
                vec(1),
                pl.BlockSpec((None, 1, ct), lambda b, k, c: (b, 0, k))]
    if chain:
        out_spec = pl.BlockSpec((None, rows, None, cpt, ct), lambda b, k, c: (b, 0, cgi(c), 0, k))
        out_shape = jax.ShapeDtypeStruct(v5.shape, F32)
    else:
        out_spec = pl.BlockSpec((None, None, cpt, ct), lambda b, k, c: (b, c, 0, k))
        out_shape = jax.ShapeDtypeStruct((bsz, ncg, cpt, w), F32)
    kern = functools.partial(_lru_kernel, rows=rows, rev=rev, chain=chain)
    return pl.pallas_call(
        kern,
        grid=(bsz, nct, ncg),
        in_specs=in_specs,
        out_specs=out_spec,
        out_shape=out_shape,
        scratch_shapes=[pltpu.VMEM((rows + CONV_W - 1, cpt, ct), F32),
                        pltpu.VMEM((rows * cpt, ct), F32),
                        pltpu.VMEM((rows * cpt, ct), F32),
                        pltpu.VMEM((1, ct), F32)],
        compiler_params=_params(("arbitrary", "arbitrary", "arbitrary")),
        name="lru_rev" if rev else "lru_fwd",
    )(v5, conv_w, conv_b.reshape(1, w), wr, b_r.reshape(1, w), wi, b_i.reshape(1, w),
      lam.reshape(1, w), h0)


def _merge_kernel(x_ref, ys_ref, ylf_ref, ylb_ref, zg_ref, gate_ref, sh_ref, sc_ref, g2_ref,
                  wglu_ref, wa_ref, wb_ref, wo_ref, h1_ref, ft_ref, *, w_lru, d):
    za = jax.nn.gelu(ys_ref[...])
    ya = (za * jax.nn.sigmoid(jnp.dot(za.astype(BF16), wglu_ref[...],
                                      preferred_element_type=F32))).astype(BF16)
    zg = zg_ref[...]
    yb = ((ylf_ref[...] + ylb_ref[...]) * jax.nn.gelu(zg[:, :w_lru].astype(F32))).astype(BF16)
    ga = jax.nn.sigmoid(zg[:, w_lru:w_lru + d].astype(F32))
    gb = jax.nn.sigmoid(zg[:, w_lru + d:].astype(F32))
    m = (ga * jnp.dot(ya, wa_ref[...], preferred_element_type=F32)
         + gb * jnp.dot(yb, wb_ref[...], preferred_element_type=F32))
    o = jnp.dot(m.astype(BF16), wo_ref[...], preferred_element_type=F32)
    h1 = x_ref[...] + gate_ref[...] * o
    h1_ref[...] = h1
    f = _rms(h1, g2_ref[...]) * (1.0 + sc_ref[...]) + sh_ref[...]
    ft_ref[...] = f.T.astype(BF16)


def _merge(x, ys5, ylf, ylb, zg, mod6, g2, w_glu, w_a, w_b, w_o):
    bsz, length, d = x.shape
    w_s5 = ys5.shape[-1]
    w_lru = ylf.shape[-1]
    tm = 256
    nt = length // tm
    row = lambda k: (lambda b, i: (b * 6 + k, 0, 0))
    tok = lambda w: pl.BlockSpec((None, tm, w), lambda b, i: (b, i, 0))
    const = lambda shape: pl.BlockSpec(shape, lambda b, i: (0, 0), pipeline_mode=pl.Buffered(1))
    kern = functools.partial(_merge_kernel, w_lru=w_lru, d=d)
    return pl.pallas_call(
        kern,
        grid=(bsz, nt),
        in_specs=[tok(d), tok(w_s5), tok(w_lru), tok(w_lru), tok(zg.shape[-1]),
                  pl.BlockSpec((None, 1, d), row(2)), pl.BlockSpec((None, 1, d), row(3)),
                  pl.BlockSpec((None, 1, d), row(4)),
                  pl.BlockSpec((1, d), lambda b, i: (0, 0)),
                  const(w_glu.shape), const(w_a.shape), const(w_b.shape), const(w_o.shape)],
        out_specs=[tok(d), pl.BlockSpec((d, tm), lambda b, i: (0, b * nt + i))],
        out_shape=[jax.ShapeDtypeStruct((bsz, length, d), F32),
                   jax.ShapeDtypeStruct((d, bsz * length), BF16)],
        compiler_params=_params(("arbitrary", "arbitrary")),
        name="merge",
    )(x, ys5, ylf, ylb, zg, mod6, mod6, mod6, g2.reshape(1, d), w_glu, w_a, w_b, w_o)


def _sort_pairs(n):
    out = []
    p = 1
    while p < n:
        k = p
        while k >= 1:
            for j in range(k % p, n - k, 2 * k):
                for i in range(min(k, n - j - k)):
                    if (i + j) // (2 * p) == (i + j + k) // (2 * p):
                        out.append((i + j, i + j + k))
            k //= 2
        p *= 2
    return out


def _top_rows_sorted(x, k):
    n_slab = x.shape[0] // 8
    slabs = [x[8 * r:8 * r + 8] for r in range(n_slab)]
    for a, b in _sort_pairs(n_slab):
        slabs[a], slabs[b] = jnp.maximum(slabs[a], slabs[b]), jnp.minimum(slabs[a], slabs[b])
    slabs.append(jnp.full_like(slabs[0], -jnp.inf))
    out = []
    for it in range(k):
        m = jnp.max(slabs[0], axis=0, keepdims=True)
        out.append(m)
        hit = slabs[0] == m
        for r in range(min(k - 1 - it, n_slab)):
            slabs[r] = jnp.where(hit, slabs[r + 1], slabs[r])
    return out


def _count_above(rows, x, strict):
    assert PEER_TOPK == 16
    above = (lambda a: a > x) if strict else (lambda a: a >= x)
    sel = jnp.where
    b3 = above(rows[7])
    b2 = above(sel(b3, rows[11], rows[3]))
    b1 = above(sel(b3, sel(b2, rows[13], rows[9]), sel(b2, rows[5], rows[1])))
    b0 = above(sel(b3, sel(b2, sel(b1, rows[14], rows[12]), sel(b1, rows[10], rows[8])),
                   sel(b2, sel(b1, rows[6], rows[4]), sel(b1, rows[2], rows[0]))))
    return (sel(b3, 8.0, 0.0) + sel(b2, 4.0, 0.0) + sel(b1, 2.0, 0.0) + sel(b0, 1.0, 0.0)
            + sel(above(rows[15]), 1.0, 0.0))


_CAND_PAIRS = [(j, k) for j in range(PEER_TOPK + 1) for k in range(PEER_TOPK + 1)
               if (j + 1) * (k + 1) <= PEER_TOPK + 1]
_CAND_ROWS = 64
assert len(_CAND_PAIRS) <= _CAND_ROWS


def _route_kernel(ft_ref, wq_ref, sk_ref, r2_ref, e2_ref, n1_ref, e1_ref, cand_scr):
    half = sk_ref.shape[2]
    tm = ft_ref.shape[1]
    qt = jnp.dot(wq_ref[...], ft_ref[...], preferred_element_type=F32).astype(BF16)
    for r in range(len(_CAND_PAIRS), _CAND_ROWS):
        cand_scr[r:r + 1, :] = jnp.full((1, 128), -jnp.inf, F32)
    for h in range(PEER_HEADS):
        s_full = [jnp.dot(sk_ref[side], qt[(2 * h + side) * half:(2 * h + side + 1) * half],
                          preferred_element_type=F32) for side in range(2)]
        for l in range(tm // 128):
            lanes = slice(l * 128, (l + 1) * 128)
            s = [sf[:, lanes] for sf in s_full]
            top0 = _top_rows_sorted(s[0], PEER_TOPK + 1)
            top1 = _top_rows_sorted(s[1], PEER_TOPK + 1)
            for r, (j, k) in enumerate(_CAND_PAIRS):
                cand_scr[r:r + 1, :] = top0[j] + top1[k]
            best = _top_rows_sorted(cand_scr[...], PEER_TOPK + 1)
            z = jnp.ones_like(best[0])
            for k in range(1, PEER_TOPK):
                z = z + jnp.exp(best[k] - best[0])
            t1 = 0.5 * (best[PEER_TOPK - 1] + best[PEER_TOPK]) - s[0]
            n1 = _count_above(top1, t1, strict=False)
            rank2 = _count_above(top1, s[1], strict=True)
            r2_ref[h, :, lanes] = rank2.astype(BF16)
            e2_ref[h, :, lanes] = jnp.exp(s[1] - top1[0]).astype(BF16)
            n1_ref[h, :, lanes] = n1
            e1_ref[h, :, lanes] = jnp.exp(s[0] - top0[0]) / z


def _route(ft, wq_t, sk):
    d, nt = ft.shape
    nk = sk.shape[1]
    tm = 256
    ospec = pl.BlockSpec((PEER_HEADS, nk, tm), lambda i: (0, 0, i))
    return pl.pallas_call(
        _route_kernel,
        grid=(nt // tm,),
        in_specs=[pl.BlockSpec((d, tm), lambda i: (0, i)),
                  pl.BlockSpec(wq_t.shape, lambda i: (0, 0), pipeline_mode=pl.Buffered(1)),
                  pl.BlockSpec(sk.shape, lambda i: (0, 0, 0))],
        out_specs=[ospec] * 4,
        out_shape=[jax.ShapeDtypeStruct((PEER_HEADS, nk, nt), dt) for dt in (BF16, BF16, F32, F32)],
        scratch_shapes=[pltpu.VMEM((_CAND_ROWS, 128), F32)],
        compiler_params=_params(("arbitrary",)),
        name="route",
    )(ft, wq_t, sk)


def _rows_bf16(row, n):
    tile = jnp.broadcast_to(row, (16, row.shape[1])).astype(BF16)
    return jnp.concatenate([tile] * (n // 16), axis=0)


def _peer_kernel(ft_ref, u_ref, v_ref, r2_ref, e2_ref, n1_ref, e1_ref, h1_ref, gate_ref, gf_ref,
                 o_ref, acc_ref, *, nk):
    j = pl.program_id(1)

    @pl.when(j == 0)
    def _():
        acc_ref[...] = jnp.zeros_like(acc_ref)

    tm = ft_ref.shape[1]
    n_i1 = u_ref.shape[0] // nk
    per = MXU_TILE // nk
    ps = []
    for c in range(n_i1 // per):
        st = jnp.dot(u_ref[c * MXU_TILE:(c + 1) * MXU_TILE, :], ft_ref[...],
                     preferred_element_type=F32)
        for a in range(per):
            i1 = j * n_i1 + c * per + a
            act = jax.nn.gelu(st[a * nk:(a + 1) * nk].astype(BF16))
            w = jnp.zeros((nk, tm), BF16)
            for h in range(PEER_HEADS):
                cnt = _rows_bf16(n1_ref[h, pl.ds(i1, 1), :], nk)
                cf = _rows_bf16(e1_ref[h, pl.ds(i1, 1), :], nk)
                w = w + jnp.where(r2_ref[h] < cnt, e2_ref[h] * cf, jnp.zeros((), BF16))
            ps.append(w * act)
    p = jnp.concatenate(ps, axis=0)
    acc_ref[...] += lax.dot_general(p, v_ref[...], (((0,), (0,)), ((), ())), preferred_element_type=F32)

    @pl.when(j == pl.num_programs(1) - 1)
    def _():
        h2 = h1_ref[...] + gate_ref[...] * acc_ref[...]
        o_ref[...] = _rms(h2, gf_ref[...])


def _peer(ft, u, v, r2, e2, n1, e1, h1, mod6, g_final, tiles_per_batch_fn):
    d, nt = ft.shape
    ne = u.shape[0]
    nk = r2.shape[1]
    tm = 512
    et = 1024
    rspec = pl.BlockSpec((PEER_HEADS, nk, tm), lambda i, j: (0, 0, i), pipeline_mode=pl.Buffered(1))
    kern = functools.partial(_peer_kernel, nk=nk)
    return pl.pallas_call(
        kern,
        grid=(nt // tm, ne // et),
        in_specs=[pl.BlockSpec((d, tm), lambda i, j: (0, i)),
                  pl.BlockSpec((et, d), lambda i, j: (j, 0)),
                  pl.BlockSpec((et, d), lambda i, j: (j, 0)),
                  rspec, rspec, rspec, rspec,
                  pl.BlockSpec((tm, d), lambda i, j: (i, 0), pipeline_mode=pl.Buffered(1)),
                  pl.BlockSpec((None, 1, d), lambda i, j: (tiles_per_batch_fn(i, tm) * 6 + 5, 0, 0)),
                  pl.BlockSpec((1, d), lambda i, j: (0, 0))],
        out_specs=pl.BlockSpec((tm, d), lambda i, j: (i, 0), pipeline_mode=pl.Buffered(1)),
        out_shape=jax.ShapeDtypeStruct((nt, d), F32),
        scratch_shapes=[pltpu.VMEM((tm, d), F32)],
        compiler_params=_params(("arbitrary", "arbitrary")),
        name="peer",
    )(ft, u, v, r2, e2, n1, e1, h1, mod6, g_final.reshape(1, d))


def kernel(x, c, ctx, c_ctx, w_ada, b_ada, g_norm1, w_in, s5_a_re, s5_a_im, s5_log_dt, s5_b_re, s5_b_im, s5_c_re, s5_c_im, s5_d, s5_w_glu, lru_conv_w, lru_conv_b, lru_lambda, lru_w_r, lru_b_r, lru_w_i, lru_b_i, w_proj_a, w_proj_b, w_out, g_norm2, peer_w_query, peer_sub_keys, peer_u, peer_v, g_final):
    bsz, length, d = x.shape
    assert w_ada.shape[0] == 1, "single-layer kernel"
    rows = length // GRID_W
    w_s5 = s5_d.shape[-1]
    w_lru = lru_lambda.shape[-1]
    assert bsz < 8 and length % (GRID_W * LRU_ROW_BLOCK) == 0 and ctx.shape[1] % LRU_ROW_BLOCK == 0

    cin = jnp.zeros((8, d), F32).at[:bsz].set(c).at[bsz].set(c_ctx)
    mod6 = _ada(cin, w_ada[0], b_ada[0]).reshape(8 * 6, 1, d)

    w_in_b = w_in[0].astype(BF16)
    zu, zv, zg = _inproj(x, g_norm1[0], mod6, lambda b: b, w_in_b, w_s5, w_lru)
    zu_c, zv_c = _inproj(ctx, g_norm1[0], mod6, lambda b: bsz, w_in_b[:, :w_s5 + w_lru], w_s5, w_lru)

    mt, bp, cp, a16 = _s5_operators(s5_a_re[0], s5_a_im[0], s5_log_dt[0], s5_b_re[0], s5_b_im[0],
                                    s5_c_re[0], s5_c_im[0], s5_d[0])
    ys5 = _s5(zu, zu_c, mt, bp, cp, a16)

    lctx = ctx.shape[1]
    vc5 = jnp.zeros((1, lctx, 1, COLS_PER_TILE, w_lru), F32).at[0, :, 0, :bsz].set(zv_c.transpose(1, 0, 2))
    v5 = zv.reshape(bsz, rows, GRID_W // COLS_PER_TILE, COLS_PER_TILE, w_lru)
    zero_h = jnp.zeros((1, 1, w_lru), F32)
    yl = []
    for dr in range(2):
        args = (lru_conv_w[0], lru_conv_b[0], lru_lambda[0, dr], lru_w_r[0, dr], lru_b_r[0, dr],
                lru_w_i[0, dr], lru_b_i[0, dr])
        hc = _lru(vc5, *args, zero_h, rev=dr == 1, chain=False)
        h0 = hc[0, 0, :bsz].reshape(bsz, 1, w_lru)
        yl.append(_lru(v5, *args, h0, rev=dr == 1, chain=True).reshape(bsz, length, w_lru))

    h1, ft = _merge(x, ys5, yl[0], yl[1], zg, mod6, g_norm2[0], s5_w_glu[0].astype(BF16),
                    w_proj_a[0].astype(BF16), w_proj_b[0].astype(BF16), w_out[0].astype(BF16))

    r2, e2, n1, e1 = _route(ft, peer_w_query[0].T.astype(BF16), peer_sub_keys[0].astype(BF16))
    out = _peer(ft, peer_u[0].astype(BF16), peer_v[0].astype(BF16), r2, e2, n1, e1,
                h1.reshape(bsz * length, d), mod6, g_final,
                lambda i, tm: i // (length // tm))
    return out.reshape(bsz, length, d)
```

```python
import functools

import jax
import jax.numpy as jnp
from jax import lax
from jax.experimental import pallas as pl
from jax.experimental.pallas import tpu as pltpu

F32 = jnp.float32
BF16 = jnp.bfloat16
EPS = 1e-6
GRID_W = 64
S5_GROUP = 16
S5_STATE = 64
S5_CHUNK = 16
LRU_HEADS = 16
LRU_C = 8.0
CONV_W = 4
CONV_PAD_LO = (CONV_W - 1) // 2
N_KEYS = 128
PEER_HEADS = 8
PEER_TOPK = 16
COLS_PER_TILE = 8
MXU_TILE = 256
NORM_ROWS = 16
VMEM_LIMIT = 56 * 1024 * 1024


def _params(sem, vmem=VMEM_LIMIT):
    return pltpu.CompilerParams(dimension_semantics=sem, vmem_limit_bytes=vmem)


def _rms(x, g):
    return x * lax.rsqrt(jnp.mean(x * x, axis=-1, keepdims=True) + EPS) * g


def _ada_kernel(c_ref, w_ref, b_ref, o_ref):
    c = c_ref[...]
    sc = c * jax.nn.sigmoid(c)
    o_ref[...] = jnp.dot(sc.astype(BF16), w_ref[...].astype(BF16),
                         preferred_element_type=F32) + b_ref[...]


def _ada(cin, w, b):
    d, n = w.shape
    tn = 1536
    return pl.pallas_call(
        _ada_kernel,
        grid=(n // tn,),
        in_specs=[pl.BlockSpec((8, d), lambda j: (0, 0)),
                  pl.BlockSpec((d, tn), lambda j: (0, j)),
                  pl.BlockSpec((1, tn), lambda j: (0, j))],
        out_specs=pl.BlockSpec((8, tn), lambda j: (0, j)),
        out_shape=jax.ShapeDtypeStruct((8, n), F32),
        compiler_params=_params(("arbitrary",)),
        name="ada",
    )(cin, w, b.reshape(1, n))


def _inproj_kernel(x_ref, g_ref, sh_ref, sc_ref, w_ref, *refs, n_u, n_v):
    if len(refs) == 4:
        zu_ref, zv_ref, zg_ref, n_scr = refs
    else:
        zu_ref, zv_ref, n_scr = refs
        zg_ref = None
    j = pl.program_id(2)

    @pl.when(j == 0)
    def _():
        g, sh, scale1 = g_ref[...], sh_ref[...], 1.0 + sc_ref[...]
        for r in range(0, x_ref.shape[0], NORM_ROWS):
            rows = slice(r, r + NORM_ROWS)
            n_scr[rows, :] = (_rms(x_ref[rows, :], g) * scale1 + sh).astype(BF16)

    def project(out_ref):
        out_ref[...] = jnp.dot(n_scr[...], w_ref[...], preferred_element_type=F32).astype(out_ref.dtype)

    pl.when(j < n_u)(lambda: project(zu_ref))
    pl.when((j >= n_u) & (j < n_u + n_v))(lambda: project(zv_ref))
    if zg_ref is not None:
        pl.when(j >= n_u + n_v)(lambda: project(zg_ref))


def _inproj(x, g, mod6, mod_row, w_bf16, width_u, width_v):
    bsz, length, d = x.shape
    n = w_bf16.shape[1]
    tm = min(1024, length)
    tn = 1024
    n_u, n_v = width_u // tn, width_v // tn
    n_g = n // tn - n_u - n_v
    kern = functools.partial(_inproj_kernel, n_u=n_u, n_v=n_v)
    row = lambda k: (lambda b, i, j: (mod_row(b) * 6 + k, 0, 0))
    out_specs = [pl.BlockSpec((None, tm, tn), lambda b, i, j: (b, i, jnp.minimum(j, n_u - 1))),
                 pl.BlockSpec((None, tm, tn), lambda b, i, j: (b, i, jnp.clip(j - n_u, 0, n_v - 1)))]
    out_shape = [jax.ShapeDtypeStruct((bsz, length, width_u), F32),
                 jax.ShapeDtypeStruct((bsz, length, width_v), F32)]
    if n_g:
        out_specs.append(pl.BlockSpec((None, tm, tn), lambda b, i, j: (b, i, jnp.maximum(j - n_u - n_v, 0))))
        out_shape.append(jax.ShapeDtypeStruct((bsz, length, n_g * tn), BF16))
    return pl.pallas_call(
        kern,
        grid=(bsz, length // tm, n // tn),
        in_specs=[pl.BlockSpec((None, tm, d), lambda b, i, j: (b, i, 0)),
                  pl.BlockSpec((1, d), lambda b, i, j: (0, 0)),
                  pl.BlockSpec((None, 1, d), row(0)),
                  pl.BlockSpec((None, 1, d), row(1)),
                  pl.BlockSpec((d, tn), lambda b, i, j: (0, j))],
        out_specs=out_specs,
        out_shape=out_shape,
        scratch_shapes=[pltpu.VMEM((tm, d), BF16)],
        compiler_params=_params(("arbitrary", "arbitrary", "arbitrary")),
        name="inproj",
    )(x, g.reshape(1, d), mod6, mod6, w_bf16)


S5_PREP_BATCH = 8


def _s5_prep_kernel(*refs):
    for i in range(S5_PREP_BATCH):
        _s5_prep_one(*[r.at[i] for r in refs])


def _s5_prep_one(are_ref, aim_ref, ldt_ref, bre_ref, bim_ref, cre_ref, cim_ref,
                 k_ref, cg_ref, bg_ref, a16_ref):
    nk = S5_CHUNK + 1
    dt = jnp.exp(ldt_ref[...])
    lre, lim = are_ref[...], aim_ref[...]
    xr, xi = lre * dt, lim * dt
    e1 = jnp.exp(xr)
    ar, ai = e1 * jnp.cos(xi), e1 * jnp.sin(xi)
    pows = [(jnp.ones_like(ar), jnp.zeros_like(ai))]
    for _ in range(nk - 1):
        kr, ki = pows[-1]
        pows.append((kr * ar - ki * ai, kr * ai + ki * ar))
    rep = lambda rows: jnp.concatenate([jnp.broadcast_to(r, (S5_GROUP, r.shape[1])) for r in rows], axis=0)
    pr, pim = rep([q[0] for q in pows]), rep([q[1] for q in pows])
    den = lre * lre + lim * lim
    qr = ((ar - 1.0) * lre + ai * lim) / den
    qi = (ai * lre - (ar - 1.0) * lim) / den
    bre, bim = bre_ref[...], bim_ref[...]
    bbr, bbi = qr * bre - qi * bim, qr * bim + qi * bre
    tile = lambda m: jnp.concatenate([m] * nk, axis=0)
    cr_t, ci_t, br_t, bi_t = tile(cre_ref[...]), tile(cim_ref[...]), tile(bbr), tile(bbi)
    cg = jnp.concatenate([cr_t * pr - ci_t * pim, -(cr_t * pim + ci_t * pr)], axis=1)
    bg = jnp.concatenate([br_t * pr - bi_t * pim, br_t * pim + bi_t * pr], axis=1)
    bcat = jnp.concatenate([bbr, bbi], axis=1)
    cg_ref[...] = cg
    bg_ref[...] = bg
    k_ref[...] = lax.dot_general(cg[:S5_CHUNK * S5_GROUP], bcat, (((1,), (1,)), ((), ())),
                                 precision=lax.Precision.HIGHEST, preferred_element_type=F32)
    n = S5_CHUNK * S5_GROUP
    a16_ref[...] = jnp.concatenate([pr[n:n + 1], pim[n:n + 1]], axis=1)


def _s5_operators(a_re, a_im, log_dt, b_re, b_im, c_re, c_im, d_skip):
    _, groups, p = a_re.shape
    h = S5_GROUP
    dg = 2 * groups
    nk = S5_CHUNK + 1
    vec = lambda t: t.reshape(dg, 1, p)
    tr = lambda t: jnp.swapaxes(t, -1, -2).reshape(dg, h, p)
    ldt = jnp.broadcast_to(log_dt[..., None], (2, groups, p))
    nb = S5_PREP_BATCH
    batch = lambda *tail: pl.BlockSpec((nb,) + tail, lambda i: (i,) + (0,) * len(tail))
    spec_v, spec_m = batch(1, p), batch(h, p)
    kfl, cg, bg, a16 = pl.pallas_call(
        _s5_prep_kernel,
        grid=(dg // nb,),
        in_specs=[spec_v, spec_v, spec_v, spec_m, spec_m, spec_m, spec_m],
        out_specs=[batch(S5_CHUNK * h, h), batch(nk * h, 2 * p), batch(nk * h, 2 * p), batch(1, 2 * p)],
        out_shape=[jax.ShapeDtypeStruct((dg, S5_CHUNK * h, h), F32),
                   jax.ShapeDtypeStruct((dg, nk * h, 2 * p), F32),
                   jax.ShapeDtypeStruct((dg, nk * h, 2 * p), F32),
                   jax.ShapeDtypeStruct((dg, 1, 2 * p), F32)],
        compiler_params=_params(("arbitrary",)),
        name="s5_prep",
    )(vec(a_re), vec(a_im), vec(ldt), tr(b_re), tr(b_im),
      c_re.reshape(dg, h, p), c_im.reshape(dg, h, p))

    t = S5_CHUNK
    kfl = kfl.reshape(2, groups, t, h, h)
    cg = cg.reshape(2, groups, nk, h, 2 * p)
    bg = bg.reshape(2, groups, nk, h, 2 * p)
    ti = jnp.arange(t)
    lag = ti[:, None] - ti[None, :]
    kf = jnp.where((lag >= 0)[None, :, :, None, None], kfl[0][:, jnp.clip(lag, 0, t - 1)], 0.0)
    kb = jnp.where((lag <= 0)[None, :, :, None, None], kfl[1][:, jnp.clip(-lag, 0, t - 1)], 0.0)
    eye = (lag == 0)[None, :, :, None, None] * jnp.eye(h, dtype=F32)[None, None, None]
    m = kf + kb + eye * d_skip.reshape(groups, 1, 1, 1, h)
    mt = m.transpose(0, 2, 4, 1, 3).reshape(groups, t * h, t * h)
    bp_f = bg[0][:, t - 1 - ti].reshape(groups, t * h, 2 * p)
    bp_b = bg[1][:, ti].reshape(groups, t * h, 2 * p)
    bp = jnp.concatenate([bp_f, bp_b], axis=-1)
    cp_f = cg[0][:, ti + 1].reshape(groups, t * h, 2 * p)
    cp_b = cg[1][:, t - ti].reshape(groups, t * h, 2 * p)
    cp = jnp.concatenate([cp_f, cp_b], axis=-1).swapaxes(1, 2)
    return mt.astype(BF16), bp.astype(BF16), cp.astype(BF16), a16.reshape(2, groups, 1, 2 * p)


def _cplx_coef(a):
    p = a.shape[1] // 2
    lane = lax.broadcasted_iota(jnp.int32, a.shape, 1)
    sw = pltpu.roll(a, p, axis=1)
    return jnp.where(lane < p, a, sw), jnp.where(lane < p, -sw, a)


def _cplx_mul(a, x):
    c1, c2 = _cplx_coef(a)
    return c1 * x + c2 * pltpu.roll(x, x.shape[1] // 2, axis=1)


def _chunk_scan(x, a, reverse):
    n = x.shape[0]
    row = lax.broadcasted_iota(jnp.int32, (n, 1), 0)
    o = 1
    while o < n:
        if reverse:
            sh = jnp.where(row < n - o, pltpu.roll(x, n - o, axis=0), 0.0)
        else:
            sh = jnp.where(row >= o, pltpu.roll(x, o, axis=0), 0.0)
        x = x + _cplx_mul(a, sh)
        a = _cplx_mul(a, a)
        o *= 2
    return x


S5_SUPER = 16
S5_IN_FLIGHT = 4


def _entering_states(v_ref, sin_ref, a, h0, reverse):
    k_n = S5_SUPER
    ns = v_ref.shape[0] // k_n
    half = a.shape[1] // 2
    order = list(range(k_n - 1, -1, -1)) if reverse else list(range(k_n))
    rows_at = lambda k: pl.ds(k, ns, stride=k_n)
    pows = [None, a]
    for _ in range(k_n - 1):
        pows.append(_cplx_mul(a, pows[-1]))
    c1, c2 = _cplx_coef(a)
    loc = {}
    s = ss = None
    for k in order:
        x = v_ref[rows_at(k), :]
        xs = pltpu.roll(x, half, axis=1)
        s, ss = (x, xs) if s is None else (c1 * s + c2 * ss + x, c1 * ss - c2 * s + xs)
        loc[k] = s
    row = lax.broadcasted_iota(jnp.int32, (ns, 1), 0)
    edge = ns - 1 if reverse else 0
    ends = loc[order[-1]] + jnp.where(row == edge, _cplx_mul(pows[k_n], h0), 0.0)
    true_ends = _chunk_scan(ends, pows[k_n], reverse)
    cin = jnp.where(row == edge, h0, pltpu.roll(true_ends, ns - 1 if reverse else 1, axis=0))
    sin_ref[rows_at(order[0]), :] = cin
    for i in range(1, k_n):
        sin_ref[rows_at(order[i]), :] = loc[order[i - 1]] + _cplx_mul(pows[i], cin)


S5_GPB = 128 // S5_GROUP
S5_TPT = 128 // S5_GROUP


def _chunk_perm():
    n = S5_TPT * S5_GPB * S5_GROUP
    i = jnp.arange(n)
    t, g, h = i // (S5_GPB * S5_GROUP), (i // S5_GROUP) % S5_GPB, i % S5_GROUP
    dst = g * (S5_TPT * S5_GROUP) + t * S5_GROUP + h
    return jnp.zeros((n, n), BF16).at[i, dst].set(1.0)


def _load_chunks(src_ref, perm, dst_scr):
    nc = src_ref.shape[0] // S5_CHUNK
    halves = []
    for r in range(S5_CHUNK // S5_TPT):
        xcat = jnp.concatenate(
            [src_ref[pl.ds(r * S5_TPT + t, nc, stride=S5_CHUNK), :].astype(BF16) for t in range(S5_TPT)], axis=1)
        halves.append(jnp.dot(xcat, perm, preferred_element_type=F32).astype(BF16))
    for g in range(S5_GPB):
        dst_scr[g] = jnp.concatenate([hv[:, g * 128:(g + 1) * 128] for hv in halves], axis=1)


def _s5_kernel(u_ref, uc_ref, perm_ref, permt_ref, mt_ref, bp_ref, cp_ref, af_ref, ab_ref, y_ref,
               u_scr, uc_scr, y_scr, v_scr, sin_scr):
    nc = u_ref.shape[0] // S5_CHUNK
    _load_chunks(u_ref, perm_ref[...], u_scr)
    _load_chunks(uc_ref, perm_ref[...], uc_scr)

    def group(g, _):
        y_scr[g] = _s5_group(u_scr[g], uc_scr[g], mt_ref[g], bp_ref[g], cp_ref[g], af_ref[g], ab_ref[g],
                             v_scr.at[g % S5_IN_FLIGHT], sin_scr.at[g % S5_IN_FLIGHT])
        return 0

    lax.fori_loop(0, S5_GPB, group, 0, unroll=S5_IN_FLIGHT)

    for r in range(S5_CHUNK // S5_TPT):
        ycat = jnp.concatenate([y_scr[g][:, r * 128:(r + 1) * 128] for g in range(S5_GPB)], axis=1)
        back = jnp.dot(ycat, permt_ref[...], preferred_element_type=F32)
        for t in range(S5_TPT):
            y_ref[pl.ds(r * S5_TPT + t, nc, stride=S5_CHUNK), :] = back[:, t * 128:(t + 1) * 128]


def _s5_group(u, uc, mt, bp, cp, af, ab, v_scr, sin_scr):
    ps = af.shape[1]
    v = jnp.dot(u, bp, preferred_element_type=F32)
    v_scr[0], v_scr[1] = v[:, :ps], v[:, ps:]
    vc = jnp.dot(uc, bp, preferred_element_type=F32)
    ncc = vc.shape[0]
    hcf = _chunk_scan(vc[:, :ps], af, False)[ncc - 1:ncc]
    hcb = _chunk_scan(vc[:, ps:], ab, True)[0:1]
    _entering_states(v_scr.at[0], sin_scr.at[0], af, hcf, False)
    _entering_states(v_scr.at[1], sin_scr.at[1], ab, hcb, True)
    s_in = jnp.concatenate([sin_scr[0], sin_scr[1]], axis=1).astype(BF16)
    y = (jnp.dot(u, mt, preferred_element_type=F32)
         + jnp.dot(s_in, cp, preferred_element_type=F32))
    return y.astype(BF16)


def _s5(zu, zu_c, mt, bp, cp, a16):
    bsz, length, w = zu.shape
    lctx = zu_c.shape[1]
    nc, ncc = length // S5_CHUNK, lctx // S5_CHUNK
    cw = S5_CHUNK * S5_GROUP
    ps = a16.shape[-1]
    perm = _chunk_perm()
    once = lambda shape, imap: pl.BlockSpec(shape, imap, pipeline_mode=pl.Buffered(1))
    wspec = pl.BlockSpec((S5_GPB, cw, cw), lambda b, q: (q, 0, 0))
    return pl.pallas_call(
        _s5_kernel,
        grid=(bsz, w // 128),
        in_specs=[once((None, length, 128), lambda b, q: (b, 0, q)),
                  pl.BlockSpec((None, lctx, 128), lambda b, q: (b, 0, q)),
                  once(perm.shape, lambda b, q: (0, 0)),
                  once(perm.shape, lambda b, q: (0, 0)),
                  wspec, wspec, wspec,
                  pl.BlockSpec((None, S5_GPB, 1, ps), lambda b, q: (0, q, 0, 0)),
                  pl.BlockSpec((None, S5_GPB, 1, ps), lambda b, q: (1, q, 0, 0))],
        out_specs=once((None, length, 128), lambda b, q: (b, 0, q)),
        out_shape=jax.ShapeDtypeStruct((bsz, length, w), F32),
        scratch_shapes=[pltpu.VMEM((S5_GPB, nc, cw), BF16),
                        pltpu.VMEM((S5_GPB, ncc, cw), BF16),
                        pltpu.VMEM((S5_GPB, nc, cw), BF16),
                        pltpu.VMEM((S5_IN_FLIGHT, 2, nc, ps), F32),
                        pltpu.VMEM((S5_IN_FLIGHT, 2, nc, ps), F32)],
        compiler_params=_params(("arbitrary", "arbitrary")),
        name="s5",
    )(zu, zu_c, perm, perm.T, mt, bp, cp, a16, a16)


LRU_ROW_BLOCK = 16


def _lru_kernel(v_ref, cw_ref, cb_ref, wr_ref, br_ref, wi_ref, bi_ref, lam_ref, h0_ref, o_ref,
                vpad, a_scr, b_scr, carry_scr, *, rows, rev, chain):
    ct = v_ref.shape[-1]
    cpt = COLS_PER_TILE
    rb = LRU_ROW_BLOCK
    nblk = rows // rb
    nslab = ct // MXU_TILE

    if chain:
        @pl.when(pl.program_id(2) == 0)
        def _():
            carry_scr[...] = h0_ref[...]

    zero_row = jnp.zeros((cpt, ct), F32)
    for k in range(CONV_PAD_LO):
        vpad[k] = zero_row
    for k in range(CONV_W - 1 - CONV_PAD_LO):
        vpad[CONV_PAD_LO + rows + k] = zero_row

    def copy(i, _):
        r0 = pl.multiple_of(i * rb, rb)
        vpad[pl.ds(r0 + CONV_PAD_LO, rb)] = v_ref[pl.ds(r0, rb)]
        return 0

    lax.fori_loop(0, nblk, copy, 0)

    cw = cw_ref[...]
    cb = cb_ref[...]
    nl = -lam_ref[...]
    softplus = jnp.maximum(nl, 0.0) + jnp.log1p(jnp.exp(-jnp.abs(nl)))
    c8 = -LRU_C * softplus
    b_r, b_i = br_ref[...], bi_ref[...]

    def block(i, carry):
        acc_a, acc_h = carry
        bi_ = (nblk - 1 - i) if rev else i
        r0 = pl.multiple_of(bi_ * rb, rb)
        x = cb
        for k in range(CONV_W):
            x = x + cw[k:k + 1] * vpad[pl.ds(r0 + k, rb)].reshape(rb * cpt, ct)
        xb = x.astype(BF16)
        pre_r = jnp.concatenate(
            [jnp.dot(xb[:, s * MXU_TILE:(s + 1) * MXU_TILE], wr_ref[s], preferred_element_type=F32)
             for s in range(nslab)], axis=1)
        pre_i = jnp.concatenate(
            [jnp.dot(xb[:, s * MXU_TILE:(s + 1) * MXU_TILE], wi_ref[s], preferred_element_type=F32)
             for s in range(nslab)], axis=1)
        r = jax.nn.sigmoid(pre_r + b_r)
        ig = jax.nn.sigmoid(pre_i + b_i)
        log_a = c8 * r
        a = jnp.exp(log_a)
        z = 1.0 - a * a
        bx = jnp.where(z > 0.0, z * lax.rsqrt(z), 0.0) * (ig * x)
        q0 = pl.multiple_of(r0 * cpt, rb * cpt)
        a_scr[pl.ds(q0, rb * cpt), :] = a
        b_scr[pl.ds(q0, rb * cpt), :] = bx
        order = range(rb - 1, -1, -1) if rev else range(rb)
        for j in order:
            aj = a[j * cpt:(j + 1) * cpt]
            acc_h = aj * acc_h + bx[j * cpt:(j + 1) * cpt]
            acc_a = acc_a * aj
        return acc_a, acc_h

    acc_a, acc_h = lax.fori_loop(0, nblk, block, (jnp.ones((cpt, ct), F32), jnp.zeros((cpt, ct), F32)))

    if not chain:
        o_ref[...] = acc_h
        return

    sub = lax.broadcasted_iota(jnp.int32, (cpt, ct), 0)
    carry = carry_scr[...]
    h_in = jnp.zeros((cpt, ct), F32)
    for s in (range(cpt - 1, -1, -1) if rev else range(cpt)):
        h_in = jnp.where(sub == s, carry, h_in)
        carry = acc_a[s:s + 1] * carry + acc_h[s:s + 1]
    carry_scr[...] = carry

    def row(i, h):
        r = (rows - 1 - i) if rev else i
        q = pl.multiple_of(r * cpt, cpt)
        h = a_scr[pl.ds(q, cpt), :] * h + b_scr[pl.ds(q, cpt), :]
        o_ref[r] = h
        return h

    lax.fori_loop(0, rows, row, h_in, unroll=8)


def _block_diag(w, per):
    hh, n, _ = w.shape
    eye = jnp.eye(per, dtype=w.dtype)
    return jnp.einsum('gpij,pq->gpiqj', w.reshape(hh // per, per, n, n), eye).reshape(hh // per, per * n, per * n)


def _lru(v5, conv_w, conv_b, lam, w_r, b_r, w_i, b_i, h0, *, rev, chain):
    bsz, rows, ncg, cpt, w = v5.shape
    ct = 512
    nct = w // ct
    per = MXU_TILE // (w // LRU_HEADS)
    wr = _block_diag(w_r, per).astype(BF16)
    wi = _block_diag(w_i, per).astype(BF16)
    nslab = ct // MXU_TILE
    cgi = (lambda c: ncg - 1 - c) if rev else (lambda c: c)
    vec = lambda n: pl.BlockSpec((n, ct), lambda b, k, c: (0, k))
    in_specs = [pl.BlockSpec((None, rows, None, cpt, ct), lambda b, k, c: (b, 0, cgi(c), 0, k)),
                vec(CONV_W), vec(1),
                pl.BlockSpec((nslab, MXU_TILE, MXU_TILE), lambda b, k, c: (k, 0, 0)), vec(1),
                pl.BlockSpec((nslab, MXU_TILE, MXU_TILE), lambda b, k, c: (k, 0, 0)), vec(1),
                vec(1),
                pl.BlockSpec((None, 1, ct), lambda b, k, c: (b, 0, k))]
    if chain:
        out_spec = pl.BlockSpec((None, rows, None, cpt, ct), lambda b, k, c: (b, 0, cgi(c), 0, k))
        out_shape = jax.ShapeDtypeStruct(v5.shape, F32)
    else:
        out_spec = pl.BlockSpec((None, None, cpt, ct), lambda b, k, c: (b, c, 0, k))
        out_shape = jax.ShapeDtypeStruct((bsz, ncg, cpt, w), F32)
    kern = functools.partial(_lru_kernel, rows=rows, rev=rev, chain=chain)
    return pl.pallas_call(
        kern,
        grid=(bsz, nct, ncg),
        in_specs=in_specs,
        out_specs=out_spec,
        out_shape=out_shape,
        scratch_shapes=[pltpu.VMEM((rows + CONV_W - 1, cpt, ct), F32),
                        pltpu.VMEM((rows * cpt, ct), F32),
                        pltpu.VMEM((rows * cpt, ct), F32),
                        pltpu.VMEM((1, ct), F32)],
        compiler_params=_params(("arbitrary", "arbitrary", "arbitrary")),
        name="lru_rev" if rev else "lru_fwd",
    )(v5, conv_w, conv_b.reshape(1, w), wr, b_r.reshape(1, w), wi, b_i.reshape(1, w),
      lam.reshape(1, w), h0)


def _merge_kernel(x_ref, ys_ref, ylf_ref, ylb_ref, zg_ref, gate_ref, sh_ref, sc_ref, g2_ref,
                  wglu_ref, wa_ref, wb_ref, wo_ref, h1_ref, ft_ref, *, w_lru, d):
    za = jax.nn.gelu(ys_ref[...])
    ya = (za * jax.nn.sigmoid(jnp.dot(za.astype(BF16), wglu_ref[...],
                                      preferred_element_type=F32))).astype(BF16)
    zg = zg_ref[...]
    yb = ((ylf_ref[...] + ylb_ref[...]) * jax.nn.gelu(zg[:, :w_lru].astype(F32))).astype(BF16)
    ga = jax.nn.sigmoid(zg[:, w_lru:w_lru + d].astype(F32))
    gb = jax.nn.sigmoid(zg[:, w_lru + d:].astype(F32))
    m = (ga * jnp.dot(ya, wa_ref[...], preferred_element_type=F32)
         + gb * jnp.dot(yb, wb_ref[...], preferred_element_type=F32))
    o = jnp.dot(m.astype(BF16), wo_ref[...], preferred_element_type=F32)
    h1 = x_ref[...] + gate_ref[...] * o
    h1_ref[...] = h1
    f = _rms(h1, g2_ref[...]) * (1.0 + sc_ref[...]) + sh_ref[...]
    ft_ref[...] = f.T.astype(BF16)


def _merge(x, ys5, ylf, ylb, zg, mod6, g2, w_glu, w_a, w_b, w_o):
    bsz, length, d = x.shape
    w_s5 = ys5.shape[-1]
    w_lru = ylf.shape[-1]
    tm = 256
    nt = length // tm
    row = lambda k: (lambda b, i: (b * 6 + k, 0, 0))
    tok = lambda w: pl.BlockSpec((None, tm, w), lambda b, i: (b, i, 0))
    const = lambda shape: pl.BlockSpec(shape, lambda b, i: (0, 0), pipeline_mode=pl.Buffered(1))
    kern = functools.partial(_merge_kernel, w_lru=w_lru, d=d)
    return pl.pallas_call(
        kern,
        grid=(bsz, nt),
        in_specs=[tok(d), tok(w_s5), tok(w_lru), tok(w_lru), tok(zg.shape[-1]),
                  pl.BlockSpec((None, 1, d), row(2)), pl.BlockSpec((None, 1, d), row(3)),
                  pl.BlockSpec((None, 1, d), row(4)),
                  pl.BlockSpec((1, d), lambda b, i: (0, 0)),
                  const(w_glu.shape), const(w_a.shape), const(w_b.shape), const(w_o.shape)],
        out_specs=[tok(d), pl.BlockSpec((d, tm), lambda b, i: (0, b * nt + i))],
        out_shape=[jax.ShapeDtypeStruct((bsz, length, d), F32),
                   jax.ShapeDtypeStruct((d, bsz * length), BF16)],
        compiler_params=_params(("arbitrary", "arbitrary")),
        name="merge",
    )(x, ys5, ylf, ylb, zg, mod6, mod6, mod6, g2.reshape(1, d), w_glu, w_a, w_b, w_o)


def _sort_pairs(n):
    out = []
    p = 1
    while p < n:
        k = p
        while k >= 1:
            for j in range(k % p, n - k, 2 * k):
                for i in range(min(k, n - j - k)):
                    if (i + j) // (2 * p) == (i + j + k) // (2 * p):
                        out.append((i + j, i + j + k))
            k //= 2
        p *= 2
    return out


def _top_rows_sorted(x, k):
    n_slab = x.shape[0] // 8
    slabs = [x[8 * r:8 * r + 8] for r in range(n_slab)]
    for a, b in _sort_pairs(n_slab):
        slabs[a], slabs[b] = jnp.maximum(slabs[a], slabs[b]), jnp.minimum(slabs[a], slabs[b])
    slabs.append(jnp.full_like(slabs[0], -jnp.inf))
    out = []
    for it in range(k):
        m = jnp.max(slabs[0], axis=0, keepdims=True)
        out.append(m)
        hit = slabs[0] == m
        for r in range(min(k - 1 - it, n_slab)):
            slabs[r] = jnp.where(hit, slabs[r + 1], slabs[r])
    return out


def _count_above(rows, x, strict):
    assert PEER_TOPK == 16
    above = (lambda a: a > x) if strict else (lambda a: a >= x)
    sel = jnp.where
    b3 = above(rows[7])
    b2 = above(sel(b3, rows[11], rows[3]))
    b1 = above(sel(b3, sel(b2, rows[13], rows[9]), sel(b2, rows[5], rows[1])))
    b0 = above(sel(b3, sel(b2, sel(b1, rows[14], rows[12]), sel(b1, rows[10], rows[8])),
                   sel(b2, sel(b1, rows[6], rows[4]), sel(b1, rows[2], rows[0]))))
    return (sel(b3, 8.0, 0.0) + sel(b2, 4.0, 0.0) + sel(b1, 2.0, 0.0) + sel(b0, 1.0, 0.0)
            + sel(above(rows[15]), 1.0, 0.0))


_CAND_PAIRS = [(j, k) for j in range(PEER_TOPK + 1) for k in range(PEER_TOPK + 1)
               if (j + 1) * (k + 1) <= PEER_TOPK + 1]
_CAND_ROWS = 64
assert len(_CAND_PAIRS) <= _CAND_ROWS


def _route_kernel(ft_ref, wq_ref, sk_ref, r2_ref, e2_ref, n1_ref, e1_ref, cand_scr):
    half = sk_ref.shape[2]
    tm = ft_ref.shape[1]
    qt = jnp.dot(wq_ref[...], ft_ref[...], preferred_element_type=F32).astype(BF16)
    for r in range(len(_CAND_PAIRS), _CAND_ROWS):
        cand_scr[r:r + 1, :] = jnp.full((1, 128), -jnp.inf, F32)
    for h in range(PEER_HEADS):
        s_full = [jnp.dot(sk_ref[side], qt[(2 * h + side) * half:(2 * h + side + 1) * half],
                          preferred_element_type=F32) for side in range(2)]
        for l in range(tm // 128):
            lanes = slice(l * 128, (l + 1) * 128)
            s = [sf[:, lanes] for sf in s_full]
            top0 = _top_rows_sorted(s[0], PEER_TOPK + 1)
            top1 = _top_rows_sorted(s[1], PEER_TOPK + 1)
            for r, (j, k) in enumerate(_CAND_PAIRS):
                cand_scr[r:r + 1, :] = top0[j] + top1[k]
            best = _top_rows_sorted(cand_scr[...], PEER_TOPK + 1)
            z = jnp.ones_like(best[0])
            for k in range(1, PEER_TOPK):
                z = z + jnp.exp(best[k] - best[0])
            t1 = 0.5 * (best[PEER_TOPK - 1] + best[PEER_TOPK]) - s[0]
            n1 = _count_above(top1, t1, strict=False)
            rank2 = _count_above(top1, s[1], strict=True)
            r2_ref[h, :, lanes] = rank2.astype(BF16)
            e2_ref[h, :, lanes] = jnp.exp(s[1] - top1[0]).astype(BF16)
            n1_ref[h, :, lanes] = n1
            e1_ref[h, :, lanes] = jnp.exp(s[0] - top0[0]) / z


def _route(ft, wq_t, sk):
    d, nt = ft.shape
    nk = sk.shape[1]
    tm = 256
    ospec = pl.BlockSpec((PEER_HEADS, nk, tm), lambda i: (0, 0, i))
    return pl.pallas_call(
        _route_kernel,
        grid=(nt // tm,),
        in_specs=[pl.BlockSpec((d, tm), lambda i: (0, i)),
                  pl.BlockSpec(wq_t.shape, lambda i: (0, 0), pipeline_mode=pl.Buffered(1)),
                  pl.BlockSpec(sk.shape, lambda i: (0, 0, 0))],
        out_specs=[ospec] * 4,
        out_shape=[jax.ShapeDtypeStruct((PEER_HEADS, nk, nt), dt) for dt in (BF16, BF16, F32, F32)],
        scratch_shapes=[pltpu.VMEM((_CAND_ROWS, 128), F32)],
        compiler_params=_params(("arbitrary",)),
        name="route",
    )(ft, wq_t, sk)


def _rows_bf16(row, n):
    tile = jnp.broadcast_to(row, (16, row.shape[1])).astype(BF16)
    return jnp.concatenate([tile] * (n // 16), axis=0)


def _peer_kernel(ft_ref, u_ref, v_ref, r2_ref, e2_ref, n1_ref, e1_ref, h1_ref, gate_ref, gf_ref,
                 o_ref, acc_ref, *, nk):
    j = pl.program_id(1)

    @pl.when(j == 0)
    def _():
        acc_ref[...] = jnp.zeros_like(acc_ref)

    tm = ft_ref.shape[1]
    n_i1 = u_ref.shape[0] // nk
    per = MXU_TILE // nk
    ps = []
    for c in range(n_i1 // per):
        st = jnp.dot(u_ref[c * MXU_TILE:(c + 1) * MXU_TILE, :], ft_ref[...],
                     preferred_element_type=F32)
        for a in range(per):
            i1 = j * n_i1 + c * per + a
            act = jax.nn.gelu(st[a * nk:(a + 1) * nk].astype(BF16))
            w = jnp.zeros((nk, tm), BF16)
            for h in range(PEER_HEADS):
                cnt = _rows_bf16(n1_ref[h, pl.ds(i1, 1), :], nk)
                cf = _rows_bf16(e1_ref[h, pl.ds(i1, 1), :], nk)
                w = w + jnp.where(r2_ref[h] < cnt, e2_ref[h] * cf, jnp.zeros((), BF16))
            ps.append(w * act)
    p = jnp.concatenate(ps, axis=0)
    acc_ref[...] += lax.dot_general(p, v_ref[...], (((0,), (0,)), ((), ())), preferred_element_type=F32)

    @pl.when(j == pl.num_programs(1) - 1)
    def _():
        h2 = h1_ref[...] + gate_ref[...] * acc_ref[...]
        o_ref[...] = _rms(h2, gf_ref[...])


def _peer(ft, u, v, r2, e2, n1, e1, h1, mod6, g_final, tiles_per_batch_fn):
    d, nt = ft.shape
    ne = u.shape[0]
    nk = r2.shape[1]
    tm = 512
    et = 1024
    rspec = pl.BlockSpec((PEER_HEADS, nk, tm), lambda i, j: (0, 0, i), pipeline_mode=pl.Buffered(1))
    kern = functools.partial(_peer_kernel, nk=nk)
    return pl.pallas_call(
        kern,
        grid=(nt // tm, ne // et),
        in_specs=[pl.BlockSpec((d, tm), lambda i, j: (0, i)),
                  pl.BlockSpec((et, d), lambda i, j: (j, 0)),
                  pl.BlockSpec((et, d), lambda i, j: (j, 0)),
                  rspec, rspec, rspec, rspec,
                  pl.BlockSpec((tm, d), lambda i, j: (i, 0), pipeline_mode=pl.Buffered(1)),
                  pl.BlockSpec((None, 1, d), lambda i, j: (tiles_per_batch_fn(i, tm) * 6 + 5, 0, 0)),
                  pl.BlockSpec((1, d), lambda i, j: (0, 0))],
        out_specs=pl.BlockSpec((tm, d), lambda i, j: (i, 0), pipeline_mode=pl.Buffered(1)),
        out_shape=jax.ShapeDtypeStruct((nt, d), F32),
        scratch_shapes=[pltpu.VMEM((tm, d), F32)],
        compiler_params=_params(("arbitrary", "arbitrary")),
        name="peer",
    )(ft, u, v, r2, e2, n1, e1, h1, mod6, g_final.reshape(1, d))


def kernel(x, c, ctx, c_ctx, w_ada, b_ada, g_norm1, w_in, s5_a_re, s5_a_im, s5_log_dt, s5_b_re, s5_b_im, s5_c_re, s5_c_im, s5_d, s5_w_glu, lru_conv_w, lru_conv_b, lru_lambda, lru_w_r, lru_b_r, lru_w_i, lru_b_i, w_proj_a, w_proj_b, w_out, g_norm2, peer_w_query, peer_sub_keys, peer_u, peer_v, g_final):
    bsz, length, d = x.shape
    assert w_ada.shape[0] == 1, "single-layer kernel"
    rows = length // GRID_W
    w_s5 = s5_d.shape[-1]
    w_lru = lru_lambda.shape[-1]
    assert bsz < 8 and length % (GRID_W * LRU_ROW_BLOCK) == 0 and ctx.shape[1] % LRU_ROW_BLOCK == 0

    cin = jnp.zeros((8, d), F32).at[:bsz].set(c).at[bsz].set(c_ctx)
    mod6 = _ada(cin, w_ada[0], b_ada[0]).reshape(8 * 6, 1, d)

    w_in_b = w_in[0].astype(BF16)
    zu, zv, zg = _inproj(x, g_norm1[0], mod6, lambda b: b, w_in_b, w_s5, w_lru)
    zu_c, zv_c = _inproj(ctx, g_norm1[0], mod6, lambda b: bsz, w_in_b[:, :w_s5 + w_lru], w_s5, w_lru)

    mt, bp, cp, a16 = _s5_operators(s5_a_re[0], s5_a_im[0], s5_log_dt[0], s5_b_re[0], s5_b_im[0],
                                    s5_c_re[0], s5_c_im[0], s5_d[0])
    ys5 = _s5(zu, zu_c, mt, bp, cp, a16)

    lctx = ctx.shape[1]
    vc5 = jnp.zeros((1, lctx, 1, COLS_PER_TILE, w_lru), F32).at[0, :, 0, :bsz].set(zv_c.transpose(1, 0, 2))
    v5 = zv.reshape(bsz, rows, GRID_W // COLS_PER_TILE, COLS_PER_TILE, w_lru)
    zero_h = jnp.zeros((1, 1, w_lru), F32)
    yl = []
    for dr in range(2):
        args = (lru_conv_w[0], lru_conv_b[0], lru_lambda[0, dr], lru_w_r[0, dr], lru_b_r[0, dr],
                lru_w_i[0, dr], lru_b_i[0, dr])
        hc = _lru(vc5, *args, zero_h, rev=dr == 1, chain=False)
        h0 = hc[0, 0, :bsz].reshape(bsz, 1, w_lru)
        yl.append(_lru(v5, *args, h0, rev=dr == 1, chain=True).reshape(bsz, length, w_lru))

    h1, ft = _merge(x, ys5, yl[0], yl[1], zg, mod6, g_norm2[0], s5_w_glu[0].astype(BF16),
                    w_proj_a[0].astype(BF16), w_proj_b[0].astype(BF16), w_out[0].astype(BF16))

    r2, e2, n1, e1 = _route(ft, peer_w_query[0].T.astype(BF16), peer_sub_keys[0].astype(BF16))
    out = _peer(ft, peer_u[0].astype(BF16), peer_v[0].astype(BF16), r2, e2, n1, e1,
                h1.reshape(bsz * length, d), mod6, g_final,
                lambda i, tm: i // (length // tm))
    return out.reshape(bsz, length, d)
```

```python
import functools

import jax
import jax.numpy as jnp
from jax import lax
from jax.experimental import pallas as pl
from jax.experimental.pallas import tpu as pltpu

F32 = jnp.float32
BF16 = jnp.bfloat16
EPS = 1e-6
GRID_W = 64
S5_GROUP = 16
S5_STATE = 64
S5_CHUNK = 16
LRU_HEADS = 16
LRU_C = 8.0
CONV_W = 4
CONV_PAD_LO = (CONV_W - 1) // 2
N_KEYS = 128
PEER_HEADS = 8
PEER_TOPK = 16
COLS_PER_TILE = 8
MXU_TILE = 256
NORM_ROWS = 16
VMEM_LIMIT = 56 * 1024 * 1024


def _params(sem, vmem=VMEM_LIMIT):
    return pltpu.CompilerParams(dimension_semantics=sem, vmem_limit_bytes=vmem)


def _rms(x, g):
    return x * lax.rsqrt(jnp.mean(x * x, axis=-1, keepdims=True) + EPS) * g


def _ada_kernel(c_ref, w_ref, b_ref, o_ref):
    c = c_ref[...]
    sc = c * jax.nn.sigmoid(c)
    o_ref[...] = jnp.dot(sc.astype(BF16), w_ref[...].astype(BF16),
                         preferred_element_type=F32) + b_ref[...]


def _ada(cin, w, b):
    d, n = w.shape
    tn = 1536
    return pl.pallas_call(
        _ada_kernel,
        grid=(n // tn,),
        in_specs=[pl.BlockSpec((8, d), lambda j: (0, 0)),
                  pl.BlockSpec((d, tn), lambda j: (0, j)),
                  pl.BlockSpec((1, tn), lambda j: (0, j))],
        out_specs=pl.BlockSpec((8, tn), lambda j: (0, j)),
        out_shape=jax.ShapeDtypeStruct((8, n), F32),
        compiler_params=_params(("arbitrary",)),
        name="ada",
    )(cin, w, b.reshape(1, n))


def _inproj_kernel(x_ref, g_ref, sh_ref, sc_ref, w_ref, *refs, n_u, n_v):
    if len(refs) == 4:
        zu_ref, zv_ref, zg_ref, n_scr = refs
    else:
        zu_ref, zv_ref, n_scr = refs
        zg_ref = None
    j = pl.program_id(2)

    @pl.when(j == 0)
    def _():
        g, sh, scale1 = g_ref[...], sh_ref[...], 1.0 + sc_ref[...]
        for r in range(0, x_ref.shape[0], NORM_ROWS):
            rows = slice(r, r + NORM_ROWS)
            n_scr[rows, :] = (_rms(x_ref[rows, :], g) * scale1 + sh).astype(BF16)

    def project(out_ref):
        out_ref[...] = jnp.dot(n_scr[...], w_ref[...], preferred_element_type=F32).astype(out_ref.dtype)

    pl.when(j < n_u)(lambda: project(zu_ref))
    pl.when((j >= n_u) & (j < n_u + n_v))(lambda: project(zv_ref))
    if zg_ref is not None:
        pl.when(j >= n_u + n_v)(lambda: project(zg_ref))


def _inproj(x, g, mod6, mod_row, w_bf16, width_u, width_v):
    bsz, length, d = x.shape
    n = w_bf16.shape[1]
    tm = min(1024, length)
    tn = 1024
    n_u, n_v = width_u // tn, width_v // tn
    n_g = n // tn - n_u - n_v
    kern = functools.partial(_inproj_kernel, n_u=n_u, n_v=n_v)
    row = lambda k: (lambda b, i, j: (mod_row(b) * 6 + k, 0, 0))
    out_specs = [pl.BlockSpec((None, tm, tn), lambda b, i, j: (b, i, jnp.minimum(j, n_u - 1))),
                 pl.BlockSpec((None, tm, tn), lambda b, i, j: (b, i, jnp.clip(j - n_u, 0, n_v - 1)))]
    out_shape = [jax.ShapeDtypeStruct((bsz, length, width_u), F32),
                 jax.ShapeDtypeStruct((bsz, length, width_v), F32)]
    if n_g:
        out_specs.append(pl.BlockSpec((None, tm, tn), lambda b, i, j: (b, i, jnp.maximum(j - n_u - n_v, 0))))
        out_shape.append(jax.ShapeDtypeStruct((bsz, length, n_g * tn), BF16))
    return pl.pallas_call(
        kern,
        grid=(bsz, length // tm, n // tn),
        in_specs=[pl.BlockSpec((None, tm, d), lambda b, i, j: (b, i, 0)),
                  pl.BlockSpec((1, d), lambda b, i, j: (0, 0)),
                  pl.BlockSpec((None, 1, d), row(0)),
                  pl.BlockSpec((None, 1, d), row(1)),
                  pl.BlockSpec((d, tn), lambda b, i, j: (0, j))],
        out_specs=out_specs,
        out_shape=out_shape,
        scratch_shapes=[pltpu.VMEM((tm, d), BF16)],
        compiler_params=_params(("arbitrary", "arbitrary", "arbitrary")),
        name="inproj",
    )(x, g.reshape(1, d), mod6, mod6, w_bf16)


S5_PREP_BATCH = 8


def _s5_prep_kernel(*refs):
    for i in range(S5_PREP_BATCH):
        _s5_prep_one(*[r.at[i] for r in refs])


def _s5_prep_one(are_ref, aim_ref, ldt_ref, bre_ref, bim_ref, cre_ref, cim_ref,
                 k_ref, cg_ref, bg_ref, a16_ref):
    nk = S5_CHUNK + 1
    dt = jnp.exp(ldt_ref[...])
    lre, lim = are_ref[...], aim_ref[...]
    xr, xi = lre * dt, lim * dt
    e1 = jnp.exp(xr)
    ar, ai = e1 * jnp.cos(xi), e1 * jnp.sin(xi)
    pows = [(jnp.ones_like(ar), jnp.zeros_like(ai))]
    for _ in range(nk - 1):
        kr, ki = pows[-1]
        pows.append((kr * ar - ki * ai, kr * ai + ki * ar))
    rep = lambda rows: jnp.concatenate([jnp.broadcast_to(r, (S5_GROUP, r.shape[1])) for r in rows], axis=0)
    pr, pim = rep([q[0] for q in pows]), rep([q[1] for q in pows])
    den = lre * lre + lim * lim
    qr = ((ar - 1.0) * lre + ai * lim) / den
    qi = (ai * lre - (ar - 1.0) * lim) / den
    bre, bim = bre_ref[...], bim_ref[...]
    bbr, bbi = qr * bre - qi * bim, qr * bim + qi * bre
    tile = lambda m: jnp.concatenate([m] * nk, axis=0)
    cr_t, ci_t, br_t, bi_t = tile(cre_ref[...]), tile(cim_ref[...]), tile(bbr), tile(bbi)
    cg = jnp.concatenate([cr_t * pr - ci_t * pim, -(cr_t * pim + ci_t * pr)], axis=1)
    bg = jnp.concatenate([br_t * pr - bi_t * pim, br_t * pim + bi_t * pr], axis=1)
    bcat = jnp.concatenate([bbr, bbi], axis=1)
    cg_ref[...] = cg
    bg_ref[...] = bg
    k_ref[...] = lax.dot_general(cg[:S5_CHUNK * S5_GROUP], bcat, (((1,), (1,)), ((), ())),
                                 precision=lax.Precision.HIGHEST, preferred_element_type=F32)
    n = S5_CHUNK * S5_GROUP
    a16_ref[...] = jnp.concatenate([pr[n:n + 1], pim[n:n + 1]], axis=1)


def _s5_operators(a_re, a_im, log_dt, b_re, b_im, c_re, c_im, d_skip):
    _, groups, p = a_re.shape
    h = S5_GROUP
    dg = 2 * groups
    nk = S5_CHUNK + 1
    vec = lambda t: t.reshape(dg, 1, p)
    tr = lambda t: jnp.swapaxes(t, -1, -2).reshape(dg, h, p)
    ldt = jnp.broadcast_to(log_dt[..., None], (2, groups, p))
    nb = S5_PREP_BATCH
    batch = lambda *tail: pl.BlockSpec((nb,) + tail, lambda i: (i,) + (0,) * len(tail))
    spec_v, spec_m = batch(1, p), batch(h, p)
    kfl, cg, bg, a16 = pl.pallas_call(
        _s5_prep_kernel,
        grid=(dg // nb,),
        in_specs=[spec_v, spec_v, spec_v, spec_m, spec_m, spec_m, spec_m],
        out_specs=[batch(S5_CHUNK * h, h), batch(nk * h, 2 * p), batch(nk * h, 2 * p), batch(1, 2 * p)],
        out_shape=[jax.ShapeDtypeStruct((dg, S5_CHUNK * h, h), F32),
                   jax.ShapeDtypeStruct((dg, nk * h, 2 * p), F32),
                   jax.ShapeDtypeStruct((dg, nk * h, 2 * p), F32),
                   jax.ShapeDtypeStruct((dg, 1, 2 * p), F32)],
        compiler_params=_params(("arbitrary",)),
        name="s5_prep",
    )(vec(a_re), vec(a_im), vec(ldt), tr(b_re), tr(b_im),
      c_re.reshape(dg, h, p), c_im.reshape(dg, h, p))

    t = S5_CHUNK
    kfl = kfl.reshape(2, groups, t, h, h)
    cg = cg.reshape(2, groups, nk, h, 2 * p)
    bg = bg.reshape(2, groups, nk, h, 2 * p)
    ti = jnp.arange(t)
    lag = ti[:, None] - ti[None, :]
    kf = jnp.where((lag >= 0)[None, :, :, None, None], kfl[0][:, jnp.clip(lag, 0, t - 1)], 0.0)
    kb = jnp.where((lag <= 0)[None, :, :, None, None], kfl[1][:, jnp.clip(-lag, 0, t - 1)], 0.0)
    eye = (lag == 0)[None, :, :, None, None] * jnp.eye(h, dtype=F32)[None, None, None]
    m = kf + kb + eye * d_skip.reshape(groups, 1, 1, 1, h)
    mt = m.transpose(0, 2, 4, 1, 3).reshape(groups, t * h, t * h)
    bp_f = bg[0][:, t - 1 - ti].reshape(groups, t * h, 2 * p)
    bp_b = bg[1][:, ti].reshape(groups, t * h, 2 * p)
    bp = jnp.concatenate([bp_f, bp_b], axis=-1)
    cp_f = cg[0][:, ti + 1].reshape(groups, t * h, 2 * p)
    cp_b = cg[1][:, t - ti].reshape(groups, t * h, 2 * p)
    cp = jnp.concatenate([cp_f, cp_b], axis=-1).swapaxes(1, 2)
    return mt.astype(BF16), bp.astype(BF16), cp.astype(BF16), a16.reshape(2, groups, 1, 2 * p)


def _cplx_coef(a):
    p = a.shape[1] // 2
    lane = lax.broadcasted_iota(jnp.int32, a.shape, 1)
    sw = pltpu.roll(a, p, axis=1)
    return jnp.where(lane < p, a, sw), jnp.where(lane < p, -sw, a)


def _cplx_mul(a, x):
    c1, c2 = _cplx_coef(a)
    return c1 * x + c2 * pltpu.roll(x, x.shape[1] // 2, axis=1)


def _chunk_scan(x, a, reverse):
    n = x.shape[0]
    row = lax.broadcasted_iota(jnp.int32, (n, 1), 0)
    o = 1
    while o < n:
        if reverse:
            sh = jnp.where(row < n - o, pltpu.roll(x, n - o, axis=0), 0.0)
        else:
            sh = jnp.where(row >= o, pltpu.roll(x, o, axis=0), 0.0)
        x = x + _cplx_mul(a, sh)
        a = _cplx_mul(a, a)
        o *= 2
    return x


S5_SUPER = 16
S5_IN_FLIGHT = 4


def _entering_states(v_ref, sin_ref, a, h0, reverse):
    k_n = S5_SUPER
    ns = v_ref.shape[0] // k_n
    half = a.shape[1] // 2
    order = list(range(k_n - 1, -1, -1)) if reverse else list(range(k_n))
    rows_at = lambda k: pl.ds(k, ns, stride=k_n)
    pows = [None, a]
    for _ in range(k_n - 1):
        pows.append(_cplx_mul(a, pows[-1]))
    c1, c2 = _cplx_coef(a)
    loc = {}
    s = ss = None
    for k in order:
        x = v_ref[rows_at(k), :]
        xs = pltpu.roll(x, half, axis=1)
        s, ss = (x, xs) if s is None else (c1 * s + c2 * ss + x, c1 * ss - c2 * s + xs)
        loc[k] = s
    row = lax.broadcasted_iota(jnp.int32, (ns, 1), 0)
    edge = ns - 1 if reverse else 0
    ends = loc[order[-1]] + jnp.where(row == edge, _cplx_mul(pows[k_n], h0), 0.0)
    true_ends = _chunk_scan(ends, pows[k_n], reverse)
    cin = jnp.where(row == edge, h0, pltpu.roll(true_ends, ns - 1 if reverse else 1, axis=0))
    sin_ref[rows_at(order[0]), :] = cin
    for i in range(1, k_n):
        sin_ref[rows_at(order[i]), :] = loc[order[i - 1]] + _cplx_mul(pows[i], cin)


S5_GPB = 128 // S5_GROUP
S5_TPT = 128 // S5_GROUP


def _chunk_perm():
    n = S5_TPT * S5_GPB * S5_GROUP
    i = jnp.arange(n)
    t, g, h = i // (S5_GPB * S5_GROUP), (i // S5_GROUP) % S5_GPB, i % S5_GROUP
    dst = g * (S5_TPT * S5_GROUP) + t * S5_GROUP + h
    return jnp.zeros((n, n), BF16).at[i, dst].set(1.0)


def _load_chunks(src_ref, perm, dst_scr):
    nc = src_ref.shape[0] // S5_CHUNK
    halves = []
    for r in range(S5_CHUNK // S5_TPT):
        xcat = jnp.concatenate(
            [src_ref[pl.ds(r * S5_TPT + t, nc, stride=S5_CHUNK), :].astype(BF16) for t in range(S5_TPT)], axis=1)
        halves.append(jnp.dot(xcat, perm, preferred_element_type=F32).astype(BF16))
    for g in range(S5_GPB):
        dst_scr[g] = jnp.concatenate([hv[:, g * 128:(g + 1) * 128] for hv in halves], axis=1)


def _s5_kernel(u_ref, uc_ref, perm_ref, permt_ref, mt_ref, bp_ref, cp_ref, af_ref, ab_ref, y_ref,
               u_scr, uc_scr, y_scr, v_scr, sin_scr):
    nc = u_ref.shape[0] // S5_CHUNK
    _load_chunks(u_ref, perm_ref[...], u_scr)
    _load_chunks(uc_ref, perm_ref[...], uc_scr)

    def group(g, _):
        y_scr[g] = _s5_group(u_scr[g], uc_scr[g], mt_ref[g], bp_ref[g], cp_ref[g], af_ref[g], ab_ref[g],
                             v_scr.at[g % S5_IN_FLIGHT], sin_scr.at[g % S5_IN_FLIGHT])
        return 0

    lax.fori_loop(0, S5_GPB, group, 0, unroll=S5_IN_FLIGHT)

    for r in range(S5_CHUNK // S5_TPT):
        ycat = jnp.concatenate([y_scr[g][:, r * 128:(r + 1) * 128] for g in range(S5_GPB)], axis=1)
        back = jnp.dot(ycat, permt_ref[...], preferred_element_type=F32)
        for t in range(S5_TPT):
            y_ref[pl.ds(r * S5_TPT + t, nc, stride=S5_CHUNK), :] = back[:, t * 128:(t + 1) * 128]


def _s5_group(u, uc, mt, bp, cp, af, ab, v_scr, sin_scr):
    ps = af.shape[1]
    v = jnp.dot(u, bp, preferred_element_type=F32)
    v_scr[0], v_scr[1] = v[:, :ps], v[:, ps:]
    vc = jnp.dot(uc, bp, preferred_element_type=F32)
    ncc = vc.shape[0]
    hcf = _chunk_scan(vc[:, :ps], af, False)[ncc - 1:ncc]
    hcb = _chunk_scan(vc[:, ps:], ab, True)[0:1]
    _entering_states(v_scr.at[0], sin_scr.at[0], af, hcf, False)
    _entering_states(v_scr.at[1], sin_scr.at[1], ab, hcb, True)
    s_in = jnp.concatenate([sin_scr[0], sin_scr[1]], axis=1).astype(BF16)
    y = (jnp.dot(u, mt, preferred_element_type=F32)
         + jnp.dot(s_in, cp, preferred_element_type=F32))
    return y.astype(BF16)


def _s5(zu, zu_c, mt, bp, cp, a16):
    bsz, length, w = zu.shape
    lctx = zu_c.shape[1]
    nc, ncc = length // S5_CHUNK, lctx // S5_CHUNK
    cw = S5_CHUNK * S5_GROUP
    ps = a16.shape[-1]
    perm = _chunk_perm()
    once = lambda shape, imap: pl.BlockSpec(shape, imap, pipeline_mode=pl.Buffered(1))
    wspec = pl.BlockSpec((S5_GPB, cw, cw), lambda b, q: (q, 0, 0))
    return pl.pallas_call(
        _s5_kernel,
        grid=(bsz, w // 128),
        in_specs=[once((None, length, 128), lambda b, q: (b, 0, q)),
                  pl.BlockSpec((None, lctx, 128), lambda b, q: (b, 0, q)),
                  once(perm.shape, lambda b, q: (0, 0)),
                  once(perm.shape, lambda b, q: (0, 0)),
                  wspec, wspec, wspec,
                  pl.BlockSpec((None, S5_GPB, 1, ps), lambda b, q: (0, q, 0, 0)),
                  pl.BlockSpec((None, S5_GPB, 1, ps), lambda b, q: (1, q, 0, 0))],
        out_specs=once((None, length, 128), lambda b, q: (b, 0, q)),
        out_shape=jax.ShapeDtypeStruct((bsz, length, w), F32),
        scratch_shapes=[pltpu.VMEM((S5_GPB, nc, cw), BF16),
                        pltpu.VMEM((S5_GPB, ncc, cw), BF16),
                        pltpu.VMEM((S5_GPB, nc, cw), BF16),
                        pltpu.VMEM((S5_IN_FLIGHT, 2, nc, ps), F32),
                        pltpu.VMEM((S5_IN_FLIGHT, 2, nc, ps), F32)],
        compiler_params=_params(("arbitrary", "arbitrary")),
        name="s5",
    )(zu, zu_c, perm, perm.T, mt, bp, cp, a16, a16)


LRU_ROW_BLOCK = 16


def _lru_kernel(v_ref, cw_ref, cb_ref, wr_ref, br_ref, wi_ref, bi_ref, lam_ref, h0_ref, o_ref,
                vpad, a_scr, b_scr, carry_scr, *, rows, rev, chain):
    ct = v_ref.shape[-1]
    cpt = COLS_PER_TILE
    rb = LRU_ROW_BLOCK
    nblk = rows // rb
    nslab = ct // MXU_TILE

    if chain:
        @pl.when(pl.program_id(2) == 0)
        def _():
            carry_scr[...] = h0_ref[...]

    zero_row = jnp.zeros((cpt, ct), F32)
    for k in range(CONV_PAD_LO):
        vpad[k] = zero_row
    for k in range(CONV_W - 1 - CONV_PAD_LO):
        vpad[CONV_PAD_LO + rows + k] = zero_row

    def copy(i, _):
        r0 = pl.multiple_of(i * rb, rb)
        vpad[pl.ds(r0 + CONV_PAD_LO, rb)] = v_ref[pl.ds(r0, rb)]
        return 0

    lax.fori_loop(0, nblk, copy, 0)

    cw = cw_ref[...]
    cb = cb_ref[...]
    nl = -lam_ref[...]
    softplus = jnp.maximum(nl, 0.0) + jnp.log1p(jnp.exp(-jnp.abs(nl)))
    c8 = -LRU_C * softplus
    b_r, b_i = br_ref[...], bi_ref[...]

    def block(i, carry):
        acc_a, acc_h = carry
        bi_ = (nblk - 1 - i) if rev else i
        r0 = pl.multiple_of(bi_ * rb, rb)
        x = cb
        for k in range(CONV_W):
            x = x + cw[k:k + 1] * vpad[pl.ds(r0 + k, rb)].reshape(rb * cpt, ct)
        xb = x.astype(BF16)
        pre_r = jnp.concatenate(
            [jnp.dot(xb[:, s * MXU_TILE:(s + 1) * MXU_TILE], wr_ref[s], preferred_element_type=F32)
             for s in range(nslab)], axis=1)
        pre_i = jnp.concatenate(
            [jnp.dot(xb[:, s * MXU_TILE:(s + 1) * MXU_TILE], wi_ref[s], preferred_element_type=F32)
             for s in range(nslab)], axis=1)
        r = jax.nn.sigmoid(pre_r + b_r)
        ig = jax.nn.sigmoid(pre_i + b_i)
        log_a = c8 * r
        a = jnp.exp(log_a)
        z = 1.0 - a * a
        bx = jnp.where(z > 0.0, z * lax.rsqrt(z), 0.0) * (ig * x)
        q0 = pl.multiple_of(r0 * cpt, rb * cpt)
        a_scr[pl.ds(q0, rb * cpt), :] = a
        b_scr[pl.ds(q0, rb * cpt), :] = bx
        order = range(rb - 1, -1, -1) if rev else range(rb)
        for j in order:
            aj = a[j * cpt:(j + 1) * cpt]
            acc_h = aj * acc_h + bx[j * cpt:(j + 1) * cpt]
            acc_a = acc_a * aj
        return acc_a, acc_h

    acc_a, acc_h = lax.fori_loop(0, nblk, block, (jnp.ones((cpt, ct), F32), jnp.zeros((cpt, ct), F32)))

    if not chain:
        o_ref[...] = acc_h
        return

    sub = lax.broadcasted_iota(jnp.int32, (cpt, ct), 0)
    carry = carry_scr[...]
    h_in = jnp.zeros((cpt, ct), F32)
    for s in (range(cpt - 1, -1, -1) if rev else range(cpt)):
        h_in = jnp.where(sub == s, carry, h_in)
        carry = acc_a[s:s + 1] * carry + acc_h[s:s + 1]
    carry_scr[...] = carry

    def row(i, h):
        r = (rows - 1 - i) if rev else i
        q = pl.multiple_of(r * cpt, cpt)
        h = a_scr[pl.ds(q, cpt), :] * h + b_scr[pl.ds(q, cpt), :]
        o_ref[r] = h
        return h

    lax.fori_loop(0, rows, row, h_in, unroll=8)


def _block_diag(w, per):
    hh, n, _ = w.shape
    eye = jnp.eye(per, dtype=w.dtype)
    return jnp.einsum('gpij,pq->gpiqj', w.reshape(hh // per, per, n, n), eye).reshape(hh // per, per * n, per * n)


def _lru(v5, conv_w, conv_b, lam, w_r, b_r, w_i, b_i, h0, *, rev, chain):
    bsz, rows, ncg, cpt, w = v5.shape
    ct = 512
    nct = w // ct
    per = MXU_TILE // (w // LRU_HEADS)
    wr = _block_diag(w_r, per).astype(BF16)
    wi = _block_diag(w_i, per).astype(BF16)
    nslab = ct // MXU_TILE
    cgi = (lambda c: ncg - 1 - c) if rev else (lambda c: c)
    vec = lambda n: pl.BlockSpec((n, ct), lambda b, k, c: (0, k))
    in_specs = [pl.BlockSpec((None, rows, None, cpt, ct), lambda b, k, c: (b, 0, cgi(c), 0, k)),
                vec(CONV_W), vec(1),
                pl.BlockSpec((nslab, MXU_TILE, MXU_TILE), lambda b, k, c: (k, 0, 0)), vec(1),
                pl.BlockSpec((nslab, MXU_TILE, MXU_TILE), lambda b, k, c: (k, 0, 0)), vec(1),
                vec(1),
                pl.BlockSpec((None, 1, ct), lambda b, k, c: (b, 0, k))]
    if chain:
        out_spec = pl.BlockSpec((None, rows, None, cpt, ct), lambda b, k, c: (b, 0, cgi(c), 0, k))
        out_shape = jax.ShapeDtypeStruct(v5.shape, F32)
    else:
        out_spec = pl.BlockSpec((None, None, cpt, ct), lambda b, k, c: (b, c, 0, k))
        out_shape = jax.ShapeDtypeStruct((bsz, ncg, cpt, w), F32)
    kern = functools.partial(_lru_kernel, rows=rows, rev=rev, chain=chain)
    return pl.pallas_call(
        kern,
        grid=(bsz, nct, ncg),
        in_specs=in_specs,
        out_specs=out_spec,
        out_shape=out_shape,
        scratch_shapes=[pltpu.VMEM((rows + CONV_W - 1, cpt, ct), F32),
                        pltpu.VMEM((rows * cpt, ct), F32),
                        pltpu.VMEM((rows * cpt, ct), F32),
                        pltpu.VMEM((1, ct), F32)],
        compiler_params=_params(("arbitrary", "arbitrary", "arbitrary")),
        name="lru_rev" if rev else "lru_fwd",
    )(v5, conv_w, conv_b.reshape(1, w), wr, b_r.reshape(1, w), wi, b_i.reshape(1, w),
      lam.reshape(1, w), h0)


def _merge_kernel(x_ref, ys_ref, ylf_ref, ylb_ref, zg_ref, gate_ref, sh_ref, sc_ref, g2_ref,
                  wglu_ref, wa_ref, wb_ref, wo_ref, h1_ref, ft_ref, *, w_lru, d):
    za = jax.nn.gelu(ys_ref[...])
    ya = (za * jax.nn.sigmoid(jnp.dot(za.astype(BF16), wglu_ref[...],
                                      preferred_element_type=F32))).astype(BF16)
    zg = zg_ref[...]
    yb = ((ylf_ref[...] + ylb_ref[...]) * jax.nn.gelu(zg[:, :w_lru].astype(F32))).astype(BF16)
    ga = jax.nn.sigmoid(zg[:, w_lru:w_lru + d].astype(F32))
    gb = jax.nn.sigmoid(zg[:, w_lru + d:].astype(F32))
    m = (ga * jnp.dot(ya, wa_ref[...], preferred_element_type=F32)
         + gb * jnp.dot(yb, wb_ref[...], preferred_element_type=F32))
    o = jnp.dot(m.astype(BF16), wo_ref[...], preferred_element_type=F32)
    h1 = x_ref[...] + gate_ref[...] * o
    h1_ref[...] = h1
    f = _rms(h1, g2_ref[...]) * (1.0 + sc_ref[...]) + sh_ref[...]
    ft_ref[...] = f.T.astype(BF16)


def _merge(x, ys5, ylf, ylb, zg, mod6, g2, w_glu, w_a, w_b, w_o):
    bsz, length, d = x.shape
    w_s5 = ys5.shape[-1]
    w_lru = ylf.shape[-1]
    tm = 256
    nt = length // tm
    row = lambda k: (lambda b, i: (b * 6 + k, 0, 0))
    tok = lambda w: pl.BlockSpec((None, tm, w), lambda b, i: (b, i, 0))
    const = lambda shape: pl.BlockSpec(shape, lambda b, i: (0, 0), pipeline_mode=pl.Buffered(1))
    kern = functools.partial(_merge_kernel, w_lru=w_lru, d=d)
    return pl.pallas_call(
        kern,
        grid=(bsz, nt),
        in_specs=[tok(d), tok(w_s5), tok(w_lru), tok(w_lru), tok(zg.shape[-1]),
                  pl.BlockSpec((None, 1, d), row(2)), pl.BlockSpec((None, 1, d), row(3)),
                  pl.BlockSpec((None, 1, d), row(4)),
                  pl.BlockSpec((1, d), lambda b, i: (0, 0)),
                  const(w_glu.shape), const(w_a.shape), const(w_b.shape), const(w_o.shape)],
        out_specs=[tok(d), pl.BlockSpec((d, tm), lambda b, i: (0, b * nt + i))],
        out_shape=[jax.ShapeDtypeStruct((bsz, length, d), F32),
                   jax.ShapeDtypeStruct((d, bsz * length), BF16)],
        compiler_params=_params(("arbitrary", "arbitrary")),
        name="merge",
    )(x, ys5, ylf, ylb, zg, mod6, mod6, mod6, g2.reshape(1, d), w_glu, w_a, w_b, w_o)


def _sort_pairs(n):
    out = []
    p = 1
    while p < n:
        k = p
        while k >= 1:
            for j in range(k % p, n - k, 2 * k):
                for i in range(min(k, n - j - k)):
                    if (i + j) // (2 * p) == (i + j + k) // (2 * p):
                        out.append((i + j, i + j + k))
            k //= 2
        p *= 2
    return out


def _top_rows_sorted(x, k):
    n_slab = x.shape[0] // 8
    slabs = [x[8 * r:8 * r + 8] for r in range(n_slab)]
    for a, b in _sort_pairs(n_slab):
        slabs[a], slabs[b] = jnp.maximum(slabs[a], slabs[b]), jnp.minimum(slabs[a], slabs[b])
    slabs.append(jnp.full_like(slabs[0], -jnp.inf))
    out = []
    for it in range(k):
        m = jnp.max(slabs[0], axis=0, keepdims=True)
        out.append(m)
        hit = slabs[0] == m
        for r in range(min(k - 1 - it, n_slab)):
            slabs[r] = jnp.where(hit, slabs[r + 1], slabs[r])
    return out


def _count_above(rows, x, strict):
    assert PEER_TOPK == 16
    above = (lambda a: a > x) if strict else (lambda a: a >= x)
    sel = jnp.where
    b3 = above(rows[7])
    b2 = above(sel(b3, rows[11], rows[3]))
    b1 = above(sel(b3, sel(b2, rows[13], rows[9]), sel(b2, rows[5], rows[1])))
    b0 = above(sel(b3, sel(b2, sel(b1, rows[14], rows[12]), sel(b1, rows[10], rows[8])),
                   sel(b2, sel(b1, rows[6], rows[4]), sel(b1, rows[2], rows[0]))))
    return (sel(b3, 8.0, 0.0) + sel(b2, 4.0, 0.0) + sel(b1, 2.0, 0.0) + sel(b0, 1.0, 0.0)
            + sel(above(rows[15]), 1.0, 0.0))


_CAND_PAIRS = [(j, k) for j in range(PEER_TOPK + 1) for k in range(PEER_TOPK + 1)
               if (j + 1) * (k + 1) <= PEER_TOPK + 1]
_CAND_ROWS = 64
assert len(_CAND_PAIRS) <= _CAND_ROWS


def _route_kernel(ft_ref, wq_ref, sk_ref, r2_ref, e2_ref, n1_ref, e1_ref, cand_scr):
    half = sk_ref.shape[2]
    tm = ft_ref.shape[1]
    qt = jnp.dot(wq_ref[...], ft_ref[...], preferred_element_type=F32).astype(BF16)
    for r in range(len(_CAND_PAIRS), _CAND_ROWS):
        cand_scr[r:r + 1, :] = jnp.full((1, 128), -jnp.inf, F32)
    for h in range(PEER_HEADS):
        s_full = [jnp.dot(sk_ref[side], qt[(2 * h + side) * half:(2 * h + side + 1) * half],
                          preferred_element_type=F32) for side in range(2)]
        for l in range(tm // 128):
            lanes = slice(l * 128, (l + 1) * 128)
            s = [sf[:, lanes] for sf in s_full]
            top0 = _top_rows_sorted(s[0], PEER_TOPK + 1)
            top1 = _top_rows_sorted(s[1], PEER_TOPK + 1)
            for r, (j, k) in enumerate(_CAND_PAIRS):
                cand_scr[r:r + 1, :] = top0[j] + top1[k]
            best = _top_rows_sorted(cand_scr[...], PEER_TOPK + 1)
            z = jnp.ones_like(best[0])
            for k in range(1, PEER_TOPK):
                z = z + jnp.exp(best[k] - best[0])
            t1 = 0.5 * (best[PEER_TOPK - 1] + best[PEER_TOPK]) - s[0]
            n1 = _count_above(top1, t1, strict=False)
            rank2 = _count_above(top1, s[1], strict=True)
            r2_ref[h, :, lanes] = rank2.astype(BF16)
            e2_ref[h, :, lanes] = jnp.exp(s[1] - top1[0]).astype(BF16)
            n1_ref[h, :, lanes] = n1
            e1_ref[h, :, lanes] = jnp.exp(s[0] - top0[0]) / z


def _route(ft, wq_t, sk):
    d, nt = ft.shape
    nk = sk.shape[1]
    tm = 512
    ospec = pl.BlockSpec((PEER_HEADS, nk, tm), lambda i: (0, 0, i))
    return pl.pallas_call(
        _route_kernel,
        grid=(nt // tm,),
        in_specs=[pl.BlockSpec((d, tm), lambda i: (0, i)),
                  pl.BlockSpec(wq_t.shape, lambda i: (0, 0), pipeline_mode=pl.Buffered(1)),
                  pl.BlockSpec(sk.shape, lambda i: (0, 0, 0))],
        out_specs=[ospec] * 4,
        out_shape=[jax.ShapeDtypeStruct((PEER_HEADS, nk, nt), dt) for dt in (BF16, BF16, F32, F32)],
        scratch_shapes=[pltpu.VMEM((_CAND_ROWS, 128), F32)],
        compiler_params=_params(("arbitrary",)),
        name="route",
    )(ft, wq_t, sk)


def _rows_bf16(row, n):
    tile = jnp.broadcast_to(row, (16, row.shape[1])).astype(BF16)
    return jnp.concatenate([tile] * (n // 16), axis=0)


def _peer_kernel(ft_ref, u_ref, v_ref, r2_ref, e2_ref, n1_ref, e1_ref, h1_ref, gate_ref, gf_ref,
                 o_ref, acc_ref, *, nk):
    j = pl.program_id(1)

    @pl.when(j == 0)
    def _():
        acc_ref[...] = jnp.zeros_like(acc_ref)

    tm = ft_ref.shape[1]
    n_i1 = u_ref.shape[0] // nk
    per = MXU_TILE // nk
    ps = []
    for c in range(n_i1 // per):
        st = jnp.dot(u_ref[c * MXU_TILE:(c + 1) * MXU_TILE, :], ft_ref[...],
                     preferred_element_type=F32)
        for a in range(per):
            i1 = j * n_i1 + c * per + a
            act = jax.nn.gelu(st[a * nk:(a + 1) * nk].astype(BF16))
            w = jnp.zeros((nk, tm), BF16)
            for h in range(PEER_HEADS):
                cnt = _rows_bf16(n1_ref[h, pl.ds(i1, 1), :], nk)
                cf = _rows_bf16(e1_ref[h, pl.ds(i1, 1), :], nk)
                w = w + jnp.where(r2_ref[h] < cnt, e2_ref[h] * cf, jnp.zeros((), BF16))
            ps.append(w * act)
    p = jnp.concatenate(ps, axis=0)
    acc_ref[...] += lax.dot_general(p, v_ref[...], (((0,), (0,)), ((), ())), preferred_element_type=F32)

    @pl.when(j == pl.num_programs(1) - 1)
    def _():
        h2 = h1_ref[...] + gate_ref[...] * acc_ref[...]
        o_ref[...] = _rms(h2, gf_ref[...])


def _peer(ft, u, v, r2, e2, n1, e1, h1, mod6, g_final, tiles_per_batch_fn):
    d, nt = ft.shape
    ne = u.shape[0]
    nk = r2.shape[1]
    tm = 512
    et = 1024
    rspec = pl.BlockSpec((PEER_HEADS, nk, tm), lambda i, j: (0, 0, i), pipeline_mode=pl.Buffered(1))
    kern = functools.partial(_peer_kernel, nk=nk)
    return pl.pallas_call(
        kern,
        grid=(nt // tm, ne // et),
        in_specs=[pl.BlockSpec((d, tm), lambda i, j: (0, i)),
                  pl.BlockSpec((et, d), lambda i, j: (j, 0)),
                  pl.BlockSpec((et, d), lambda i, j: (j, 0)),
                  rspec, rspec, rspec, rspec,
                  pl.BlockSpec((tm, d), lambda i, j: (i, 0), pipeline_mode=pl.Buffered(1)),
                  pl.BlockSpec((None, 1, d), lambda i, j: (tiles_per_batch_fn(i, tm) * 6 + 5, 0, 0)),
                  pl.BlockSpec((1, d), lambda i, j: (0, 0))],
        out_specs=pl.BlockSpec((tm, d), lambda i, j: (i, 0), pipeline_mode=pl.Buffered(1)),
        out_shape=jax.ShapeDtypeStruct((nt, d), F32),
        scratch_shapes=[pltpu.VMEM((tm, d), F32)],
        compiler_params=_params(("arbitrary", "arbitrary")),
        name="peer",
    )(ft, u, v, r2, e2, n1, e1, h1, mod6, g_final.reshape(1, d))


def kernel(x, c, ctx, c_ctx, w_ada, b_ada, g_norm1, w_in, s5_a_re, s5_a_im, s5_log_dt, s5_b_re, s5_b_im, s5_c_re, s5_c_im, s5_d, s5_w_glu, lru_conv_w, lru_conv_b, lru_lambda, lru_w_r, lru_b_r, lru_w_i, lru_b_i, w_proj_a, w_proj_b, w_out, g_norm2, peer_w_query, peer_sub_keys, peer_u, peer_v, g_final):
    bsz, length, d = x.shape
    assert w_ada.shape[0] == 1, "single-layer kernel"
    rows = length // GRID_W
    w_s5 = s5_d.shape[-1]
    w_lru = lru_lambda.shape[-1]
    assert bsz < 8 and length % (GRID_W * LRU_ROW_BLOCK) == 0 and ctx.shape[1] % LRU_ROW_BLOCK == 0

    cin = jnp.zeros((8, d), F32).at[:bsz].set(c).at[bsz].set(c_ctx)
    mod6 = _ada(cin, w_ada[0], b_ada[0]).reshape(8 * 6, 1, d)

    w_in_b = w_in[0].astype(BF16)
    zu, zv, zg = _inproj(x, g_norm1[0], mod6, lambda b: b, w_in_b, w_s5, w_lru)
    zu_c, zv_c = _inproj(ctx, g_norm1[0], mod6, lambda b: bsz, w_in_b[:, :w_s5 + w_lru], w_s5, w_lru)

    mt, bp, cp, a16 = _s5_operators(s5_a_re[0], s5_a_im[0], s5_log_dt[0], s5_b_re[0], s5_b_im[0],
                                    s5_c_re[0], s5_c_im[0], s5_d[0])
    ys5 = _s5(zu, zu_c, mt, bp, cp, a16)

    lctx = ctx.shape[1]
    vc5 = jnp.zeros((1, lctx, 1, COLS_PER_TILE, w_lru), F32).at[0, :, 0, :bsz].set(zv_c.transpose(1, 0, 2))
    v5 = zv.reshape(bsz, rows, GRID_W // COLS_PER_TILE, COLS_PER_TILE, w_lru)
    zero_h = jnp.zeros((1, 1, w_lru), F32)
    yl = []
    for dr in range(2):
        args = (lru_conv_w[0], lru_conv_b[0], lru_lambda[0, dr], lru_w_r[0, dr], lru_b_r[0, dr],
                lru_w_i[0, dr], lru_b_i[0, dr])
        hc = _lru(vc5, *args, zero_h, rev=dr == 1, chain=False)
        h0 = hc[0, 0, :bsz].reshape(bsz, 1, w_lru)
        yl.append(_lru(v5, *args, h0, rev=dr == 1, chain=True).reshape(bsz, length, w_lru))

    h1, ft = _merge(x, ys5, yl[0], yl[1], zg, mod6, g_norm2[0], s5_w_glu[0].astype(BF16),
                    w_proj_a[0].astype(BF16), w_proj_b[0].astype(BF16), w_out[0].astype(BF16))

    r2, e2, n1, e1 = _route(ft, peer_w_query[0].T.astype(BF16), peer_sub_keys[0].astype(BF16))
    out = _peer(ft, peer_u[0].astype(BF16), peer_v[0].astype(BF16), r2, e2, n1, e1,
                h1.reshape(bsz * length, d), mod6, g_final,
                lambda i, tm: i // (length // tm))
    return out.reshape(bsz, length, d)
```
